```python
import math
import jax
import jax.numpy as jnp
from jax import lax
import numpy as np

D_MODEL = 1024
BATCH = 2
SEQ = 8192
DEPTH = 2

D_MIX = D_MODEL
ATT_HEADS = 8
ATT_HEAD_DIM = 64
ATT_WIDTH = ATT_HEADS * ATT_HEAD_DIM
MOBA_BLOCK = 256
MOBA_TOPK = 3
Q_CHUNK = 128
ROPE_THETA = 500000.0
ROPE_DIM = ATT_HEAD_DIM // 4
ML_HEADS = 4
ML_WIDTH = D_MIX - ATT_WIDTH
ML_V_DIM = ML_WIDTH // ML_HEADS
ML_QK_DIM = ML_V_DIM // 2
ML_QK_WIDTH = ML_HEADS * ML_QK_DIM
ML_CHUNK = 128
CONV_WIDTH = 4
N_IN = 3 * ATT_WIDTH + 2 * ML_QK_WIDTH + 2 * ML_WIDTH + 2 * ML_HEADS
D_FF = 2816
N_EXPERTS = 8
TOP_K = 2
D_EXPERT = 3584
N_DENSE = (DEPTH + 1) // 2
N_MOE = DEPTH // 2
LN_EPS = 1e-5
RMS_EPS = 1e-6

kernel_name = "hymba_moba_mlstm_deepnorm_moe"


def layer_norm(x, g, b):
    xf = x.astype(jnp.float32)
    mu = xf.mean(-1, keepdims=True)
    var = jnp.square(xf - mu).mean(-1, keepdims=True)
    return ((xf - mu) * lax.rsqrt(var + LN_EPS) * g.astype(jnp.float32) + b.astype(jnp.float32)).astype(x.dtype)


def head_rmsnorm(h, g, out_dtype):
    B, S, H, Dh = h.shape
    hf = h.astype(jnp.float32)
    hf = hf * lax.rsqrt(jnp.mean(jnp.square(hf), -1, keepdims=True) + RMS_EPS)
    return (hf.reshape(B, S, H * Dh) * g.astype(jnp.float32)).astype(out_dtype)


def partial_rope(x, positions):
    half = ROPE_DIM // 2
    inv_freq = ROPE_THETA ** (-jnp.arange(half, dtype=jnp.float32) / half)
    ang = positions.astype(jnp.float32)[:, None, :, None] * inv_freq
    cos, sin = jnp.cos(ang), jnp.sin(ang)
    xf = x.astype(jnp.float32)
    x1, x2, rest = xf[..., :half], xf[..., half:ROPE_DIM], xf[..., ROPE_DIM:]
    out = jnp.concatenate([x1 * cos - x2 * sin, x2 * cos + x1 * sin, rest], -1)
    return out.astype(x.dtype)


def moba_attention(q, k, v):
    B, H, S, Dh = q.shape
    nb = -(-S // MOBA_BLOCK)
    pad = nb * MOBA_BLOCK - S
    kp = jnp.pad(k, ((0, 0), (0, 0), (0, pad), (0, 0)))
    vp = jnp.pad(v, ((0, 0), (0, 0), (0, pad), (0, 0)))
    kb = kp.reshape(B, H, nb, MOBA_BLOCK, Dh)
    vb = vp.reshape(B, H, nb, MOBA_BLOCK, Dh)
    k_mean = kb.astype(jnp.float32).mean(3)
    topk = min(MOBA_TOPK, nb)
    scale = Dh ** -0.5
    bi = jnp.arange(B)[:, None, None, None]
    hi = jnp.arange(H)[None, :, None, None]
    block_ids = jnp.arange(nb)

    def one_chunk(c):
        q0 = c * Q_CHUNK
        blk = q0 // MOBA_BLOCK
        qf = lax.dynamic_slice_in_dim(q, q0, Q_CHUNK, axis=2).astype(jnp.float32)
        gate = jnp.einsum('bhqd,bhnd->bhqn', qf, k_mean)
        gate = jnp.where(block_ids < blk, gate, -jnp.inf)
        _, sel = lax.top_k(gate, topk)
        sel_ok = sel < blk
        k_sel = kb[bi, hi, sel].astype(jnp.float32)
        v_sel = vb[bi, hi, sel].astype(jnp.float32)
        s_sel = jnp.einsum('bhqd,bhqknd->bhqkn', qf, k_sel) * scale
        s_sel = jnp.where(sel_ok[..., None], s_sel, -jnp.inf)
        k_own = lax.dynamic_index_in_dim(kb, blk, axis=2, keepdims=False).astype(jnp.float32)
        v_own = lax.dynamic_index_in_dim(vb, blk, axis=2, keepdims=False).astype(jnp.float32)
        s_own = jnp.einsum('bhqd,bhnd->bhqn', qf, k_own) * scale
        q_pos = q0 + jnp.arange(Q_CHUNK)
        k_pos = blk * MOBA_BLOCK + jnp.arange(MOBA_BLOCK)
        s_own = jnp.where(k_pos[None, :] <= q_pos[:, None], s_own, -jnp.inf)
        n_sel = topk * MOBA_BLOCK
        p = jax.nn.softmax(jnp.concatenate([s_sel.reshape(B, H, Q_CHUNK, n_sel), s_own], -1), axis=-1)
        p_sel = p[..., :n_sel].reshape(B, H, Q_CHUNK, topk, MOBA_BLOCK)
        p_own = p[..., n_sel:]
        out = (jnp.einsum('bhqkn,bhqknd->bhqd', p_sel, v_sel)
               + jnp.einsum('bhqn,bhnd->bhqd', p_own, v_own))
        return out.astype(q.dtype)

    outs = lax.map(one_chunk, jnp.arange(S // Q_CHUNK))
    return jnp.moveaxis(outs, 0, 2).reshape(B, H, S, Dh)


def causal_depthwise_conv(x, w, b):
    S = x.shape[1]
    xp = jnp.pad(x, ((0, 0), (CONV_WIDTH - 1, 0), (0, 0)))
    y = b + w[0] * xp[:, 0:S]
    for j in range(1, CONV_WIDTH):
        y = y + w[j] * xp[:, j:j + S]
    return y


def mlstm_chunkwise(q, k, v, i_pre, f_pre):
    B, H, S, Dk = q.shape
    Dv = v.shape[-1]
    L = ML_CHUNK
    nc = S // L
    qf = q.astype(jnp.float32) * (Dk ** -0.5)
    kf = k.astype(jnp.float32)
    vf = v.astype(jnp.float32)
    ig = i_pre.astype(jnp.float32)
    logf = jax.nn.log_sigmoid(f_pre.astype(jnp.float32))

    def chunks(t):
        return jnp.moveaxis(t.reshape(B, H, nc, L, *t.shape[3:]), 2, 0)

    causal = jnp.tril(jnp.ones((L, L), dtype=bool))

    def step(carry, xs):
        C, n, m = carry
        qc, kc, vc, ic, fc = xs
        b = jnp.cumsum(fc, axis=-1)
        d_log = jnp.where(causal, b[..., :, None] - b[..., None, :] + ic[..., None, :], -jnp.inf)
        inter = b + m[..., None]
        m_t = jnp.maximum(inter, d_log.max(-1))
        w_intra = jnp.exp(d_log - m_t[..., None])
        w_inter = jnp.exp(inter - m_t)
        s = jnp.einsum('bhtd,bhsd->bhts', qc, kc) * w_intra
        num = (jnp.einsum('bhts,bhsv->bhtv', s, vc)
               + w_inter[..., None] * jnp.einsum('bhvd,bhtd->bhtv', C, qc))
        den = s.sum(-1) + w_inter * jnp.einsum('bhd,bhtd->bht', n, qc)
        h = num / jnp.maximum(jnp.abs(den), jnp.exp(-m_t))[..., None]
        b_last = b[..., -1]
        decay = b_last[..., None] - b + ic
        m_new = jnp.maximum(b_last + m, decay.max(-1))
        w_state = jnp.exp(decay - m_new[..., None])
        carry_scale = jnp.exp(b_last + m - m_new)
        C_new = carry_scale[..., None, None] * C + jnp.einsum('bhs,bhsv,bhsd->bhvd', w_state, vc, kc)
        n_new = carry_scale[..., None] * n + jnp.einsum('bhs,bhsd->bhd', w_state, kc)
        return (C_new, n_new, m_new), h

    init = (jnp.zeros((B, H, Dv, Dk), jnp.float32), jnp.zeros((B, H, Dk), jnp.float32),
            jnp.zeros((B, H), jnp.float32))
    _, hs = lax.scan(step, init, (chunks(qf), chunks(kf), chunks(vf), chunks(ig), chunks(logf)))
    return jnp.moveaxis(hs, 0, 2).reshape(B, H, S, Dv)


def hybrid_mixer(x, positions, w_in, gate_bias, conv_w, conv_b, norm_attn, norm_mlstm, w_out):
    B, S, _ = x.shape
    proj = x @ w_in
    cuts = [int(c) for c in np.cumsum([ATT_WIDTH, ATT_WIDTH, ATT_WIDTH, 2 * ML_QK_WIDTH, ML_WIDTH, ML_WIDTH])]
    aq, ak, av, mqk, mv, mo, mg = jnp.split(proj, cuts, axis=-1)

    def att_heads(t):
        return t.reshape(B, S, ATT_HEADS, ATT_HEAD_DIM).transpose(0, 2, 1, 3)
    qa = partial_rope(att_heads(aq), positions)
    ka = partial_rope(att_heads(ak), positions)
    att = moba_attention(qa, ka, att_heads(av))
    att_out = head_rmsnorm(att.transpose(0, 2, 1, 3), norm_attn, x.dtype)

    mqk = jax.nn.silu(causal_depthwise_conv(mqk, conv_w, conv_b))
    mq, mk = mqk[..., :ML_QK_WIDTH], mqk[..., ML_QK_WIDTH:]
    def ml_heads(t, d):
        return t.reshape(B, S, ML_HEADS, d).transpose(0, 2, 1, 3)
    gates = (mg + gate_bias).transpose(0, 2, 1)
    h_ml = mlstm_chunkwise(ml_heads(mq, ML_QK_DIM), ml_heads(mk, ML_QK_DIM), ml_heads(mv, ML_V_DIM),
                           gates[:, :ML_HEADS], gates[:, ML_HEADS:])
    ml_out = head_rmsnorm(h_ml.transpose(0, 2, 1, 3), norm_mlstm, x.dtype) * jax.nn.sigmoid(mo)

    return jnp.concatenate([att_out, ml_out], axis=-1) @ w_out


def swiglu(x, w_gate, w_up, w_down):
    return (jax.nn.silu(x @ w_gate) * (x @ w_up)) @ w_down


def moe_swiglu(x, router_w, router_b, w_gate, w_up, w_down):
    logits = (x @ router_w).astype(jnp.float32) + router_b.astype(jnp.float32)
    top_val, top_idx = lax.top_k(logits, TOP_K)
    top_w = jax.nn.softmax(top_val, axis=-1)
    combine = jnp.sum(jax.nn.one_hot(top_idx, N_EXPERTS, dtype=jnp.float32) * top_w[..., None], axis=-2)
    combine = combine.astype(x.dtype)
    y = jnp.zeros_like(x)
    for e in range(N_EXPERTS):
        y = y + combine[..., e:e + 1] * swiglu(x, w_gate[e], w_up[e], w_down[e])
    return y


def setup_inputs(seed: int = 0) -> dict:
    key = jax.random.key(seed)
    ks = jax.random.split(key, 24)
    f32 = jnp.float32
    beta = (8.0 * DEPTH) ** -0.25

    def nrm(k, shape, scale):
        return scale * jax.random.normal(k, shape, f32)

    x = nrm(ks[0], (BATCH, SEQ, D_MODEL), 1.0)
    positions = jnp.broadcast_to(jnp.arange(SEQ, dtype=jnp.int32), (BATCH, SEQ))
    w_in = nrm(ks[1], (DEPTH, D_MODEL, N_IN), D_MODEL ** -0.5)
    i_bias = nrm(ks[2], (DEPTH, ML_HEADS), 0.1)
    f_bias = jnp.linspace(3.0, 6.0, ML_HEADS, dtype=f32) + nrm(ks[3], (DEPTH, ML_HEADS), 0.1)
    gate_bias = jnp.concatenate([i_bias, f_bias], axis=-1)
    conv_w = nrm(ks[4], (DEPTH, CONV_WIDTH, 2 * ML_QK_WIDTH), CONV_WIDTH ** -0.5)
    conv_b = nrm(ks[5], (DEPTH, 2 * ML_QK_WIDTH), 0.02)
    norm_attn = 1.0 + nrm(ks[6], (DEPTH, ATT_WIDTH), 0.02)
    norm_mlstm = 1.0 + nrm(ks[7], (DEPTH, ML_WIDTH), 0.02)
    w_out = nrm(ks[8], (DEPTH, D_MIX, D_MODEL), beta * D_MIX ** -0.5)
    ln1_g = 1.0 + nrm(ks[9], (DEPTH, D_MODEL), 0.02)
    ln1_b = nrm(ks[10], (DEPTH, D_MODEL), 0.02)
    dense_w_gate = nrm(ks[11], (N_DENSE, D_MODEL, D_FF), D_MODEL ** -0.5)
    dense_w_up = nrm(ks[12], (N_DENSE, D_MODEL, D_FF), D_MODEL ** -0.5)
    dense_w_down = nrm(ks[13], (N_DENSE, D_FF, D_MODEL), beta * D_FF ** -0.5)
    router_w = nrm(ks[14], (N_MOE, D_MODEL, N_EXPERTS), D_MODEL ** -0.5)
    router_b = nrm(ks[15], (N_MOE, N_EXPERTS), 0.01)
    moe_w_gate = nrm(ks[16], (N_MOE, N_EXPERTS, D_MODEL, D_EXPERT), D_MODEL ** -0.5)
    moe_w_up = nrm(ks[17], (N_MOE, N_EXPERTS, D_MODEL, D_EXPERT), D_MODEL ** -0.5)
    moe_w_down = nrm(ks[18], (N_MOE, N_EXPERTS, D_EXPERT, D_MODEL), beta * D_EXPERT ** -0.5)
    ln2_g = 1.0 + nrm(ks[19], (DEPTH, D_MODEL), 0.02)
    ln2_b = nrm(ks[20], (DEPTH, D_MODEL), 0.02)
    return {"x": x, "positions": positions, "w_in": w_in, "gate_bias": gate_bias,
            "conv_w": conv_w, "conv_b": conv_b, "norm_attn": norm_attn, "norm_mlstm": norm_mlstm,
            "w_out": w_out, "ln1_g": ln1_g, "ln1_b": ln1_b,
            "dense_w_gate": dense_w_gate, "dense_w_up": dense_w_up, "dense_w_down": dense_w_down,
            "router_w": router_w, "router_b": router_b,
            "moe_w_gate": moe_w_gate, "moe_w_up": moe_w_up, "moe_w_down": moe_w_down,
            "ln2_g": ln2_g, "ln2_b": ln2_b}


def reference(x, positions, w_in, gate_bias, conv_w, conv_b, norm_attn, norm_mlstm, w_out,
              ln1_g, ln1_b, dense_w_gate, dense_w_up, dense_w_down, router_w, router_b,
              moe_w_gate, moe_w_up, moe_w_down, ln2_g, ln2_b):
    alpha = (2.0 * DEPTH) ** 0.25
    for l in range(DEPTH):
        mix = hybrid_mixer(x, positions, w_in[l], gate_bias[l], conv_w[l], conv_b[l],
                           norm_attn[l], norm_mlstm[l], w_out[l])
        x = layer_norm(alpha * x + mix, ln1_g[l], ln1_b[l])
        j = l // 2
        if l % 2 == 0:
            ffn = swiglu(x, dense_w_gate[j], dense_w_up[j], dense_w_down[j])
        else:
            ffn = moe_swiglu(x, router_w[j], router_b[j], moe_w_gate[j], moe_w_up[j], moe_w_down[j])
        x = layer_norm(alpha * x + ffn, ln2_g[l], ln2_b[l])
    return x
```

```python
import functools

import jax
import jax.numpy as jnp
from jax import lax
from jax.experimental import pallas as pl
from jax.experimental.pallas import tpu as pltpu

F32 = jnp.float32
BF16 = jnp.bfloat16
I32 = jnp.int32

ATT_HEADS = 8
ATT_HEAD_DIM = 64
ATT_WIDTH = ATT_HEADS * ATT_HEAD_DIM
MOBA_BLOCK = 256
MOBA_TOPK = 3
ROPE_THETA = 500000.0
ROPE_DIM = ATT_HEAD_DIM // 4
ML_HEADS = 4
ML_V_DIM = 128
ML_QK_DIM = 64
ML_QK_WIDTH = ML_HEADS * ML_QK_DIM
ML_WIDTH = ML_HEADS * ML_V_DIM
ML_CHUNK = 128
CONV_WIDTH = 4
TOP_K = 2
LN_EPS = 1e-5
RMS_EPS = 1e-6

LANES = 128
SUBLANES = 8
V7X_VMEM_BYTES = 64 * 1024 * 1024
VMEM_LIMIT = (V7X_VMEM_BYTES * 3) // 4

MASK_BIAS = -1e9
NEG_INF = float("-inf")
HIGHEST = lax.Precision.HIGHEST

_NT = (((1,), (1,)), ((), ()))
_TN = (((0,), (0,)), ((), ()))


def _params(*sem):
    return pltpu.CompilerParams(dimension_semantics=sem, vmem_limit_bytes=VMEM_LIMIT)


def _iota(shape, dim):
    return lax.broadcasted_iota(I32, shape, dim)


def _sigmoid(x):
    return 1.0 / (1.0 + jnp.exp(-x))


def _layer_norm(y, g, b):
    mu = jnp.mean(y, axis=-1, keepdims=True)
    yc = y - mu
    var = jnp.mean(yc * yc, axis=-1, keepdims=True)
    return yc * lax.rsqrt(var + LN_EPS) * g + b


def _rope_tab_kernel(pos_ref, freq_ref, cos_ref, s1_ref, s2_ref):
    ang = pos_ref[...].astype(F32) * freq_ref[...]
    d = _iota(ang.shape, 1) & (ATT_HEAD_DIM - 1)
    half = ROPE_DIM // 2
    s = jnp.sin(ang)
    cos_ref[...] = jnp.cos(ang)
    s1_ref[...] = jnp.where(d < half, -s, 0.0)
    s2_ref[...] = jnp.where((d >= half) & (d < ROPE_DIM), s, 0.0)


def _rope_tables(positions):
    T = positions.size
    tm = min(T, 1024)
    half = ROPE_DIM // 2
    inv_freq = ROPE_THETA ** (-jnp.arange(half, dtype=F32) / half)
    d = jnp.arange(LANES) % ATT_HEAD_DIM
    freq = jnp.where(d < ROPE_DIM, inv_freq[d % half], 0.0).astype(F32)[None, :]
    tab = jax.ShapeDtypeStruct((T, LANES), F32)
    return pl.pallas_call(
        _rope_tab_kernel,
        grid=(T // tm,),
        in_specs=[pl.BlockSpec((tm, 1), lambda i: (i, 0)),
                  pl.BlockSpec((1, LANES), lambda i: (0, 0))],
        out_specs=[pl.BlockSpec((tm, LANES), lambda i: (i, 0))] * 3,
        out_shape=[tab, tab, tab],
        compiler_params=_params("parallel"),
        name="rope_tables",
    )(positions.reshape(T, 1), freq)


def _inproj_kernel(x_ref, w_ref, wg_ref, gb_ref, cos_ref, s1_ref, s2_ref,
                   q_ref, k_ref, v_ref, mqk_ref, mv_ref, mo_ref, g_ref, km_ref):
    xb = x_ref[...].astype(BF16)
    W = ATT_WIDTH

    def proj(c):
        return jnp.dot(xb, w_ref[:, c * W:(c + 1) * W], preferred_element_type=F32)

    rep = W // LANES
    cos = jnp.concatenate([cos_ref[...]] * rep, axis=1)
    s1 = jnp.concatenate([s1_ref[...]] * rep, axis=1)
    s2 = jnp.concatenate([s2_ref[...]] * rep, axis=1)
    half = ROPE_DIM // 2

    def rope(t):
        return t * cos + pltpu.roll(t, W - half, 1) * s1 + pltpu.roll(t, half, 1) * s2

    q_ref[...] = (rope(proj(0)) * (ATT_HEAD_DIM ** -0.5)).astype(BF16)
    k = rope(proj(1))
    k_ref[...] = k.astype(BF16)
    for g in range(k.shape[0] // MOBA_BLOCK):
        blk = k[g * MOBA_BLOCK:(g + 1) * MOBA_BLOCK]
        km_ref[0, g:g + 1, :] = jnp.sum(blk, axis=0, keepdims=True) * (1.0 / MOBA_BLOCK)
    v_ref[...] = proj(2).astype(BF16)
    mqk_ref[...] = proj(3).astype(BF16)
    mv_ref[...] = proj(4).astype(BF16)
    mo_ref[...] = proj(5).astype(BF16)
    g_ref[...] = jnp.dot(xb, wg_ref[...], preferred_element_type=F32) + gb_ref[...]


def _inproj(x2, w_main, w_gate, gate_bias, tabs):
    T, D = x2.shape
    tm = min(T, 512)
    nkb = tm // MOBA_BLOCK
    W = ATT_WIDTH
    row = lambda i: (i, 0)
    const = lambda i: (0, 0)
    act = jax.ShapeDtypeStruct((T, W), BF16)
    return pl.pallas_call(
        _inproj_kernel,
        grid=(T // tm,),
        in_specs=[pl.BlockSpec((tm, D), row),
                  pl.BlockSpec(w_main.shape, const),
                  pl.BlockSpec(w_gate.shape, const),
                  pl.BlockSpec((1, LANES), const),
                  pl.BlockSpec((tm, LANES), row),
                  pl.BlockSpec((tm, LANES), row),
                  pl.BlockSpec((tm, LANES), row)],
        out_specs=[pl.BlockSpec((tm, W), row)] * 6
        + [pl.BlockSpec((tm, LANES), row),
           pl.BlockSpec((1, nkb, W), lambda i: (i, 0, 0))],
        out_shape=[act] * 6 + [jax.ShapeDtypeStruct((T, LANES), F32),
                               jax.ShapeDtypeStruct((T // tm, nkb, W), F32)],
        compiler_params=_params("parallel"),
        name="inproj",
    )(x2, w_main, w_gate, gate_bias, *tabs)


def _attn_kernel(q_ref, k_ref, v_ref, km_ref, gain_ref, o_ref, kaug_ref):
    i = pl.program_id(2)
    tq = q_ref.shape[0]
    nblk = k_ref.shape[0] // MOBA_BLOCK
    lane = _iota((tq, LANES), 1)
    head_a = lane < ATT_HEAD_DIM

    @pl.when(i == 0)
    def _build_augmented_keys():
        kaug_ref[:, :LANES] = k_ref[...]
        blane = _iota((MOBA_BLOCK, LANES), 1)

        def fill(j, carry):
            r0 = pl.multiple_of(j * MOBA_BLOCK, MOBA_BLOCK)
            kaug_ref[pl.ds(r0, MOBA_BLOCK), LANES:] = (blane == j).astype(BF16)
            return carry

        lax.fori_loop(0, nblk, fill, 0)

    q = q_ref[...]
    zero = jnp.zeros_like(q)
    km = km_ref[...]

    def select_bias(qh):
        gate = lax.dot_general(qh.astype(F32), km, _NT, precision=HIGHEST,
                               preferred_element_type=F32)
        gate = jnp.where(lane < i, gate, NEG_INF)
        picked = lane == i
        for _ in range(MOBA_TOPK):
            best = jnp.max(gate, axis=1, keepdims=True)
            first = jnp.min(jnp.where(gate == best, lane, LANES), axis=1, keepdims=True)
            hit = (lane == first) & (best > NEG_INF)
            picked = picked | hit
            gate = jnp.where(hit, NEG_INF, gate)
        return jnp.where(picked, 0.0, MASK_BIAS).astype(BF16)

    qs = []
    for qh in (jnp.where(head_a, q, zero), jnp.where(head_a, zero, q)):
        qs.append(jnp.concatenate([qh, select_bias(qh)], axis=1))

    def scores(qa, kj):
        return lax.dot_general(qa, kj, _NT, preferred_element_type=F32)

    r0 = pl.multiple_of(i * MOBA_BLOCK, MOBA_BLOCK)
    kd = kaug_ref[pl.ds(r0, MOBA_BLOCK), :]
    vd = v_ref[pl.ds(r0, MOBA_BLOCK), :]
    causal = _iota((tq, MOBA_BLOCK), 1) <= _iota((tq, MOBA_BLOCK), 0)
    ms, ls, pvs = [], [], []
    for qa in qs:
        s = jnp.where(causal, scores(qa, kd), NEG_INF)
        m = jnp.max(s, axis=1, keepdims=True)
        p = jnp.exp(s - m)
        ms.append(m)
        ls.append(jnp.sum(p, axis=1, keepdims=True))
        pvs.append(jnp.dot(p.astype(BF16), vd, preferred_element_type=F32))
    acc0 = jnp.where(head_a, pvs[0], pvs[1])

    def body(j, carry):
        m_a, l_a, m_b, l_b, acc = carry
        rj = pl.multiple_of(j * MOBA_BLOCK, MOBA_BLOCK)
        kj = kaug_ref[pl.ds(rj, MOBA_BLOCK), :]
        vj = v_ref[pl.ds(rj, MOBA_BLOCK), :]
        new = []
        for qa, m, l in ((qs[0], m_a, l_a), (qs[1], m_b, l_b)):
            s = scores(qa, kj)
            m_new = jnp.maximum(m, jnp.max(s, axis=1, keepdims=True))
            p = jnp.exp(s - m_new)
            alpha = jnp.exp(m - m_new)
            l_new = alpha * l + jnp.sum(p, axis=1, keepdims=True)
            pv = jnp.dot(p.astype(BF16), vj, preferred_element_type=F32)
            new.append((m_new, l_new, alpha, pv))
        (m_a, l_a, al_a, pv_a), (m_b, l_b, al_b, pv_b) = new
        acc = acc * jnp.where(head_a, al_a, al_b) + jnp.where(head_a, pv_a, pv_b)
        return m_a, l_a, m_b, l_b, acc

    _, l_a, _, l_b, acc = lax.fori_loop(0, i, body, (ms[0], ls[0], ms[1], ls[1], acc0))
    o = acc / jnp.where(head_a, l_a, l_b)
    sq = o * o
    ms_a = jnp.sum(jnp.where(head_a, sq, 0.0), axis=1, keepdims=True)
    ms_b = jnp.sum(jnp.where(head_a, 0.0, sq), axis=1, keepdims=True)
    mean_sq = jnp.where(head_a, ms_a, ms_b) * (1.0 / ATT_HEAD_DIM)
    o_ref[...] = (o * lax.rsqrt(mean_sq + RMS_EPS) * gain_ref[...]).astype(BF16)


def _moba_attention(q, k, v, kmean, gain, B, S):
    T, W = q.shape
    nq = S // MOBA_BLOCK
    npairs = W // LANES
    return pl.pallas_call(
        _attn_kernel,
        grid=(B, npairs, nq),
        in_specs=[pl.BlockSpec((MOBA_BLOCK, LANES), lambda b, h, i: (b * nq + i, h)),
                  pl.BlockSpec((S, LANES), lambda b, h, i: (b, h)),
                  pl.BlockSpec((S, LANES), lambda b, h, i: (b, h)),
                  pl.BlockSpec((None, LANES, LANES), lambda b, h, i: (b, 0, h)),
                  pl.BlockSpec((1, LANES), lambda b, h, i: (0, h))],
        out_specs=pl.BlockSpec((MOBA_BLOCK, LANES), lambda b, h, i: (b * nq + i, h)),
        out_shape=jax.ShapeDtypeStruct((T, W), BF16),
        scratch_shapes=[pltpu.VMEM((S, 2 * LANES), BF16)],
        compiler_params=_params("parallel", "parallel", "arbitrary"),
        name="moba_attention",
    )(q, k, v, kmean, gain)


def _mlstm_kernel(mqk_ref, mv_ref, mo_ref, g_ref, cw_ref, cb_ref, gain_ref, o_ref,
                  c_ref, m_ref, tail_ref):
    L = ML_CHUNK

    @pl.when(pl.program_id(1) == 0)
    def _reset():
        c_ref[...] = jnp.zeros_like(c_ref)
        m_ref[...] = jnp.zeros_like(m_ref)
        tail_ref[...] = jnp.zeros_like(tail_ref)

    cur = mqk_ref[...].astype(F32)
    xp = jnp.concatenate([tail_ref[...], cur], axis=0)
    base = SUBLANES - (CONV_WIDTH - 1)
    y = cb_ref[...] + cw_ref[0:1, :] * xp[base:base + L]
    for j in range(1, CONV_WIDTH):
        y = y + cw_ref[j:j + 1, :] * xp[base + j:base + j + L]
    tail_ref[...] = cur[L - SUBLANES:]
    y = y * _sigmoid(y)

    gts = g_ref[...]
    lane = _iota((L, LANES), 1)
    row = _iota((L, LANES), 0)
    logf = -(jnp.maximum(-gts, 0.0) + jnp.log1p(jnp.exp(-jnp.abs(gts))))
    is_f = (lane >= ML_HEADS) & (lane < 2 * ML_HEADS)
    tri = (lane <= row)
    bcum = jnp.dot(tri.astype(F32), jnp.where(is_f, logf, 0.0), precision=HIGHEST,
                   preferred_element_type=F32)
    cols = jnp.where(lane < ML_HEADS, gts, bcum)
    rows = cols.T

    ones_col = (lane == 0).astype(BF16)
    for h in range(ML_HEADS):
        pair, odd = divmod(h, 2)
        in_head = (lane >= odd * ML_QK_DIM) & (lane < (odd + 1) * ML_QK_DIM)
        yq = y[:, pair * LANES:(pair + 1) * LANES]
        yk = y[:, ML_QK_WIDTH + pair * LANES:ML_QK_WIDTH + (pair + 1) * LANES]
        qh = jnp.where(in_head, yq * (ML_QK_DIM ** -0.5), 0.0).astype(BF16)
        kh = jnp.where(in_head, yk, 0.0).astype(BF16)
        vh = mv_ref[:, h * ML_V_DIM:(h + 1) * ML_V_DIM]
        vaug = jnp.concatenate([vh, ones_col], axis=1)

        i_c = cols[:, h:h + 1]
        b_c = cols[:, ML_HEADS + h:ML_HEADS + h + 1]
        i_r = rows[h:h + 1, :]
        b_r = rows[ML_HEADS + h:ML_HEADS + h + 1, :]
        m_prev = m_ref[h:h + 1, 0:1]

        d_log = jnp.where(tri, b_c - b_r + i_r, NEG_INF)
        inter = b_c + m_prev
        m_t = jnp.maximum(inter, jnp.max(d_log, axis=1, keepdims=True))
        w_intra = jnp.exp(d_log - m_t)
        w_inter = jnp.exp(inter - m_t)

        state = c_ref[h]
        s = lax.dot_general(qh, kh, _NT, preferred_element_type=F32) * w_intra
        numden = (jnp.dot(s.astype(BF16), vaug, preferred_element_type=F32)
                  + w_inter * lax.dot_general(qh, state.astype(BF16), _NT,
                                              preferred_element_type=F32))
        num = numden[:, :ML_V_DIM]
        den = numden[:, ML_V_DIM:ML_V_DIM + 1]
        hcur = num / jnp.maximum(jnp.abs(den), jnp.exp(-m_t))

        b_last = b_c[L - 1:L, :]
        decay = b_last - b_c + i_c
        m_new = jnp.maximum(b_last + m_prev, jnp.max(decay, axis=0, keepdims=True))
        w_state = jnp.exp(decay - m_new)
        carry_scale = jnp.exp(b_last + m_prev - m_new)
        vw = (vaug.astype(F32) * w_state).astype(BF16)
        c_ref[h] = carry_scale * state + lax.dot_general(vw, kh, _TN,
                                                         preferred_element_type=F32)
        m_ref[h:h + 1, :] = jnp.broadcast_to(m_new, (1, LANES))

        mean_sq = jnp.mean(hcur * hcur, axis=1, keepdims=True)
        sl = slice(h * ML_V_DIM, (h + 1) * ML_V_DIM)
        gate = _sigmoid(mo_ref[:, sl].astype(F32))
        o_ref[:, sl] = (hcur * lax.rsqrt(mean_sq + RMS_EPS) * gain_ref[:, sl]
                        * gate).astype(BF16)


def _mlstm(mqk, mv, mo, gates, conv_w, conv_b, gain, B, S):
    T, W = mv.shape
    L = ML_CHUNK
    nc = S // L
    row = lambda b, c: (b * nc + c, 0)
    const = lambda b, c: (0, 0)
    return pl.pallas_call(
        _mlstm_kernel,
        grid=(B, nc),
        in_specs=[pl.BlockSpec((L, W), row),
                  pl.BlockSpec((L, W), row),
                  pl.BlockSpec((L, W), row),
                  pl.BlockSpec((L, LANES), row),
                  pl.BlockSpec(conv_w.shape, const),
                  pl.BlockSpec((1, W), const),
                  pl.BlockSpec((1, W), const)],
        out_specs=pl.BlockSpec((L, W), row),
        out_shape=jax.ShapeDtypeStruct((T, W), BF16),
        scratch_shapes=[pltpu.VMEM((ML_HEADS, 2 * ML_V_DIM, LANES), F32),
                        pltpu.VMEM((SUBLANES, LANES), F32),
                        pltpu.VMEM((SUBLANES, W), F32)],
        compiler_params=_params("parallel", "arbitrary"),
        name="mlstm",
    )(mqk, mv, mo, gates, conv_w, conv_b, gain)


def _outproj_kernel(att_ref, ml_ref, wa_ref, wb_ref, x_ref, g_ref, b_ref, o_ref, *, alpha):
    mix = (jnp.dot(att_ref[...], wa_ref[...], preferred_element_type=F32)
           + jnp.dot(ml_ref[...], wb_ref[...], preferred_element_type=F32))
    o_ref[...] = _layer_norm(alpha * x_ref[...] + mix, g_ref[...], b_ref[...])


def _outproj(att, ml, w_out, x2, ln_g, ln_b, alpha):
    T, D = x2.shape
    W = att.shape[1]
    tm = min(T, 512)
    row = lambda i: (i, 0)
    const = lambda i: (0, 0)
    return pl.pallas_call(
        functools.partial(_outproj_kernel, alpha=alpha),
        grid=(T // tm,),
        in_specs=[pl.BlockSpec((tm, W), row),
                  pl.BlockSpec((tm, W), row),
                  pl.BlockSpec((W, D), lambda i: (0, 0)),
                  pl.BlockSpec((W, D), lambda i: (1, 0)),
                  pl.BlockSpec((tm, D), row),
                  pl.BlockSpec((1, D), const),
                  pl.BlockSpec((1, D), const)],
        out_specs=pl.BlockSpec((tm, D), row),
        out_shape=jax.ShapeDtypeStruct((T, D), F32),
        compiler_params=_params("parallel"),
        name="outproj_ln",
    )(att, ml, w_out, w_out, x2, ln_g, ln_b)


def _swiglu_chunk(xb, wg, wu, wd):
    g = jnp.dot(xb, wg, preferred_element_type=F32)
    u = jnp.dot(xb, wu, preferred_element_type=F32)
    h = (g * _sigmoid(g) * u).astype(BF16)
    return jnp.dot(h, wd, preferred_element_type=F32)


def _dense_ffn_kernel(x_ref, wg_ref, wu_ref, wd_ref, g_ref, b_ref, o_ref, xb_ref, acc_ref,
                      *, alpha):
    f = pl.program_id(1)

    @pl.when(f == 0)
    def _start():
        xb_ref[...] = x_ref[...].astype(BF16)
        acc_ref[...] = jnp.zeros_like(acc_ref)

    acc_ref[...] += _swiglu_chunk(xb_ref[...], wg_ref[...], wu_ref[...], wd_ref[...])

    @pl.when(f == pl.num_programs(1) - 1)
    def _finish():
        o_ref[...] = _layer_norm(alpha * x_ref[...] + acc_ref[...], g_ref[...], b_ref[...])


def _hidden_chunk(width, target):
    best = LANES
    for c in range(LANES, target + 1, LANES):
        if width % c == 0:
            best = c
    return best


def _dense_ffn(x2, wg, wu, wd, ln_g, ln_b, alpha):
    T, D = x2.shape
    F = wg.shape[1]
    tm = min(T, 512)
    tf = _hidden_chunk(F, 1408)
    row = lambda i, f: (i, 0)
    const = lambda i, f: (0, 0)
    return pl.pallas_call(
        functools.partial(_dense_ffn_kernel, alpha=alpha),
        grid=(T // tm, F // tf),
        in_specs=[pl.BlockSpec((tm, D), row),
                  pl.BlockSpec((D, tf), lambda i, f: (0, f)),
                  pl.BlockSpec((D, tf), lambda i, f: (0, f)),
                  pl.BlockSpec((tf, D), lambda i, f: (f, 0)),
                  pl.BlockSpec((1, D), const),
                  pl.BlockSpec((1, D), const)],
        out_specs=pl.BlockSpec((tm, D), row),
        out_shape=jax.ShapeDtypeStruct((T, D), F32),
        scratch_shapes=[pltpu.VMEM((tm, D), BF16), pltpu.VMEM((tm, D), F32)],
        compiler_params=_params("parallel", "arbitrary"),
        name="dense_ffn_ln",
    )(x2, wg, wu, wd, ln_g, ln_b)


def _router_kernel(x_ref, rw_ref, rb_ref, ids_ref, wts_ref, cnt_ref, carry_ref, *, n_experts):
    @pl.when(pl.program_id(0) == 0)
    def _reset():
        carry_ref[...] = jnp.zeros_like(carry_ref)

    tm = x_ref.shape[0]
    logits = jnp.dot(x_ref[...], rw_ref[...], precision=HIGHEST,
                     preferred_element_type=F32) + rb_ref[...]
    lane = _iota((tm, LANES), 1)
    logits = jnp.where(lane < n_experts, logits, NEG_INF)

    def top(lg):
        best = jnp.max(lg, axis=1, keepdims=True)
        first = jnp.min(jnp.where(lg == best, lane, LANES), axis=1, keepdims=True)
        return best, first

    v1, e1 = top(logits)
    hot1 = lane == e1
    v2, e2 = top(jnp.where(hot1, NEG_INF, logits))
    hot2 = lane == e2
    ex = jnp.exp(v2 - v1)
    w1 = 1.0 / (1.0 + ex)
    w2 = ex / (1.0 + ex)

    assigned = (hot1 | hot2).astype(BF16)
    before = (_iota((tm, tm), 1) < _iota((tm, tm), 0)).astype(BF16)
    carry = carry_ref[0:1, :]
    rank = jnp.dot(before, assigned, preferred_element_type=F32) + carry
    r1 = jnp.sum(jnp.where(hot1, rank, 0.0), axis=1, keepdims=True).astype(I32)
    r2 = jnp.sum(jnp.where(hot2, rank, 0.0), axis=1, keepdims=True).astype(I32)
    total = carry + jnp.sum(assigned.astype(F32), axis=0, keepdims=True)
    carry_ref[...] = jnp.broadcast_to(total, carry_ref.shape)
    cnt_ref[...] = jnp.broadcast_to(total, cnt_ref.shape).astype(I32)

    ids_ref[...] = jnp.where(lane == 0, e1, jnp.where(lane == 1, e2,
                             jnp.where(lane == 2, r1, jnp.where(lane == 3, r2, 0))))
    wts_ref[...] = jnp.where(lane == 0, w1, jnp.where(lane == 1, w2, 0.0))


def _router(x2, router_w, router_b):
    T, D = x2.shape
    E = router_w.shape[1]
    tm = min(T, 512)
    rw = jnp.pad(router_w, ((0, 0), (0, LANES - E)))
    rb = jnp.pad(router_b, (0, LANES - E))[None, :]
    row = lambda i: (i, 0)
    const = lambda i: (0, 0)
    return pl.pallas_call(
        functools.partial(_router_kernel, n_experts=E),
        grid=(T // tm,),
        in_specs=[pl.BlockSpec((tm, D), row),
                  pl.BlockSpec((D, LANES), const),
                  pl.BlockSpec((1, LANES), const)],
        out_specs=[pl.BlockSpec((tm, LANES), row),
                   pl.BlockSpec((tm, LANES), row),
                   pl.BlockSpec((SUBLANES, LANES), const)],
        out_shape=[jax.ShapeDtypeStruct((T, LANES), I32),
                   jax.ShapeDtypeStruct((T, LANES), F32),
                   jax.ShapeDtypeStruct((SUBLANES, LANES), I32)],
        scratch_shapes=[pltpu.VMEM((SUBLANES, LANES), F32)],
        compiler_params=_params("arbitrary"),
        name="router",
    )(x2, rw, rb)


def _scatter_kernel(offs_ref, route_ref, x_ref, xg_in_ref, xg_ref, sem):
    del xg_in_ref
    tm = x_ref.shape[0]

    def copy(r, k):
        slot = offs_ref[route_ref[0, k, r]] + route_ref[0, TOP_K + k, r]
        return pltpu.make_async_copy(x_ref.at[pl.ds(r, 1)], xg_ref.at[pl.ds(slot, 1)], sem)

    def start(r, c):
        for k in range(TOP_K):
            copy(r, k).start()
        return c

    def wait(r, c):
        for k in range(TOP_K):
            copy(r, k).wait()
        return c

    lax.fori_loop(0, tm, start, 0)
    lax.fori_loop(0, tm, wait, 0)


def _scatter_rows(x2, route, offs, n_rows):
    T, D = x2.shape
    nt, _, tm = route.shape
    xg0 = jnp.zeros((n_rows, D), F32)
    return pl.pallas_call(
        _scatter_kernel,
        grid_spec=pltpu.PrefetchScalarGridSpec(
            num_scalar_prefetch=1,
            grid=(nt,),
            in_specs=[pl.BlockSpec((1, 2 * TOP_K, tm), lambda i, offs: (i, 0, 0),
                                   memory_space=pltpu.SMEM),
                      pl.BlockSpec((tm, D), lambda i, offs: (i, 0)),
                      pl.BlockSpec(memory_space=pl.ANY)],
            out_specs=pl.BlockSpec(memory_space=pl.ANY),
            scratch_shapes=[pltpu.SemaphoreType.DMA(())]),
        out_shape=jax.ShapeDtypeStruct((n_rows, D), F32),
        input_output_aliases={3: 0},
        compiler_params=_params("arbitrary"),
        name="moe_scatter",
    )(offs, route, x2, xg0)


def _expert_kernel(te_ref, na_ref, x_ref, wg_ref, wu_ref, wd_ref, o_ref, xb_ref, acc_ref):
    g = pl.program_id(0)
    f = pl.program_id(1)
    active = g < na_ref[0]

    @pl.when(active & (f == 0))
    def _start():
        xb_ref[...] = x_ref[...].astype(BF16)
        acc_ref[...] = jnp.zeros_like(acc_ref)

    @pl.when(active)
    def _accumulate():
        acc_ref[...] += _swiglu_chunk(xb_ref[...], wg_ref[...], wu_ref[...], wd_ref[...])

    last = f == pl.num_programs(1) - 1

    @pl.when(active & last)
    def _finish():
        o_ref[...] = acc_ref[...]

    @pl.when(jnp.logical_not(active) & last)
    def _unused_tile():
        o_ref[...] = jnp.zeros_like(o_ref)


def _expert_ffn(xg, wg, wu, wd, tile_expert, n_active, tm):
    P, D = xg.shape
    E, _, F = wg.shape
    tf = _hidden_chunk(F, 896)
    nf = F // tf

    def tile(g, te, na):
        return jnp.minimum(g, na[0] - 1)

    def chunk(g, f, na):
        return jnp.where(g < na[0], f, nf - 1)

    return pl.pallas_call(
        _expert_kernel,
        grid_spec=pltpu.PrefetchScalarGridSpec(
            num_scalar_prefetch=2,
            grid=(P // tm, nf),
            in_specs=[pl.BlockSpec((tm, D), lambda g, f, te, na: (tile(g, te, na), 0)),
                      pl.BlockSpec((None, D, tf),
                                   lambda g, f, te, na: (te[tile(g, te, na)], 0, chunk(g, f, na))),
                      pl.BlockSpec((None, D, tf),
                                   lambda g, f, te, na: (te[tile(g, te, na)], 0, chunk(g, f, na))),
                      pl.BlockSpec((None, tf, D),
                                   lambda g, f, te, na: (te[tile(g, te, na)], chunk(g, f, na), 0))],
            out_specs=pl.BlockSpec((tm, D), lambda g, f, te, na: (g, 0)),
            scratch_shapes=[pltpu.VMEM((tm, D), BF16), pltpu.VMEM((tm, D), F32)]),
        out_shape=jax.ShapeDtypeStruct((P, D), F32),
        compiler_params=_params("arbitrary", "arbitrary"),
        name="moe_experts",
    )(tile_expert, n_active, xg, wg, wu, wd)


def _combine_kernel(offs_ref, route_ref, yg_ref, wts_ref, x_ref, g_ref, b_ref, o_ref,
                    buf_ref, sem, *, alpha):
    tm = x_ref.shape[0]

    def copy(r, k):
        slot = offs_ref[route_ref[0, k, r]] + route_ref[0, TOP_K + k, r]
        return pltpu.make_async_copy(yg_ref.at[pl.ds(slot, 1)], buf_ref.at[k, pl.ds(r, 1)], sem)

    def start(r, c):
        for k in range(TOP_K):
            copy(r, k).start()
        return c

    def wait(r, c):
        for k in range(TOP_K):
            copy(r, k).wait()
        return c

    lax.fori_loop(0, tm, start, 0)
    lax.fori_loop(0, tm, wait, 0)
    w = wts_ref[...]
    ffn = w[:, 0:1] * buf_ref[0] + w[:, 1:2] * buf_ref[1]
    o_ref[...] = _layer_norm(alpha * x_ref[...] + ffn, g_ref[...], b_ref[...])


def _combine(yg, route, offs, wts, x2, ln_g, ln_b, alpha):
    T, D = x2.shape
    nt, _, tm = route.shape
    row = lambda i, offs: (i, 0)
    const = lambda i, offs: (0, 0)
    return pl.pallas_call(
        functools.partial(_combine_kernel, alpha=alpha),
        grid_spec=pltpu.PrefetchScalarGridSpec(
            num_scalar_prefetch=1,
            grid=(nt,),
            in_specs=[pl.BlockSpec((1, 2 * TOP_K, tm), lambda i, offs: (i, 0, 0),
                                   memory_space=pltpu.SMEM),
                      pl.BlockSpec(memory_space=pl.ANY),
                      pl.BlockSpec((tm, LANES), row),
                      pl.BlockSpec((tm, D), row),
                      pl.BlockSpec((1, D), const),
                      pl.BlockSpec((1, D), const)],
            out_specs=pl.BlockSpec((tm, D), row),
            scratch_shapes=[pltpu.VMEM((TOP_K, tm, D), F32), pltpu.SemaphoreType.DMA(())]),
        out_shape=jax.ShapeDtypeStruct((T, D), F32),
        compiler_params=_params("arbitrary"),
        name="moe_combine_ln",
    )(offs, route, yg, wts, x2, ln_g, ln_b)


def _moe_ffn(x2, router_w, router_b, wg, wu, wd, ln_g, ln_b, alpha):
    T, D = x2.shape
    E = router_w.shape[1]
    tm_e = min(T, 512)
    tm_r = min(T, 256)
    ids, wts, counts = _router(x2, router_w, router_b)

    cnt = counts[0, :E]
    padded = ((cnt + tm_e - 1) // tm_e) * tm_e
    ends = jnp.cumsum(padded)
    offs = (ends - padded).astype(I32)
    n_rows = TOP_K * T + E * tm_e
    n_tiles = n_rows // tm_e
    tile_start = jnp.arange(n_tiles, dtype=I32) * tm_e
    tile_expert = jnp.minimum(jnp.sum(tile_start[:, None] >= ends[None, :], axis=1), E - 1).astype(I32)
    n_active = (ends[-1:] // tm_e).astype(I32)

    route = ids[:, :2 * TOP_K].reshape(T // tm_r, tm_r, 2 * TOP_K).transpose(0, 2, 1)
    xg = _scatter_rows(x2, route, offs, n_rows)
    yg = _expert_ffn(xg, wg, wu, wd, tile_expert, n_active, tm_e)
    return _combine(yg, route, offs, wts, x2, ln_g, ln_b, alpha)


def kernel(x, positions, w_in, gate_bias, conv_w, conv_b, norm_attn, norm_mlstm, w_out,
           ln1_g, ln1_b, dense_w_gate, dense_w_up, dense_w_down, router_w, router_b,
           moe_w_gate, moe_w_up, moe_w_down, ln2_g, ln2_b):
    B, S, D = x.shape
    depth = w_in.shape[0]
    T = B * S
    alpha = (2.0 * depth) ** 0.25
    n_main = 3 * ATT_WIDTH + 2 * ML_QK_WIDTH + 2 * ML_WIDTH
    nblk = S // MOBA_BLOCK

    tabs = _rope_tables(positions)
    x2 = x.reshape(T, D)
    for l in range(depth):
        w_main = w_in[l, :, :n_main].astype(BF16)
        w_gate = jnp.pad(w_in[l, :, n_main:], ((0, 0), (0, LANES - 2 * ML_HEADS))).astype(BF16)
        gbias = jnp.pad(gate_bias[l], (0, LANES - 2 * ML_HEADS))[None, :]
        q, k, v, mqk, mv, mo, gates, kmean = _inproj(x2, w_main, w_gate, gbias, tabs)
        kmean = jnp.pad(kmean.reshape(B, nblk, ATT_WIDTH), ((0, 0), (0, LANES - nblk), (0, 0)))
        att = _moba_attention(q, k, v, kmean, norm_attn[l][None, :], B, S)
        ml = _mlstm(mqk, mv, mo, gates, conv_w[l], conv_b[l][None, :],
                    norm_mlstm[l][None, :], B, S)
        x2 = _outproj(att, ml, w_out[l].astype(BF16), x2, ln1_g[l][None, :], ln1_b[l][None, :],
                      alpha)
        j = l // 2
        if l % 2 == 0:
            x2 = _dense_ffn(x2, dense_w_gate[j].astype(BF16), dense_w_up[j].astype(BF16),
                            dense_w_down[j].astype(BF16), ln2_g[l][None, :], ln2_b[l][None, :],
                            alpha)
        else:
            x2 = _moe_ffn(x2, router_w[j], router_b[j], moe_w_gate[j].astype(BF16),
                          moe_w_up[j].astype(BF16), moe_w_down[j].astype(BF16),
                          ln2_g[l][None, :], ln2_b[l][None, :], alpha)
    return x2.reshape(B, S, D)
```

```python
import functools

import jax
import jax.numpy as jnp
from jax import lax
from jax.experimental import pallas as pl
from jax.experimental.pallas import tpu as pltpu

F32 = jnp.float32
BF16 = jnp.bfloat16
I32 = jnp.int32

ATT_HEADS = 8
ATT_HEAD_DIM = 64
ATT_WIDTH = ATT_HEADS * ATT_HEAD_DIM
MOBA_BLOCK = 256
MOBA_TOPK = 3
ROPE_THETA = 500000.0
ROPE_DIM = ATT_HEAD_DIM // 4
ML_HEADS = 4
ML_V_DIM = 128
ML_QK_DIM = 64
ML_QK_WIDTH = ML_HEADS * ML_QK_DIM
ML_WIDTH = ML_HEADS * ML_V_DIM
ML_CHUNK = 128
CONV_WIDTH = 4
TOP_K = 2
LN_EPS = 1e-5
RMS_EPS = 1e-6

LANES = 128
SUBLANES = 8
V7X_VMEM_BYTES = 64 * 1024 * 1024
VMEM_LIMIT = (V7X_VMEM_BYTES * 3) // 4

MASK_BIAS = -1e9
NEG_INF = float("-inf")
HIGHEST = lax.Precision.HIGHEST
LOG2_E = 1.4426950408889634
DMA_UNROLL = 8

_NT = (((1,), (1,)), ((), ()))
_TN = (((0,), (0,)), ((), ()))


def _params(*sem):
    return pltpu.CompilerParams(dimension_semantics=sem, vmem_limit_bytes=VMEM_LIMIT)


def _iota(shape, dim):
    return lax.broadcasted_iota(I32, shape, dim)


def _sigmoid(x):
    return 1.0 / (1.0 + jnp.exp(-x))


def _layer_norm(y, g, b):
    mu = jnp.mean(y, axis=-1, keepdims=True)
    yc = y - mu
    var = jnp.mean(yc * yc, axis=-1, keepdims=True)
    return yc * lax.rsqrt(var + LN_EPS) * g + b


def _rope_tab_kernel(pos_ref, freq_ref, cos_ref, s1_ref, s2_ref):
    ang = pos_ref[...].astype(F32) * freq_ref[...]
    d = _iota(ang.shape, 1) & (ATT_HEAD_DIM - 1)
    half = ROPE_DIM // 2
    s = jnp.sin(ang)
    cos_ref[...] = jnp.cos(ang)
    s1_ref[...] = jnp.where(d < half, -s, 0.0)
    s2_ref[...] = jnp.where((d >= half) & (d < ROPE_DIM), s, 0.0)


def _rope_tables(positions):
    T = positions.size
    tm = min(T, 1024)
    half = ROPE_DIM // 2
    inv_freq = ROPE_THETA ** (-jnp.arange(half, dtype=F32) / half)
    d = jnp.arange(LANES) % ATT_HEAD_DIM
    freq = jnp.where(d < ROPE_DIM, inv_freq[d % half], 0.0).astype(F32)[None, :]
    tab = jax.ShapeDtypeStruct((T, LANES), F32)
    return pl.pallas_call(
        _rope_tab_kernel,
        grid=(T // tm,),
        in_specs=[pl.BlockSpec((tm, 1), lambda i: (i, 0)),
                  pl.BlockSpec((1, LANES), lambda i: (0, 0))],
        out_specs=[pl.BlockSpec((tm, LANES), lambda i: (i, 0))] * 3,
        out_shape=[tab, tab, tab],
        compiler_params=_params("parallel"),
        name="rope_tables",
    )(positions.reshape(T, 1), freq)


def _inproj_kernel(x_ref, w_ref, wg_ref, gb_ref, cos_ref, s1_ref, s2_ref,
                   q_ref, k_ref, v_ref, mqk_ref, mv_ref, mo_ref, g_ref, km_ref):
    xb = x_ref[...].astype(BF16)
    W = ATT_WIDTH

    def proj(c):
        return jnp.dot(xb, w_ref[:, c * W:(c + 1) * W], preferred_element_type=F32)

    rep = W // LANES
    cos = jnp.concatenate([cos_ref[...]] * rep, axis=1)
    s1 = jnp.concatenate([s1_ref[...]] * rep, axis=1)
    s2 = jnp.concatenate([s2_ref[...]] * rep, axis=1)
    half = ROPE_DIM // 2

    def rope(t):
        return t * cos + pltpu.roll(t, W - half, 1) * s1 + pltpu.roll(t, half, 1) * s2

    q_ref[...] = (rope(proj(0)) * (LOG2_E * ATT_HEAD_DIM ** -0.5)).astype(BF16)
    k = rope(proj(1))
    k_ref[...] = k.astype(BF16)
    for g in range(k.shape[0] // MOBA_BLOCK):
        blk = k[g * MOBA_BLOCK:(g + 1) * MOBA_BLOCK]
        km_ref[0, g:g + 1, :] = jnp.sum(blk, axis=0, keepdims=True) * (1.0 / MOBA_BLOCK)
    v_ref[...] = proj(2).astype(BF16)
    mqk_ref[...] = proj(3).astype(BF16)
    mv_ref[...] = proj(4).astype(BF16)
    mo_ref[...] = proj(5).astype(BF16)
    g_ref[...] = jnp.dot(xb, wg_ref[...], preferred_element_type=F32) + gb_ref[...]


def _inproj(x2, w_main, w_gate, gate_bias, tabs):
    T, D = x2.shape
    tm = min(T, 512)
    nkb = tm // MOBA_BLOCK
    W = ATT_WIDTH
    row = lambda i: (i, 0)
    const = lambda i: (0, 0)
    act = jax.ShapeDtypeStruct((T, W), BF16)
    return pl.pallas_call(
        _inproj_kernel,
        grid=(T // tm,),
        in_specs=[pl.BlockSpec((tm, D), row),
                  pl.BlockSpec(w_main.shape, const),
                  pl.BlockSpec(w_gate.shape, const),
                  pl.BlockSpec((1, LANES), const),
                  pl.BlockSpec((tm, LANES), row),
                  pl.BlockSpec((tm, LANES), row),
                  pl.BlockSpec((tm, LANES), row)],
        out_specs=[pl.BlockSpec((tm, W), row)] * 6
        + [pl.BlockSpec((tm, LANES), row),
           pl.BlockSpec((1, nkb, W), lambda i: (i, 0, 0))],
        out_shape=[act] * 6 + [jax.ShapeDtypeStruct((T, LANES), F32),
                               jax.ShapeDtypeStruct((T // tm, nkb, W), F32)],
        compiler_params=_params("parallel"),
        name="inproj",
    )(x2, w_main, w_gate, gate_bias, *tabs)


def _split_bf16(x, parts):
    out = []
    for _ in range(parts):
        hi = x.astype(BF16)
        out.append(hi)
        x = x - hi.astype(F32)
    return out


def _attn_kernel(q_ref, k_ref, v_ref, km_ref, gain_ref, o_ref, kaug_ref, vaug_ref, *, group):
    i = pl.program_id(2)
    tq = q_ref.shape[0]
    nblk = k_ref.shape[0] // MOBA_BLOCK
    lane = _iota((tq, LANES), 1)
    head_a = lane < ATT_HEAD_DIM

    @pl.when(i == 0)
    def _build_augmented_keys_values():
        kaug_ref[:, :LANES] = k_ref[...]
        blane = _iota((MOBA_BLOCK, LANES), 1)
        in_a = blane < ATT_HEAD_DIM
        one = jnp.ones((MOBA_BLOCK, LANES), BF16)

        def fill(j, carry):
            rows = pl.ds(pl.multiple_of(j * MOBA_BLOCK, MOBA_BLOCK), MOBA_BLOCK)
            kaug_ref[rows, LANES:] = (blane == j).astype(BF16)
            vj = v_ref[rows, :]
            vaug_ref[0, rows, :] = jnp.where(in_a, vj, one)
            vaug_ref[1, rows, :] = jnp.where(in_a, one, vj)
            return carry

        lax.fori_loop(0, nblk, fill, 0)

    q = q_ref[...]
    zero = jnp.zeros_like(q)
    nrow = -(-nblk // SUBLANES) * SUBLANES
    km_parts = _split_bf16(km_ref[:nrow, :], 3)
    brow = _iota((nrow, tq), 0)

    def select_bias(qh):
        gate = sum(lax.dot_general(part, qh, _NT, preferred_element_type=F32)
                   for part in km_parts)
        gate = jnp.where(brow < i, gate, NEG_INF)
        picked = brow < 0
        for _ in range(MOBA_TOPK):
            best = jnp.max(gate, axis=0, keepdims=True)
            first = jnp.min(jnp.where(gate == best, brow, nrow), axis=0, keepdims=True)
            hit = (brow == first) & (best > NEG_INF)
            picked = picked | hit
            gate = jnp.where(hit, NEG_INF, gate)
        bias = jnp.where(picked, 0.0, MASK_BIAS)
        if nrow < LANES:
            bias = jnp.concatenate([bias, jnp.full((LANES - nrow, tq), MASK_BIAS, F32)], axis=0)
        return bias.T.astype(BF16)

    qh = (jnp.where(head_a, q, zero), jnp.where(head_a, zero, q))
    qa = [jnp.concatenate([x, select_bias(x)], axis=1) for x in qh]

    own = pl.ds(pl.multiple_of(i * MOBA_BLOCK, MOBA_BLOCK), MOBA_BLOCK)
    kd = k_ref[own, :]
    causal = _iota((tq, MOBA_BLOCK), 1) <= _iota((tq, MOBA_BLOCK), 0)
    state = []
    for h in range(2):
        s = jnp.where(causal, lax.dot_general(qh[h], kd, _NT, preferred_element_type=F32),
                      NEG_INF)
        m = jnp.max(s, axis=1, keepdims=True)
        p = jnp.exp2(s - m).astype(BF16)
        state += [m, jnp.dot(p, vaug_ref[h, own, :], preferred_element_type=F32)]

    span = group * MOBA_BLOCK

    def body(g, carry):
        rows = pl.ds(pl.multiple_of(g * span, span), span)
        kg = kaug_ref[rows, :]
        new = []
        for h in range(2):
            m, acc = carry[2 * h], carry[2 * h + 1]
            s = lax.dot_general(qa[h], kg, _NT, preferred_element_type=F32)
            m_new = jnp.maximum(m, jnp.max(s, axis=1, keepdims=True))
            p = jnp.exp2(s - m_new).astype(BF16)
            acc = jnp.exp2(m - m_new) * acc + jnp.dot(p, vaug_ref[h, rows, :],
                                                     preferred_element_type=F32)
            new += [m_new, acc]
        return tuple(new)

    n_groups = lax.div(i + (group - 1), group)
    _, acc_a, _, acc_b = lax.fori_loop(0, n_groups, body, tuple(state))
    num = jnp.where(head_a, acc_a, acc_b)
    den = jnp.where(head_a, pltpu.roll(acc_a, ATT_HEAD_DIM, 1), pltpu.roll(acc_b, ATT_HEAD_DIM, 1))
    o = num / den
    sq = o * o
    ms_a = jnp.sum(jnp.where(head_a, sq, 0.0), axis=1, keepdims=True)
    ms_b = jnp.sum(jnp.where(head_a, 0.0, sq), axis=1, keepdims=True)
    mean_sq = jnp.where(head_a, ms_a, ms_b) * (1.0 / ATT_HEAD_DIM)
    o_ref[...] = (o * lax.rsqrt(mean_sq + RMS_EPS) * gain_ref[...]).astype(BF16)


def _moba_attention(q, k, v, kmean, gain, B, S):
    T, W = q.shape
    nq = S // MOBA_BLOCK
    npairs = W // LANES
    group = 4 if nq % 4 == 0 else 1
    return pl.pallas_call(
        functools.partial(_attn_kernel, group=group),
        grid=(B, npairs, nq),
        in_specs=[pl.BlockSpec((MOBA_BLOCK, LANES), lambda b, h, i: (b * nq + i, h)),
                  pl.BlockSpec((S, LANES), lambda b, h, i: (b, h)),
                  pl.BlockSpec((S, LANES), lambda b, h, i: (b, h)),
                  pl.BlockSpec((None, LANES, LANES), lambda b, h, i: (b, 0, h)),
                  pl.BlockSpec((1, LANES), lambda b, h, i: (0, h))],
        out_specs=pl.BlockSpec((MOBA_BLOCK, LANES), lambda b, h, i: (b * nq + i, h)),
        out_shape=jax.ShapeDtypeStruct((T, W), BF16),
        scratch_shapes=[pltpu.VMEM((S, 2 * LANES), BF16), pltpu.VMEM((2, S, LANES), BF16)],
        compiler_params=_params("parallel", "parallel", "arbitrary"),
        name="moba_attention",
    )(q, k, v, kmean, gain)


def _mlstm_kernel(mqk_ref, mv_ref, mo_ref, g_ref, cw_ref, cb_ref, gain_ref, o_ref,
                  c_ref, m_ref, tail_ref):
    L = ML_CHUNK

    @pl.when(pl.program_id(1) == 0)
    def _reset():
        c_ref[...] = jnp.zeros_like(c_ref)
        m_ref[...] = jnp.zeros_like(m_ref)
        tail_ref[...] = jnp.zeros_like(tail_ref)

    cur = mqk_ref[...].astype(F32)
    xp = jnp.concatenate([tail_ref[...], cur], axis=0)
    base = SUBLANES - (CONV_WIDTH - 1)
    y = cb_ref[...] + cw_ref[0:1, :] * xp[base:base + L]
    for j in range(1, CONV_WIDTH):
        y = y + cw_ref[j:j + 1, :] * xp[base + j:base + j + L]
    tail_ref[...] = cur[L - SUBLANES:]
    y = y * _sigmoid(y)

    gts = g_ref[...]
    lane = _iota((L, LANES), 1)
    row = _iota((L, LANES), 0)
    logf = -(jnp.maximum(-gts, 0.0) + jnp.log1p(jnp.exp(-jnp.abs(gts))))
    is_f = (lane >= ML_HEADS) & (lane < 2 * ML_HEADS)
    tri = (lane <= row)
    bcum = jnp.dot(tri.astype(F32), jnp.where(is_f, logf, 0.0), precision=HIGHEST,
                   preferred_element_type=F32)
    cols = jnp.where(lane < ML_HEADS, gts, bcum)
    rows = cols.T

    ones_col = (lane == 0).astype(BF16)
    for h in range(ML_HEADS):
        pair, odd = divmod(h, 2)
        in_head = (lane >= odd * ML_QK_DIM) & (lane < (odd + 1) * ML_QK_DIM)
        yq = y[:, pair * LANES:(pair + 1) * LANES]
        yk = y[:, ML_QK_WIDTH + pair * LANES:ML_QK_WIDTH + (pair + 1) * LANES]
        qh = jnp.where(in_head, yq * (ML_QK_DIM ** -0.5), 0.0).astype(BF16)
        kh = jnp.where(in_head, yk, 0.0).astype(BF16)
        vh = mv_ref[:, h * ML_V_DIM:(h + 1) * ML_V_DIM]
        vaug = jnp.concatenate([vh, ones_col], axis=1)

        i_c = cols[:, h:h + 1]
        b_c = cols[:, ML_HEADS + h:ML_HEADS + h + 1]
        i_r = rows[h:h + 1, :]
        b_r = rows[ML_HEADS + h:ML_HEADS + h + 1, :]
        m_prev = m_ref[h:h + 1, 0:1]

        d_log = jnp.where(tri, b_c - b_r + i_r, NEG_INF)
        inter = b_c + m_prev
        m_t = jnp.maximum(inter, jnp.max(d_log, axis=1, keepdims=True))
        w_intra = jnp.exp(d_log - m_t)
        w_inter = jnp.exp(inter - m_t)

        state = c_ref[h]
        s = lax.dot_general(qh, kh, _NT, preferred_element_type=F32) * w_intra
        numden = (jnp.dot(s.astype(BF16), vaug, preferred_element_type=F32)
                  + w_inter * lax.dot_general(qh, state.astype(BF16), _NT,
                                              preferred_element_type=F32))
        num = numden[:, :ML_V_DIM]
        den = numden[:, ML_V_DIM:ML_V_DIM + 1]
        hcur = num / jnp.maximum(jnp.abs(den), jnp.exp(-m_t))

        b_last = b_c[L - 1:L, :]
        decay = b_last - b_c + i_c
        m_new = jnp.maximum(b_last + m_prev, jnp.max(decay, axis=0, keepdims=True))
        w_state = jnp.exp(decay - m_new)
        carry_scale = jnp.exp(b_last + m_prev - m_new)
        vw = (vaug.astype(F32) * w_state).astype(BF16)
        c_ref[h] = carry_scale * state + lax.dot_general(vw, kh, _TN,
                                                         preferred_element_type=F32)
        m_ref[h:h + 1, :] = jnp.broadcast_to(m_new, (1, LANES))

        mean_sq = jnp.mean(hcur * hcur, axis=1, keepdims=True)
        sl = slice(h * ML_V_DIM, (h + 1) * ML_V_DIM)
        gate = _sigmoid(mo_ref[:, sl].astype(F32))
        o_ref[:, sl] = (hcur * lax.rsqrt(mean_sq + RMS_EPS) * gain_ref[:, sl]
                        * gate).astype(BF16)


def _mlstm(mqk, mv, mo, gates, conv_w, conv_b, gain, B, S):
    T, W = mv.shape
    L = ML_CHUNK
    nc = S // L
    row = lambda b, c: (b * nc + c, 0)
    const = lambda b, c: (0, 0)
    return pl.pallas_call(
        _mlstm_kernel,
        grid=(B, nc),
        in_specs=[pl.BlockSpec((L, W), row),
                  pl.BlockSpec((L, W), row),
                  pl.BlockSpec((L, W), row),
                  pl.BlockSpec((L, LANES), row),
                  pl.BlockSpec(conv_w.shape, const),
                  pl.BlockSpec((1, W), const),
                  pl.BlockSpec((1, W), const)],
        out_specs=pl.BlockSpec((L, W), row),
        out_shape=jax.ShapeDtypeStruct((T, W), BF16),
        scratch_shapes=[pltpu.VMEM((ML_HEADS, 2 * ML_V_DIM, LANES), F32),
                        pltpu.VMEM((SUBLANES, LANES), F32),
                        pltpu.VMEM((SUBLANES, W), F32)],
        compiler_params=_params("parallel", "arbitrary"),
        name="mlstm",
    )(mqk, mv, mo, gates, conv_w, conv_b, gain)


def _outproj_kernel(att_ref, ml_ref, wa_ref, wb_ref, x_ref, g_ref, b_ref, o_ref, *, alpha):
    mix = (jnp.dot(att_ref[...], wa_ref[...], preferred_element_type=F32)
           + jnp.dot(ml_ref[...], wb_ref[...], preferred_element_type=F32))
    o_ref[...] = _layer_norm(alpha * x_ref[...] + mix, g_ref[...], b_ref[...])


def _outproj(att, ml, w_out, x2, ln_g, ln_b, alpha):
    T, D = x2.shape
    W = att.shape[1]
    tm = min(T, 512)
    row = lambda i: (i, 0)
    const = lambda i: (0, 0)
    return pl.pallas_call(
        functools.partial(_outproj_kernel, alpha=alpha),
        grid=(T // tm,),
        in_specs=[pl.BlockSpec((tm, W), row),
                  pl.BlockSpec((tm, W), row),
                  pl.BlockSpec((W, D), lambda i: (0, 0)),
                  pl.BlockSpec((W, D), lambda i: (1, 0)),
                  pl.BlockSpec((tm, D), row),
                  pl.BlockSpec((1, D), const),
                  pl.BlockSpec((1, D), const)],
        out_specs=pl.BlockSpec((tm, D), row),
        out_shape=jax.ShapeDtypeStruct((T, D), F32),
        compiler_params=_params("parallel"),
        name="outproj_ln",
    )(att, ml, w_out, w_out, x2, ln_g, ln_b)


def _swiglu_chunk(xb, wg, wu, wd):
    g = jnp.dot(xb, wg, preferred_element_type=F32)
    u = jnp.dot(xb, wu, preferred_element_type=F32)
    h = (g * _sigmoid(g) * u).astype(BF16)
    return jnp.dot(h, wd, preferred_element_type=F32)


def _dense_ffn_kernel(x_ref, wg_ref, wu_ref, wd_ref, g_ref, b_ref, o_ref, *, alpha, chunk):
    x = x_ref[...]
    xb = x.astype(BF16)
    F = wg_ref.shape[1]
    acc = None
    for a in range(0, F, chunk):
        b = min(a + chunk, F)
        part = _swiglu_chunk(xb, wg_ref[:, a:b], wu_ref[:, a:b], wd_ref[a:b, :])
        acc = part if acc is None else acc + part
    o_ref[...] = _layer_norm(alpha * x + acc, g_ref[...], b_ref[...])


def _hidden_chunk(width, target):
    for unit in (2 * LANES, LANES):
        best = 0
        for c in range(unit, target + 1, unit):
            if width % c == 0:
                best = c
        if best:
            return best
    return width


def _dense_ffn(x2, wg, wu, wd, ln_g, ln_b, alpha):
    T, D = x2.shape
    F = wg.shape[1]
    tm = min(T, 512)
    row = lambda i: (i, 0)
    const = lambda i: (0, 0)
    resident = dict(pipeline_mode=pl.Buffered(1))
    return pl.pallas_call(
        functools.partial(_dense_ffn_kernel, alpha=alpha, chunk=4 * LANES),
        grid=(T // tm,),
        in_specs=[pl.BlockSpec((tm, D), row),
                  pl.BlockSpec((D, F), const, **resident),
                  pl.BlockSpec((D, F), const, **resident),
                  pl.BlockSpec((F, D), const, **resident),
                  pl.BlockSpec((1, D), const),
                  pl.BlockSpec((1, D), const)],
        out_specs=pl.BlockSpec((tm, D), row),
        out_shape=jax.ShapeDtypeStruct((T, D), F32),
        compiler_params=_params("parallel"),
        name="dense_ffn_ln",
    )(x2, wg, wu, wd, ln_g, ln_b)


def _router_kernel(x_ref, rw_ref, rb_ref, ids_ref, wts_ref, cnt_ref, carry_ref, *, n_experts):
    @pl.when(pl.program_id(0) == 0)
    def _reset():
        carry_ref[...] = jnp.zeros_like(carry_ref)

    tm = x_ref.shape[0]
    logits = jnp.dot(x_ref[...], rw_ref[...], precision=HIGHEST,
                     preferred_element_type=F32) + rb_ref[...]
    lane = _iota((tm, LANES), 1)
    logits = jnp.where(lane < n_experts, logits, NEG_INF)

    def top(lg):
        best = jnp.max(lg, axis=1, keepdims=True)
        first = jnp.min(jnp.where(lg == best, lane, LANES), axis=1, keepdims=True)
        return best, first

    v1, e1 = top(logits)
    hot1 = lane == e1
    v2, e2 = top(jnp.where(hot1, NEG_INF, logits))
    hot2 = lane == e2
    ex = jnp.exp(v2 - v1)
    w1 = 1.0 / (1.0 + ex)
    w2 = ex / (1.0 + ex)

    assigned = (hot1 | hot2).astype(BF16)
    before = (_iota((tm, tm), 1) < _iota((tm, tm), 0)).astype(BF16)
    carry = carry_ref[0:1, :]
    rank = jnp.dot(before, assigned, preferred_element_type=F32) + carry
    r1 = jnp.sum(jnp.where(hot1, rank, 0.0), axis=1, keepdims=True).astype(I32)
    r2 = jnp.sum(jnp.where(hot2, rank, 0.0), axis=1, keepdims=True).astype(I32)
    total = carry + jnp.sum(assigned.astype(F32), axis=0, keepdims=True)
    carry_ref[...] = jnp.broadcast_to(total, carry_ref.shape)
    cnt_ref[...] = jnp.broadcast_to(total, cnt_ref.shape).astype(I32)

    ids_ref[...] = jnp.where(lane == 0, e1, jnp.where(lane == 1, e2,
                             jnp.where(lane == 2, r1, jnp.where(lane == 3, r2, 0))))
    wts_ref[...] = jnp.where(lane == 0, w1, jnp.where(lane == 1, w2, 0.0))


def _router(x2, router_w, router_b):
    T, D = x2.shape
    E = router_w.shape[1]
    tm = min(T, 512)
    rw = jnp.pad(router_w, ((0, 0), (0, LANES - E)))
    rb = jnp.pad(router_b, (0, LANES - E))[None, :]
    row = lambda i: (i, 0)
    const = lambda i: (0, 0)
    return pl.pallas_call(
        functools.partial(_router_kernel, n_experts=E),
        grid=(T // tm,),
        in_specs=[pl.BlockSpec((tm, D), row),
                  pl.BlockSpec((D, LANES), const),
                  pl.BlockSpec((1, LANES), const)],
        out_specs=[pl.BlockSpec((tm, LANES), row),
                   pl.BlockSpec((tm, LANES), row),
                   pl.BlockSpec((SUBLANES, LANES), const)],
        out_shape=[jax.ShapeDtypeStruct((T, LANES), I32),
                   jax.ShapeDtypeStruct((T, LANES), F32),
                   jax.ShapeDtypeStruct((SUBLANES, LANES), I32)],
        scratch_shapes=[pltpu.VMEM((SUBLANES, LANES), F32)],
        compiler_params=_params("arbitrary"),
        name="router",
    )(x2, rw, rb)


def _scatter_kernel(offs_ref, route_ref, x_ref, xg_in_ref, xg_ref, sem):
    del xg_in_ref
    tm = x_ref.shape[0]

    def copy(r, k):
        slot = offs_ref[route_ref[0, k, r]] + route_ref[0, TOP_K + k, r]
        return pltpu.make_async_copy(x_ref.at[pl.ds(r, 1)], xg_ref.at[pl.ds(slot, 1)], sem)

    def start(r, c):
        for k in range(TOP_K):
            copy(r, k).start()
        return c

    def wait(r, c):
        for k in range(TOP_K):
            copy(r, k).wait()
        return c

    lax.fori_loop(0, tm, start, 0, unroll=DMA_UNROLL)
    lax.fori_loop(0, tm, wait, 0, unroll=DMA_UNROLL)


def _scatter_rows(x2, route, offs, n_rows):
    T, D = x2.shape
    nt, _, tm = route.shape
    xg0 = jnp.zeros((n_rows, D), F32)
    return pl.pallas_call(
        _scatter_kernel,
        grid_spec=pltpu.PrefetchScalarGridSpec(
            num_scalar_prefetch=1,
            grid=(nt,),
            in_specs=[pl.BlockSpec((1, 2 * TOP_K, tm), lambda i, offs: (i, 0, 0),
                                   memory_space=pltpu.SMEM),
                      pl.BlockSpec((tm, D), lambda i, offs: (i, 0)),
                      pl.BlockSpec(memory_space=pl.ANY)],
            out_specs=pl.BlockSpec(memory_space=pl.ANY),
            scratch_shapes=[pltpu.SemaphoreType.DMA(())]),
        out_shape=jax.ShapeDtypeStruct((n_rows, D), F32),
        input_output_aliases={3: 0},
        compiler_params=_params("arbitrary"),
        name="moe_scatter",
    )(offs, route, x2, xg0)


def _expert_kernel(te_ref, na_ref, x_ref, wg_ref, wu_ref, wd_ref, o_ref, xb_ref, acc_ref):
    g = pl.program_id(0)
    f = pl.program_id(1)
    active = g < na_ref[0]

    @pl.when(active & (f == 0))
    def _start():
        xb_ref[...] = x_ref[...].astype(BF16)
        acc_ref[...] = jnp.zeros_like(acc_ref)

    @pl.when(active)
    def _accumulate():
        acc_ref[...] += _swiglu_chunk(xb_ref[...], wg_ref[...], wu_ref[...], wd_ref[...])

    last = f == pl.num_programs(1) - 1

    @pl.when(active & last)
    def _finish():
        o_ref[...] = acc_ref[...]

    @pl.when(jnp.logical_not(active) & last)
    def _unused_tile():
        o_ref[...] = jnp.zeros_like(o_ref)


def _expert_ffn(xg, wg, wu, wd, tile_expert, n_active, tm):
    P, D = xg.shape
    E, _, F = wg.shape
    tf = _hidden_chunk(F, 1792)
    nf = F // tf

    def tile(g, te, na):
        return jnp.minimum(g, na[0] - 1)

    def chunk(g, f, na):
        return jnp.where(g < na[0], f, nf - 1)

    return pl.pallas_call(
        _expert_kernel,
        grid_spec=pltpu.PrefetchScalarGridSpec(
            num_scalar_prefetch=2,
            grid=(P // tm, nf),
            in_specs=[pl.BlockSpec((tm, D), lambda g, f, te, na: (tile(g, te, na), 0)),
                      pl.BlockSpec((None, D, tf),
                                   lambda g, f, te, na: (te[tile(g, te, na)], 0, chunk(g, f, na))),
                      pl.BlockSpec((None, D, tf),
                                   lambda g, f, te, na: (te[tile(g, te, na)], 0, chunk(g, f, na))),
                      pl.BlockSpec((None, tf, D),
                                   lambda g, f, te, na: (te[tile(g, te, na)], chunk(g, f, na), 0))],
            out_specs=pl.BlockSpec((tm, D), lambda g, f, te, na: (g, 0)),
            scratch_shapes=[pltpu.VMEM((tm, D), BF16), pltpu.VMEM((tm, D), F32)]),
        out_shape=jax.ShapeDtypeStruct((P, D), F32),
        compiler_params=_params("arbitrary", "arbitrary"),
        name="moe_experts",
    )(tile_expert, n_active, xg, wg, wu, wd)


def _combine_kernel(offs_ref, route_ref, yg_ref, wts_ref, x_ref, g_ref, b_ref, o_ref,
                    buf_ref, sem, *, alpha):
    tm = x_ref.shape[0]

    def copy(r, k):
        slot = offs_ref[route_ref[0, k, r]] + route_ref[0, TOP_K + k, r]
        return pltpu.make_async_copy(yg_ref.at[pl.ds(slot, 1)], buf_ref.at[k, pl.ds(r, 1)], sem)

    def start(r, c):
        for k in range(TOP_K):
            copy(r, k).start()
        return c

    def wait(r, c):
        for k in range(TOP_K):
            copy(r, k).wait()
        return c

    lax.fori_loop(0, tm, start, 0, unroll=DMA_UNROLL)
    lax.fori_loop(0, tm, wait, 0, unroll=DMA_UNROLL)
    w = wts_ref[...]
    ffn = w[:, 0:1] * buf_ref[0] + w[:, 1:2] * buf_ref[1]
    o_ref[...] = _layer_norm(alpha * x_ref[...] + ffn, g_ref[...], b_ref[...])


def _combine(yg, route, offs, wts, x2, ln_g, ln_b, alpha):
    T, D = x2.shape
    nt, _, tm = route.shape
    row = lambda i, offs: (i, 0)
    const = lambda i, offs: (0, 0)
    return pl.pallas_call(
        functools.partial(_combine_kernel, alpha=alpha),
        grid_spec=pltpu.PrefetchScalarGridSpec(
            num_scalar_prefetch=1,
            grid=(nt,),
            in_specs=[pl.BlockSpec((1, 2 * TOP_K, tm), lambda i, offs: (i, 0, 0),
                                   memory_space=pltpu.SMEM),
                      pl.BlockSpec(memory_space=pl.ANY),
                      pl.BlockSpec((tm, LANES), row),
                      pl.BlockSpec((tm, D), row),
                      pl.BlockSpec((1, D), const),
                      pl.BlockSpec((1, D), const)],
            out_specs=pl.BlockSpec((tm, D), row),
            scratch_shapes=[pltpu.VMEM((TOP_K, tm, D), F32), pltpu.SemaphoreType.DMA(())]),
        out_shape=jax.ShapeDtypeStruct((T, D), F32),
        compiler_params=_params("arbitrary"),
        name="moe_combine_ln",
    )(offs, route, yg, wts, x2, ln_g, ln_b)


def _moe_ffn(x2, router_w, router_b, wg, wu, wd, ln_g, ln_b, alpha):
    T, D = x2.shape
    E = router_w.shape[1]
    tm_e = min(T, 512)
    tm_r = min(T, 256)
    ids, wts, counts = _router(x2, router_w, router_b)

    cnt = counts[0, :E]
    padded = ((cnt + tm_e - 1) // tm_e) * tm_e
    ends = jnp.cumsum(padded)
    offs = (ends - padded).astype(I32)
    n_rows = TOP_K * T + E * tm_e
    n_tiles = n_rows // tm_e
    tile_start = jnp.arange(n_tiles, dtype=I32) * tm_e
    tile_expert = jnp.minimum(jnp.sum(tile_start[:, None] >= ends[None, :], axis=1), E - 1).astype(I32)
    n_active = (ends[-1:] // tm_e).astype(I32)

    route = ids[:, :2 * TOP_K].reshape(T // tm_r, tm_r, 2 * TOP_K).transpose(0, 2, 1)
    xg = _scatter_rows(x2, route, offs, n_rows)
    yg = _expert_ffn(xg, wg, wu, wd, tile_expert, n_active, tm_e)
    return _combine(yg, route, offs, wts, x2, ln_g, ln_b, alpha)


def kernel(x, positions, w_in, gate_bias, conv_w, conv_b, norm_attn, norm_mlstm, w_out,
           ln1_g, ln1_b, dense_w_gate, dense_w_up, dense_w_down, router_w, router_b,
           moe_w_gate, moe_w_up, moe_w_down, ln2_g, ln2_b):
    B, S, D = x.shape
    depth = w_in.shape[0]
    T = B * S
    alpha = (2.0 * depth) ** 0.25
    n_main = 3 * ATT_WIDTH + 2 * ML_QK_WIDTH + 2 * ML_WIDTH
    nblk = S // MOBA_BLOCK

    tabs = _rope_tables(positions)
    x2 = x.reshape(T, D)
    for l in range(depth):
        w_main = w_in[l, :, :n_main].astype(BF16)
        w_gate = jnp.pad(w_in[l, :, n_main:], ((0, 0), (0, LANES - 2 * ML_HEADS))).astype(BF16)
        gbias = jnp.pad(gate_bias[l], (0, LANES - 2 * ML_HEADS))[None, :]
        q, k, v, mqk, mv, mo, gates, kmean = _inproj(x2, w_main, w_gate, gbias, tabs)
        kmean = jnp.pad(kmean.reshape(B, nblk, ATT_WIDTH), ((0, 0), (0, LANES - nblk), (0, 0)))
        att = _moba_attention(q, k, v, kmean, norm_attn[l][None, :], B, S)
        ml = _mlstm(mqk, mv, mo, gates, conv_w[l], conv_b[l][None, :],
                    norm_mlstm[l][None, :], B, S)
        x2 = _outproj(att, ml, w_out[l].astype(BF16), x2, ln1_g[l][None, :], ln1_b[l][None, :],
                      alpha)
        j = l // 2
        if l % 2 == 0:
            x2 = _dense_ffn(x2, dense_w_gate[j].astype(BF16), dense_w_up[j].astype(BF16),
                            dense_w_down[j].astype(BF16), ln2_g[l][None, :], ln2_b[l][None, :],
                            alpha)
        else:
            x2 = _moe_ffn(x2, router_w[j], router_b[j], moe_w_gate[j].astype(BF16),
                          moe_w_up[j].astype(BF16), moe_w_down[j].astype(BF16),
                          ln2_g[l][None, :], ln2_b[l][None, :], alpha)
    return x2.reshape(B, S, D)
```

```python
import functools

import jax
import jax.numpy as jnp
from jax import lax
from jax.experimental import pallas as pl
from jax.experimental.pallas import tpu as pltpu

F32 = jnp.float32
BF16 = jnp.bfloat16
I32 = jnp.int32

ATT_HEADS = 8
ATT_HEAD_DIM = 64
ATT_WIDTH = ATT_HEADS * ATT_HEAD_DIM
MOBA_BLOCK = 256
MOBA_TOPK = 3
ROPE_THETA = 500000.0
ROPE_DIM = ATT_HEAD_DIM // 4
ML_HEADS = 4
ML_V_DIM = 128
ML_QK_DIM = 64
ML_QK_WIDTH = ML_HEADS * ML_QK_DIM
ML_WIDTH = ML_HEADS * ML_V_DIM
ML_CHUNK = 128
CONV_WIDTH = 4
TOP_K = 2
LN_EPS = 1e-5
RMS_EPS = 1e-6

LANES = 128
SUBLANES = 8
V7X_VMEM_BYTES = 64 * 1024 * 1024
VMEM_LIMIT = (V7X_VMEM_BYTES * 3) // 4

MASK_BIAS = -1e9
MAX_LAG_RISE = 60.0
NEG_INF = float("-inf")
HIGHEST = lax.Precision.HIGHEST
LOG2_E = 1.4426950408889634
DMA_UNROLL = 8

_NT = (((1,), (1,)), ((), ()))


def _params(*sem):
    return pltpu.CompilerParams(dimension_semantics=sem, vmem_limit_bytes=VMEM_LIMIT)


def _iota(shape, dim):
    return lax.broadcasted_iota(I32, shape, dim)


def _sigmoid(x):
    return 1.0 / (1.0 + jnp.exp(-x))


def _layer_norm(y, g, b):
    mu = jnp.mean(y, axis=-1, keepdims=True)
    yc = y - mu
    var = jnp.mean(yc * yc, axis=-1, keepdims=True)
    return yc * lax.rsqrt(var + LN_EPS) * g + b


def _rope_tab_kernel(pos_ref, freq_ref, cos_ref, s1_ref, s2_ref):
    ang = pos_ref[...].astype(F32) * freq_ref[...]
    d = _iota(ang.shape, 1) & (ATT_HEAD_DIM - 1)
    half = ROPE_DIM // 2
    s = jnp.sin(ang)
    cos_ref[...] = jnp.cos(ang)
    s1_ref[...] = jnp.where(d < half, -s, 0.0)
    s2_ref[...] = jnp.where((d >= half) & (d < ROPE_DIM), s, 0.0)


def _rope_tables(positions):
    T = positions.size
    tm = min(T, 1024)
    half = ROPE_DIM // 2
    inv_freq = ROPE_THETA ** (-jnp.arange(half, dtype=F32) / half)
    d = jnp.arange(LANES) % ATT_HEAD_DIM
    freq = jnp.where(d < ROPE_DIM, inv_freq[d % half], 0.0).astype(F32)[None, :]
    tab = jax.ShapeDtypeStruct((T, LANES), F32)
    return pl.pallas_call(
        _rope_tab_kernel,
        grid=(T // tm,),
        in_specs=[pl.BlockSpec((tm, 1), lambda i: (i, 0)),
                  pl.BlockSpec((1, LANES), lambda i: (0, 0))],
        out_specs=[pl.BlockSpec((tm, LANES), lambda i: (i, 0))] * 3,
        out_shape=[tab, tab, tab],
        compiler_params=_params("parallel"),
        name="rope_tables",
    )(positions.reshape(T, 1), freq)


def _inproj_kernel(x_ref, w_ref, wg_ref, gb_ref, cos_ref, s1_ref, s2_ref,
                   q_ref, k_ref, v_ref, mqk_ref, mv_ref, mo_ref, g_ref, km_ref):
    xb = x_ref[...].astype(BF16)
    W = ATT_WIDTH

    def proj(c):
        return jnp.dot(xb, w_ref[:, c * W:(c + 1) * W], preferred_element_type=F32)

    rep = W // LANES
    cos = jnp.concatenate([cos_ref[...]] * rep, axis=1)
    s1 = jnp.concatenate([s1_ref[...]] * rep, axis=1)
    s2 = jnp.concatenate([s2_ref[...]] * rep, axis=1)
    half = ROPE_DIM // 2

    def rope(t):
        return t * cos + pltpu.roll(t, W - half, 1) * s1 + pltpu.roll(t, half, 1) * s2

    q_ref[...] = (rope(proj(0)) * (LOG2_E * ATT_HEAD_DIM ** -0.5)).astype(BF16)
    k = rope(proj(1))
    k_ref[...] = k.astype(BF16)
    for g in range(k.shape[0] // MOBA_BLOCK):
        blk = k[g * MOBA_BLOCK:(g + 1) * MOBA_BLOCK]
        km_ref[0, g:g + 1, :] = jnp.sum(blk, axis=0, keepdims=True) * (1.0 / MOBA_BLOCK)
    v_ref[...] = proj(2).astype(BF16)
    mqk_ref[...] = proj(3).astype(BF16)
    mv_ref[...] = proj(4).astype(BF16)
    mo_ref[...] = proj(5).astype(BF16)
    g_ref[...] = jnp.dot(xb, wg_ref[...], preferred_element_type=F32) + gb_ref[...]


def _inproj(x2, w_main, w_gate, gate_bias, tabs):
    T, D = x2.shape
    tm = min(T, 512)
    nkb = tm // MOBA_BLOCK
    W = ATT_WIDTH
    row = lambda i: (i, 0)
    const = lambda i: (0, 0)
    act = jax.ShapeDtypeStruct((T, W), BF16)
    return pl.pallas_call(
        _inproj_kernel,
        grid=(T // tm,),
        in_specs=[pl.BlockSpec((tm, D), row),
                  pl.BlockSpec(w_main.shape, const),
                  pl.BlockSpec(w_gate.shape, const),
                  pl.BlockSpec((1, LANES), const),
                  pl.BlockSpec((tm, LANES), row),
                  pl.BlockSpec((tm, LANES), row),
                  pl.BlockSpec((tm, LANES), row)],
        out_specs=[pl.BlockSpec((tm, W), row)] * 6
        + [pl.BlockSpec((tm, LANES), row),
           pl.BlockSpec((1, nkb, W), lambda i: (i, 0, 0))],
        out_shape=[act] * 6 + [jax.ShapeDtypeStruct((T, LANES), F32),
                               jax.ShapeDtypeStruct((T // tm, nkb, W), F32)],
        compiler_params=_params("parallel"),
        name="inproj",
    )(x2, w_main, w_gate, gate_bias, *tabs)


def _split_bf16(x, parts):
    out = []
    for _ in range(parts):
        hi = x.astype(BF16)
        out.append(hi)
        x = x - hi.astype(F32)
    return out


def _attn_kernel(q_ref, k_ref, v_ref, km_ref, gain_ref, o_ref, kaug_ref, vaug_ref, kabs_ref,
                 *, group):
    i = pl.program_id(2)
    tq = q_ref.shape[0]
    nblk = k_ref.shape[0] // MOBA_BLOCK
    lane = _iota((tq, LANES), 1)
    head_a = lane < ATT_HEAD_DIM

    @pl.when(i == 0)
    def _build_augmented_keys_values():
        blane = _iota((MOBA_BLOCK, LANES), 1)
        brow_k = _iota((LANES, MOBA_BLOCK), 0)
        in_a = blane < ATT_HEAD_DIM
        one = jnp.ones((MOBA_BLOCK, LANES), BF16)

        def fill(j, carry):
            rows = pl.ds(pl.multiple_of(j * MOBA_BLOCK, MOBA_BLOCK), MOBA_BLOCK)
            kaug_ref[:LANES, rows] = k_ref[rows, :].astype(F32).T.astype(BF16)
            kaug_ref[LANES:, rows] = jnp.where(brow_k == j, 1.0, 0.0).astype(BF16)
            vj = v_ref[rows, :]
            vaug_ref[0, rows, :] = jnp.where(in_a, vj, one)
            vaug_ref[1, rows, :] = jnp.where(in_a, one, vj)
            kj = jnp.abs(k_ref[rows, :].astype(F32))
            return jnp.maximum(carry, jnp.max(kj, axis=0, keepdims=True))

        kabs = lax.fori_loop(0, nblk, fill, jnp.zeros((1, LANES), F32))
        kabs_ref[...] = jnp.broadcast_to(kabs, kabs_ref.shape)

    q = q_ref[...]
    zero = jnp.zeros_like(q)
    nrow = -(-nblk // SUBLANES) * SUBLANES
    km_parts = _split_bf16(km_ref[:nrow, :], 3)
    brow = _iota((nrow, tq), 0)

    def select_bias(qh):
        gate = sum(lax.dot_general(part, qh, _NT, preferred_element_type=F32)
                   for part in km_parts)
        gate = jnp.where(brow < i, gate, NEG_INF)
        picked = brow < 0
        for _ in range(MOBA_TOPK):
            best = jnp.max(gate, axis=0, keepdims=True)
            first = jnp.min(jnp.where(gate == best, brow, nrow), axis=0, keepdims=True)
            hit = (brow == first) & (best > NEG_INF)
            picked = picked | hit
            gate = jnp.where(hit, NEG_INF, gate)
        bias = jnp.where(picked, 0.0, MASK_BIAS)
        if nrow < LANES:
            bias = jnp.concatenate([bias, jnp.full((LANES - nrow, tq), MASK_BIAS, F32)], axis=0)
        return bias.T.astype(BF16)

    qh = (jnp.where(head_a, q, zero), jnp.where(head_a, zero, q))
    qa = [jnp.concatenate([x, select_bias(x)], axis=1) for x in qh]

    own = pl.ds(pl.multiple_of(i * MOBA_BLOCK, MOBA_BLOCK), MOBA_BLOCK)
    kd = k_ref[own, :]
    causal = _iota((tq, MOBA_BLOCK), 1) <= _iota((tq, MOBA_BLOCK), 0)
    state = []
    for h in range(2):
        s = jnp.where(causal, lax.dot_general(qh[h], kd, _NT, preferred_element_type=F32),
                      NEG_INF)
        m = jnp.max(s, axis=1, keepdims=True)
        p = jnp.exp2(s - m).astype(BF16)
        state += [m, jnp.dot(p, vaug_ref[h, own, :], preferred_element_type=F32)]

    span = group * MOBA_BLOCK

    def exact_body(g, carry):
        rows = pl.ds(pl.multiple_of(g * span, span), span)
        kg = kaug_ref[:, rows]
        new = []
        for h in range(2):
            m, acc = carry[2 * h], carry[2 * h + 1]
            s = jnp.dot(qa[h], kg, preferred_element_type=F32)
            m_new = jnp.maximum(m, jnp.max(s, axis=1, keepdims=True))
            p = jnp.exp2(s - m_new).astype(BF16)
            acc = jnp.exp2(m - m_new) * acc + jnp.dot(p, vaug_ref[h, rows, :],
                                                     preferred_element_type=F32)
            new += [m_new, acc]
        return tuple(new)

    def lagged_body(g, carry):
        rows = pl.ds(pl.multiple_of(g * span, span), span)
        kg = kaug_ref[:, rows]
        new = []
        for h in range(2):
            m, acc = carry[2 * h], carry[2 * h + 1]
            s = jnp.dot(qa[h], kg, preferred_element_type=F32)
            p = jnp.exp2(s - m).astype(BF16)
            m_new = jnp.maximum(m, jnp.max(s, axis=1, keepdims=True))
            acc = jnp.exp2(m - m_new) * (acc + jnp.dot(p, vaug_ref[h, rows, :],
                                                       preferred_element_type=F32))
            new += [m_new, acc]
        return tuple(new)

    kabs = kabs_ref[0:1, :]
    rise = [jnp.sum(jnp.abs(qh[h].astype(F32)) * kabs, axis=1, keepdims=True) - state[2 * h]
            for h in range(2)]
    lag_ok = jnp.max(jnp.maximum(rise[0], rise[1])) <= MAX_LAG_RISE

    n_groups = lax.div(i + (group - 1), group)
    _, acc_a, _, acc_b = lax.cond(
        lag_ok,
        lambda: lax.fori_loop(0, n_groups, lagged_body, tuple(state)),
        lambda: lax.fori_loop(0, n_groups, exact_body, tuple(state)))
    num = jnp.where(head_a, acc_a, acc_b)
    den = jnp.where(head_a, pltpu.roll(acc_a, ATT_HEAD_DIM, 1), pltpu.roll(acc_b, ATT_HEAD_DIM, 1))
    o = num / den
    sq = o * o
    ms_a = jnp.sum(jnp.where(head_a, sq, 0.0), axis=1, keepdims=True)
    ms_b = jnp.sum(jnp.where(head_a, 0.0, sq), axis=1, keepdims=True)
    mean_sq = jnp.where(head_a, ms_a, ms_b) * (1.0 / ATT_HEAD_DIM)
    o_ref[...] = (o * lax.rsqrt(mean_sq + RMS_EPS) * gain_ref[...]).astype(BF16)


def _moba_attention(q, k, v, kmean, gain, B, S):
    T, W = q.shape
    nq = S // MOBA_BLOCK
    npairs = W // LANES
    group = 4 if nq % 4 == 0 else 1
    return pl.pallas_call(
        functools.partial(_attn_kernel, group=group),
        grid=(B, npairs, nq),
        in_specs=[pl.BlockSpec((MOBA_BLOCK, LANES), lambda b, h, i: (b * nq + i, h)),
                  pl.BlockSpec((S, LANES), lambda b, h, i: (b, h)),
                  pl.BlockSpec((S, LANES), lambda b, h, i: (b, h)),
                  pl.BlockSpec((None, LANES, LANES), lambda b, h, i: (b, 0, h)),
                  pl.BlockSpec((1, LANES), lambda b, h, i: (0, h))],
        out_specs=pl.BlockSpec((MOBA_BLOCK, LANES), lambda b, h, i: (b * nq + i, h)),
        out_shape=jax.ShapeDtypeStruct((T, W), BF16),
        scratch_shapes=[pltpu.VMEM((2 * LANES, S), BF16), pltpu.VMEM((2, S, LANES), BF16),
                        pltpu.VMEM((SUBLANES, LANES), F32)],
        compiler_params=_params("parallel", "parallel", "arbitrary"),
        name="moba_attention",
    )(q, k, v, kmean, gain)


def _mlstm_kernel(mqk_ref, mv_ref, mo_ref, g_ref, cw_ref, cb_ref, gain_ref, o_ref,
                  c_ref, m_ref, tail_ref, kt_ref):
    L = ML_CHUNK

    @pl.when(pl.program_id(1) == 0)
    def _reset():
        c_ref[...] = jnp.zeros_like(c_ref)
        m_ref[...] = jnp.zeros_like(m_ref)
        tail_ref[...] = jnp.zeros_like(tail_ref)

    cur = mqk_ref[...].astype(F32)
    xp = jnp.concatenate([tail_ref[...], cur], axis=0)
    base = SUBLANES - (CONV_WIDTH - 1)
    y = cb_ref[...] + cw_ref[0:1, :] * xp[base:base + L]
    for j in range(1, CONV_WIDTH):
        y = y + cw_ref[j:j + 1, :] * xp[base + j:base + j + L]
    tail_ref[...] = cur[L - SUBLANES:]
    y = y * _sigmoid(y)

    lane = _iota((L, LANES), 1)
    row = _iota((L, LANES), 0)
    tri = lane <= row
    g8 = g_ref[...].T[:SUBLANES]
    grow = _iota((SUBLANES, L), 0)
    logf = -(jnp.maximum(-g8, 0.0) + jnp.log1p(jnp.exp(-jnp.abs(g8))))
    upper = (row <= lane).astype(F32)
    bcum = jnp.dot(jnp.where(grow >= ML_HEADS, logf, 0.0), upper, precision=HIGHEST,
                   preferred_element_type=F32)
    rows8 = jnp.where(grow < ML_HEADS, g8, bcum)
    cols = jnp.concatenate([rows8, jnp.zeros((LANES - SUBLANES, L), F32)], axis=0).T

    for p in range(ML_HEADS // 2):
        kt_ref[p] = y[:, ML_QK_WIDTH + p * LANES:ML_QK_WIDTH + (p + 1) * LANES].T
    ones_col = jnp.where(lane == 0, 1.0, 0.0).astype(BF16)
    for h in range(ML_HEADS):
        pair, odd = divmod(h, 2)
        in_head = (lane >= odd * ML_QK_DIM) & (lane < (odd + 1) * ML_QK_DIM)
        dim_in_head = (row >= odd * ML_QK_DIM) & (row < (odd + 1) * ML_QK_DIM)
        yq = y[:, pair * LANES:(pair + 1) * LANES]
        qh = jnp.where(in_head, yq * (ML_QK_DIM ** -0.5), 0.0).astype(BF16)
        kt = jnp.where(dim_in_head, kt_ref[pair], 0.0).astype(BF16)
        vh = mv_ref[:, h * ML_V_DIM:(h + 1) * ML_V_DIM]
        vaug = jnp.concatenate([vh, ones_col], axis=1)

        i_c = cols[:, h:h + 1]
        b_c = cols[:, ML_HEADS + h:ML_HEADS + h + 1]
        i_r = rows8[h:h + 1, :]
        b_r = rows8[ML_HEADS + h:ML_HEADS + h + 1, :]
        m_prev = m_ref[h:h + 1, 0:1]

        d_log = jnp.where(tri, b_c - b_r + i_r, NEG_INF)
        inter = b_c + m_prev
        m_t = jnp.maximum(inter, jnp.max(d_log, axis=1, keepdims=True))
        w_intra = jnp.exp(d_log - m_t)
        w_inter = jnp.exp(inter - m_t)

        state = c_ref[h]
        s = jnp.dot(qh, kt, preferred_element_type=F32) * w_intra
        numden = (jnp.dot(s.astype(BF16), vaug, preferred_element_type=F32)
                  + w_inter * jnp.dot(qh, state.astype(BF16), preferred_element_type=F32))
        num = numden[:, :ML_V_DIM]
        den = numden[:, ML_V_DIM:ML_V_DIM + 1]
        hcur = num / jnp.maximum(jnp.abs(den), jnp.exp(-m_t))

        b_last = b_c[L - 1:L, :]
        decay = b_last - b_c + i_c
        m_new = jnp.maximum(b_last + m_prev, jnp.max(decay, axis=0, keepdims=True))
        w_state = jnp.exp(decay - m_new)
        carry_scale = jnp.exp(b_last + m_prev - m_new)
        vw = (vaug.astype(F32) * w_state).astype(BF16)
        c_ref[h] = carry_scale * state + jnp.dot(kt, vw, preferred_element_type=F32)
        m_ref[h:h + 1, :] = jnp.broadcast_to(m_new, (1, LANES))

        mean_sq = jnp.mean(hcur * hcur, axis=1, keepdims=True)
        sl = slice(h * ML_V_DIM, (h + 1) * ML_V_DIM)
        gate = _sigmoid(mo_ref[:, sl].astype(F32))
        o_ref[:, sl] = (hcur * lax.rsqrt(mean_sq + RMS_EPS) * gain_ref[:, sl]
                        * gate).astype(BF16)


def _mlstm(mqk, mv, mo, gates, conv_w, conv_b, gain, B, S):
    T, W = mv.shape
    L = ML_CHUNK
    nc = S // L
    row = lambda b, c: (b * nc + c, 0)
    const = lambda b, c: (0, 0)
    return pl.pallas_call(
        _mlstm_kernel,
        grid=(B, nc),
        in_specs=[pl.BlockSpec((L, W), row),
                  pl.BlockSpec((L, W), row),
                  pl.BlockSpec((L, W), row),
                  pl.BlockSpec((L, LANES), row),
                  pl.BlockSpec(conv_w.shape, const),
                  pl.BlockSpec((1, W), const),
                  pl.BlockSpec((1, W), const)],
        out_specs=pl.BlockSpec((L, W), row),
        out_shape=jax.ShapeDtypeStruct((T, W), BF16),
        scratch_shapes=[pltpu.VMEM((ML_HEADS, LANES, 2 * ML_V_DIM), F32),
                        pltpu.VMEM((SUBLANES, LANES), F32),
                        pltpu.VMEM((SUBLANES, W), F32),
                        pltpu.VMEM((ML_HEADS // 2, LANES, L), F32)],
        compiler_params=_params("parallel", "arbitrary"),
        name="mlstm",
    )(mqk, mv, mo, gates, conv_w, conv_b, gain)


def _outproj_kernel(att_ref, ml_ref, wa_ref, wb_ref, x_ref, g_ref, b_ref, o_ref, *, alpha):
    mix = (jnp.dot(att_ref[...], wa_ref[...], preferred_element_type=F32)
           + jnp.dot(ml_ref[...], wb_ref[...], preferred_element_type=F32))
    o_ref[...] = _layer_norm(alpha * x_ref[...] + mix, g_ref[...], b_ref[...])


def _outproj(att, ml, w_out, x2, ln_g, ln_b, alpha):
    T, D = x2.shape
    W = att.shape[1]
    tm = min(T, 512)
    row = lambda i: (i, 0)
    const = lambda i: (0, 0)
    return pl.pallas_call(
        functools.partial(_outproj_kernel, alpha=alpha),
        grid=(T // tm,),
        in_specs=[pl.BlockSpec((tm, W), row),
                  pl.BlockSpec((tm, W), row),
                  pl.BlockSpec((W, D), lambda i: (0, 0)),
                  pl.BlockSpec((W, D), lambda i: (1, 0)),
                  pl.BlockSpec((tm, D), row),
                  pl.BlockSpec((1, D), const),
                  pl.BlockSpec((1, D), const)],
        out_specs=pl.BlockSpec((tm, D), row),
        out_shape=jax.ShapeDtypeStruct((T, D), F32),
        compiler_params=_params("parallel"),
        name="outproj_ln",
    )(att, ml, w_out, w_out, x2, ln_g, ln_b)


def _swiglu_chunk(xb, wg, wu, wd):
    g = jnp.dot(xb, wg, preferred_element_type=F32)
    u = jnp.dot(xb, wu, preferred_element_type=F32)
    h = (g * _sigmoid(g) * u).astype(BF16)
    return jnp.dot(h, wd, preferred_element_type=F32)


def _dense_ffn_kernel(x_ref, wg_ref, wu_ref, wd_ref, g_ref, b_ref, o_ref, *, alpha, chunk):
    x = x_ref[...]
    xb = x.astype(BF16)
    F = wg_ref.shape[1]
    acc = None
    for a in range(0, F, chunk):
        b = min(a + chunk, F)
        part = _swiglu_chunk(xb, wg_ref[:, a:b], wu_ref[:, a:b], wd_ref[a:b, :])
        acc = part if acc is None else acc + part
    o_ref[...] = _layer_norm(alpha * x + acc, g_ref[...], b_ref[...])


def _hidden_chunk(width, target):
    for unit in (2 * LANES, LANES):
        best = 0
        for c in range(unit, target + 1, unit):
            if width % c == 0:
                best = c
        if best:
            return best
    return width


def _dense_ffn(x2, wg, wu, wd, ln_g, ln_b, alpha):
    T, D = x2.shape
    F = wg.shape[1]
    tm = min(T, 512)
    row = lambda i: (i, 0)
    const = lambda i: (0, 0)
    resident = dict(pipeline_mode=pl.Buffered(1))
    return pl.pallas_call(
        functools.partial(_dense_ffn_kernel, alpha=alpha, chunk=4 * LANES),
        grid=(T // tm,),
        in_specs=[pl.BlockSpec((tm, D), row),
                  pl.BlockSpec((D, F), const, **resident),
                  pl.BlockSpec((D, F), const, **resident),
                  pl.BlockSpec((F, D), const, **resident),
                  pl.BlockSpec((1, D), const),
                  pl.BlockSpec((1, D), const)],
        out_specs=pl.BlockSpec((tm, D), row),
        out_shape=jax.ShapeDtypeStruct((T, D), F32),
        compiler_params=_params("parallel"),
        name="dense_ffn_ln",
    )(x2, wg, wu, wd, ln_g, ln_b)


def _router_kernel(x_ref, rw_ref, rb_ref, ids_ref, wts_ref, cnt_ref, carry_ref, *, n_experts):
    @pl.when(pl.program_id(0) == 0)
    def _reset():
        carry_ref[...] = jnp.zeros_like(carry_ref)

    tm = x_ref.shape[0]
    x_hi, x_lo = _split_bf16(x_ref[...], 2)
    w_hi, w_lo = _split_bf16(rw_ref[...], 2)
    logits = (jnp.dot(x_hi, w_hi, preferred_element_type=F32)
              + (jnp.dot(x_lo, w_hi, preferred_element_type=F32)
                 + jnp.dot(x_hi, w_lo, preferred_element_type=F32))) + rb_ref[...]
    lane = _iota((tm, LANES), 1)
    logits = jnp.where(lane < n_experts, logits, NEG_INF)

    def top(lg):
        best = jnp.max(lg, axis=1, keepdims=True)
        first = jnp.min(jnp.where(lg == best, lane, LANES), axis=1, keepdims=True)
        return best, first

    v1, e1 = top(logits)
    hot1 = lane == e1
    v2, e2 = top(jnp.where(hot1, NEG_INF, logits))
    hot2 = lane == e2
    ex = jnp.exp(v2 - v1)
    w1 = 1.0 / (1.0 + ex)
    w2 = ex / (1.0 + ex)

    assigned = (hot1 | hot2).astype(BF16)
    before = (_iota((tm, tm), 1) < _iota((tm, tm), 0)).astype(BF16)
    carry = carry_ref[0:1, :]
    rank = jnp.dot(before, assigned, preferred_element_type=F32) + carry
    r1 = jnp.sum(jnp.where(hot1, rank, 0.0), axis=1, keepdims=True).astype(I32)
    r2 = jnp.sum(jnp.where(hot2, rank, 0.0), axis=1, keepdims=True).astype(I32)
    total = carry + jnp.sum(assigned.astype(F32), axis=0, keepdims=True)
    carry_ref[...] = jnp.broadcast_to(total, carry_ref.shape)
    cnt_ref[...] = jnp.broadcast_to(total, cnt_ref.shape).astype(I32)

    ids_ref[...] = jnp.where(lane == 0, e1, jnp.where(lane == 1, e2,
                             jnp.where(lane == 2, r1, jnp.where(lane == 3, r2, 0))))
    wts_ref[...] = jnp.where(lane == 0, w1, jnp.where(lane == 1, w2, 0.0))


def _router(x2, router_w, router_b):
    T, D = x2.shape
    E = router_w.shape[1]
    tm = min(T, 512)
    rw = jnp.pad(router_w, ((0, 0), (0, LANES - E)))
    rb = jnp.pad(router_b, (0, LANES - E))[None, :]
    row = lambda i: (i, 0)
    const = lambda i: (0, 0)
    return pl.pallas_call(
        functools.partial(_router_kernel, n_experts=E),
        grid=(T // tm,),
        in_specs=[pl.BlockSpec((tm, D), row),
                  pl.BlockSpec((D, LANES), const),
                  pl.BlockSpec((1, LANES), const)],
        out_specs=[pl.BlockSpec((tm, LANES), row),
                   pl.BlockSpec((tm, LANES), row),
                   pl.BlockSpec((SUBLANES, LANES), const)],
        out_shape=[jax.ShapeDtypeStruct((T, LANES), I32),
                   jax.ShapeDtypeStruct((T, LANES), F32),
                   jax.ShapeDtypeStruct((SUBLANES, LANES), I32)],
        scratch_shapes=[pltpu.VMEM((SUBLANES, LANES), F32)],
        compiler_params=_params("arbitrary"),
        name="router",
    )(x2, rw, rb)


def _scatter_kernel(offs_ref, route_ref, x_ref, xg_in_ref, xg_ref, sem):
    del xg_in_ref
    tm = x_ref.shape[0]

    def copy(r, k):
        slot = offs_ref[route_ref[0, k, r]] + route_ref[0, TOP_K + k, r]
        return pltpu.make_async_copy(x_ref.at[pl.ds(r, 1)], xg_ref.at[pl.ds(slot, 1)], sem)

    def start(r, c):
        for k in range(TOP_K):
            copy(r, k).start()
        return c

    def wait(r, c):
        for k in range(TOP_K):
            copy(r, k).wait()
        return c

    lax.fori_loop(0, tm, start, 0, unroll=DMA_UNROLL)
    lax.fori_loop(0, tm, wait, 0, unroll=DMA_UNROLL)


def _scatter_rows(x2, route, offs, n_rows):
    T, D = x2.shape
    nt, _, tm = route.shape
    xg0 = jnp.zeros((n_rows, D), F32)
    return pl.pallas_call(
        _scatter_kernel,
        grid_spec=pltpu.PrefetchScalarGridSpec(
            num_scalar_prefetch=1,
            grid=(nt,),
            in_specs=[pl.BlockSpec((1, 2 * TOP_K, tm), lambda i, offs: (i, 0, 0),
                                   memory_space=pltpu.SMEM),
                      pl.BlockSpec((tm, D), lambda i, offs: (i, 0)),
                      pl.BlockSpec(memory_space=pl.ANY)],
            out_specs=pl.BlockSpec(memory_space=pl.ANY),
            scratch_shapes=[pltpu.SemaphoreType.DMA(())]),
        out_shape=jax.ShapeDtypeStruct((n_rows, D), F32),
        input_output_aliases={3: 0},
        compiler_params=_params("arbitrary"),
        name="moe_scatter",
    )(offs, route, x2, xg0)


def _expert_kernel(te_ref, na_ref, x_ref, wg_ref, wu_ref, wd_ref, o_ref, xb_ref, acc_ref):
    g = pl.program_id(0)
    f = pl.program_id(1)
    active = g < na_ref[0]

    @pl.when(active & (f == 0))
    def _start():
        xb_ref[...] = x_ref[...].astype(BF16)
        acc_ref[...] = jnp.zeros_like(acc_ref)

    @pl.when(active)
    def _accumulate():
        acc_ref[...] += _swiglu_chunk(xb_ref[...], wg_ref[...], wu_ref[...], wd_ref[...])

    last = f == pl.num_programs(1) - 1

    @pl.when(active & last)
    def _finish():
        o_ref[...] = acc_ref[...]

    @pl.when(jnp.logical_not(active) & last)
    def _unused_tile():
        o_ref[...] = jnp.zeros_like(o_ref)


def _expert_ffn(xg, wg, wu, wd, tile_expert, n_active, tm):
    P, D = xg.shape
    E, _, F = wg.shape
    tf = _hidden_chunk(F, 1792)
    nf = F // tf

    def tile(g, te, na):
        return jnp.maximum(jnp.minimum(g, na[0] - 1), 0)

    def chunk(g, f, na):
        return jnp.where(g < na[0], f, nf - 1)

    return pl.pallas_call(
        _expert_kernel,
        grid_spec=pltpu.PrefetchScalarGridSpec(
            num_scalar_prefetch=2,
            grid=(P // tm, nf),
            in_specs=[pl.BlockSpec((tm, D), lambda g, f, te, na: (tile(g, te, na), 0)),
                      pl.BlockSpec((None, D, tf),
                                   lambda g, f, te, na: (te[tile(g, te, na)], 0, chunk(g, f, na))),
                      pl.BlockSpec((None, D, tf),
                                   lambda g, f, te, na: (te[tile(g, te, na)], 0, chunk(g, f, na))),
                      pl.BlockSpec((None, tf, D),
                                   lambda g, f, te, na: (te[tile(g, te, na)], chunk(g, f, na), 0))],
            out_specs=pl.BlockSpec((tm, D), lambda g, f, te, na: (g, 0)),
            scratch_shapes=[pltpu.VMEM((tm, D), BF16), pltpu.VMEM((tm, D), F32)]),
        out_shape=jax.ShapeDtypeStruct((P, D), F32),
        compiler_params=_params("arbitrary", "arbitrary"),
        name="moe_experts",
    )(tile_expert, n_active, xg, wg, wu, wd)


def _combine_kernel(offs_ref, route_ref, yg_ref, wts_ref, x_ref, g_ref, b_ref, o_ref,
                    buf_ref, sem, *, alpha):
    tm = x_ref.shape[0]

    def copy(r, k):
        slot = offs_ref[route_ref[0, k, r]] + route_ref[0, TOP_K + k, r]
        return pltpu.make_async_copy(yg_ref.at[pl.ds(slot, 1)], buf_ref.at[k, pl.ds(r, 1)], sem)

    def start(r, c):
        for k in range(TOP_K):
            copy(r, k).start()
        return c

    def wait(r, c):
        for k in range(TOP_K):
            copy(r, k).wait()
        return c

    lax.fori_loop(0, tm, start, 0, unroll=DMA_UNROLL)
    lax.fori_loop(0, tm, wait, 0, unroll=DMA_UNROLL)
    w = wts_ref[...]
    ffn = w[:, 0:1] * buf_ref[0] + w[:, 1:2] * buf_ref[1]
    o_ref[...] = _layer_norm(alpha * x_ref[...] + ffn, g_ref[...], b_ref[...])


def _combine(yg, route, offs, wts, x2, ln_g, ln_b, alpha):
    T, D = x2.shape
    nt, _, tm = route.shape
    row = lambda i, offs: (i, 0)
    const = lambda i, offs: (0, 0)
    return pl.pallas_call(
        functools.partial(_combine_kernel, alpha=alpha),
        grid_spec=pltpu.PrefetchScalarGridSpec(
            num_scalar_prefetch=1,
            grid=(nt,),
            in_specs=[pl.BlockSpec((1, 2 * TOP_K, tm), lambda i, offs: (i, 0, 0),
                                   memory_space=pltpu.SMEM),
                      pl.BlockSpec(memory_space=pl.ANY),
                      pl.BlockSpec((tm, LANES), row),
                      pl.BlockSpec((tm, D), row),
                      pl.BlockSpec((1, D), const),
                      pl.BlockSpec((1, D), const)],
            out_specs=pl.BlockSpec((tm, D), row),
            scratch_shapes=[pltpu.VMEM((TOP_K, tm, D), F32), pltpu.SemaphoreType.DMA(())]),
        out_shape=jax.ShapeDtypeStruct((T, D), F32),
        compiler_params=_params("arbitrary"),
        name="moe_combine_ln",
    )(offs, route, yg, wts, x2, ln_g, ln_b)


def _moe_ffn(x2, router_w, router_b, wg, wu, wd, ln_g, ln_b, alpha):
    T, D = x2.shape
    E = router_w.shape[1]
    tm_e = min(T, 512)
    tm_r = min(T, 256)
    ids, wts, counts = _router(x2, router_w, router_b)

    cnt = counts[0, :E]
    padded = ((cnt + tm_e - 1) // tm_e) * tm_e
    ends = jnp.cumsum(padded)
    offs = (ends - padded).astype(I32)
    n_rows = TOP_K * T + E * tm_e
    n_tiles = n_rows // tm_e
    tile_start = jnp.arange(n_tiles, dtype=I32) * tm_e
    tile_expert = jnp.minimum(jnp.sum(tile_start[:, None] >= ends[None, :], axis=1), E - 1).astype(I32)
    n_active = (ends[-1:] // tm_e).astype(I32)

    route = ids[:, :2 * TOP_K].reshape(T // tm_r, tm_r, 2 * TOP_K).transpose(0, 2, 1)
    xg = _scatter_rows(x2, route, offs, n_rows)
    yg = _expert_ffn(xg, wg, wu, wd, tile_expert, n_active, tm_e)
    return _combine(yg, route, offs, wts, x2, ln_g, ln_b, alpha)


def kernel(x, positions, w_in, gate_bias, conv_w, conv_b, norm_attn, norm_mlstm, w_out,
           ln1_g, ln1_b, dense_w_gate, dense_w_up, dense_w_down, router_w, router_b,
           moe_w_gate, moe_w_up, moe_w_down, ln2_g, ln2_b):
    B, S, D = x.shape
    depth = w_in.shape[0]
    T = B * S
    alpha = (2.0 * depth) ** 0.25
    n_main = 3 * ATT_WIDTH + 2 * ML_QK_WIDTH + 2 * ML_WIDTH
    nblk = S // MOBA_BLOCK

    tabs = _rope_tables(positions)
    x2 = x.reshape(T, D)
    for l in range(depth):
        w_main = w_in[l, :, :n_main].astype(BF16)
        w_gate = jnp.pad(w_in[l, :, n_main:], ((0, 0), (0, LANES - 2 * ML_HEADS))).astype(BF16)
        gbias = jnp.pad(gate_bias[l], (0, LANES - 2 * ML_HEADS))[None, :]
        q, k, v, mqk, mv, mo, gates, kmean = _inproj(x2, w_main, w_gate, gbias, tabs)
        kmean = jnp.pad(kmean.reshape(B, nblk, ATT_WIDTH), ((0, 0), (0, LANES - nblk), (0, 0)))
        att = _moba_attention(q, k, v, kmean, norm_attn[l][None, :], B, S)
        ml = _mlstm(mqk, mv, mo, gates, conv_w[l], conv_b[l][None, :],
                    norm_mlstm[l][None, :], B, S)
        x2 = _outproj(att, ml, w_out[l].astype(BF16), x2, ln1_g[l][None, :], ln1_b[l][None, :],
                      alpha)
        j = l // 2
        if l % 2 == 0:
            x2 = _dense_ffn(x2, dense_w_gate[j].astype(BF16), dense_w_up[j].astype(BF16),
                            dense_w_down[j].astype(BF16), ln2_g[l][None, :], ln2_b[l][None, :],
                            alpha)
        else:
            x2 = _moe_ffn(x2, router_w[j], router_b[j], moe_w_gate[j].astype(BF16),
                          moe_w_up[j].astype(BF16), moe_w_down[j].astype(BF16),
                          ln2_g[l][None, :], ln2_b[l][None, :], alpha)
    return x2.reshape(B, S, D)
```

```python
import functools

import jax
import jax.numpy as jnp
from jax import lax
from jax.experimental import pallas as pl
from jax.experimental.pallas import tpu as pltpu

F32 = jnp.float32
BF16 = jnp.bfloat16
I32 = jnp.int32

ATT_HEADS = 8
ATT_HEAD_DIM = 64
ATT_WIDTH = ATT_HEADS * ATT_HEAD_DIM
MOBA_BLOCK = 256
MOBA_TOPK = 3
ROPE_THETA = 500000.0
ROPE_DIM = ATT_HEAD_DIM // 4
ML_HEADS = 4
ML_V_DIM = 128
ML_QK_DIM = 64
ML_QK_WIDTH = ML_HEADS * ML_QK_DIM
ML_WIDTH = ML_HEADS * ML_V_DIM
ML_CHUNK = 128
CONV_WIDTH = 4
TOP_K = 2
LN_EPS = 1e-5
RMS_EPS = 1e-6

LANES = 128
SUBLANES = 8
V7X_VMEM_BYTES = 64 * 1024 * 1024
VMEM_LIMIT = (V7X_VMEM_BYTES * 3) // 4

MASK_BIAS = -1e9
MAX_LAG_RISE = 60.0
NEG_INF = float("-inf")
HIGHEST = lax.Precision.HIGHEST
LOG2_E = 1.4426950408889634
DMA_UNROLL = 8

_NT = (((1,), (1,)), ((), ()))


def _params(*sem):
    return pltpu.CompilerParams(dimension_semantics=sem, vmem_limit_bytes=VMEM_LIMIT)


def _iota(shape, dim):
    return lax.broadcasted_iota(I32, shape, dim)


def _sigmoid(x):
    return 1.0 / (1.0 + jnp.exp(-x))


def _layer_norm(y, g, b):
    mu = jnp.mean(y, axis=-1, keepdims=True)
    yc = y - mu
    var = jnp.mean(yc * yc, axis=-1, keepdims=True)
    return yc * lax.rsqrt(var + LN_EPS) * g + b


def _rope_tab_kernel(pos_ref, freq_ref, cos_ref, s1_ref, s2_ref):
    ang = pos_ref[...].astype(F32) * freq_ref[...]
    d = _iota(ang.shape, 1) & (ATT_HEAD_DIM - 1)
    half = ROPE_DIM // 2
    s = jnp.sin(ang)
    cos_ref[...] = jnp.cos(ang)
    s1_ref[...] = jnp.where(d < half, -s, 0.0)
    s2_ref[...] = jnp.where((d >= half) & (d < ROPE_DIM), s, 0.0)


def _rope_tables(positions):
    T = positions.size
    tm = min(T, 1024)
    half = ROPE_DIM // 2
    inv_freq = ROPE_THETA ** (-jnp.arange(half, dtype=F32) / half)
    d = jnp.arange(LANES) % ATT_HEAD_DIM
    freq = jnp.where(d < ROPE_DIM, inv_freq[d % half], 0.0).astype(F32)[None, :]
    tab = jax.ShapeDtypeStruct((T, LANES), F32)
    return pl.pallas_call(
        _rope_tab_kernel,
        grid=(T // tm,),
        in_specs=[pl.BlockSpec((tm, 1), lambda i: (i, 0)),
                  pl.BlockSpec((1, LANES), lambda i: (0, 0))],
        out_specs=[pl.BlockSpec((tm, LANES), lambda i: (i, 0))] * 3,
        out_shape=[tab, tab, tab],
        compiler_params=_params("parallel"),
        name="rope_tables",
    )(positions.reshape(T, 1), freq)


def _inproj_kernel(x_ref, w_ref, wg_ref, gb_ref, cos_ref, s1_ref, s2_ref,
                   q_ref, k_ref, v_ref, mqk_ref, mv_ref, mo_ref, g_ref, km_ref):
    xb = x_ref[...].astype(BF16)
    W = ATT_WIDTH

    def proj(c):
        return jnp.dot(xb, w_ref[:, c * W:(c + 1) * W], preferred_element_type=F32)

    rep = W // LANES
    cos = jnp.concatenate([cos_ref[...]] * rep, axis=1)
    s1 = jnp.concatenate([s1_ref[...]] * rep, axis=1)
    s2 = jnp.concatenate([s2_ref[...]] * rep, axis=1)
    half = ROPE_DIM // 2

    def rope(t):
        return t * cos + pltpu.roll(t, W - half, 1) * s1 + pltpu.roll(t, half, 1) * s2

    q_ref[...] = (rope(proj(0)) * (LOG2_E * ATT_HEAD_DIM ** -0.5)).astype(BF16)
    k = rope(proj(1))
    k_ref[...] = k.astype(BF16)
    for g in range(k.shape[0] // MOBA_BLOCK):
        blk = k[g * MOBA_BLOCK:(g + 1) * MOBA_BLOCK]
        km_ref[0, g:g + 1, :] = jnp.sum(blk, axis=0, keepdims=True) * (1.0 / MOBA_BLOCK)
    v_ref[...] = proj(2).astype(BF16)
    mqk_ref[...] = proj(3).astype(BF16)
    mv_ref[...] = proj(4).astype(BF16)
    mo_ref[...] = proj(5).astype(BF16)
    g_ref[...] = jnp.dot(xb, wg_ref[...], preferred_element_type=F32) + gb_ref[...]


def _inproj(x2, w_main, w_gate, gate_bias, tabs):
    T, D = x2.shape
    tm = min(T, 512)
    nkb = tm // MOBA_BLOCK
    W = ATT_WIDTH
    row = lambda i: (i, 0)
    const = lambda i: (0, 0)
    act = jax.ShapeDtypeStruct((T, W), BF16)
    return pl.pallas_call(
        _inproj_kernel,
        grid=(T // tm,),
        in_specs=[pl.BlockSpec((tm, D), row),
                  pl.BlockSpec(w_main.shape, const),
                  pl.BlockSpec(w_gate.shape, const),
                  pl.BlockSpec((1, LANES), const),
                  pl.BlockSpec((tm, LANES), row),
                  pl.BlockSpec((tm, LANES), row),
                  pl.BlockSpec((tm, LANES), row)],
        out_specs=[pl.BlockSpec((tm, W), row)] * 6
        + [pl.BlockSpec((tm, LANES), row),
           pl.BlockSpec((1, nkb, W), lambda i: (i, 0, 0))],
        out_shape=[act] * 6 + [jax.ShapeDtypeStruct((T, LANES), F32),
                               jax.ShapeDtypeStruct((T // tm, nkb, W), F32)],
        compiler_params=_params("parallel"),
        name="inproj",
    )(x2, w_main, w_gate, gate_bias, *tabs)


def _split_bf16(x, parts):
    out = []
    for _ in range(parts):
        hi = x.astype(BF16)
        out.append(hi)
        x = x - hi.astype(F32)
    return out


def _attn_kernel(q_ref, k_ref, v_ref, km_ref, gain_ref, o_ref, kaug_ref, vaug_ref, kabs_ref,
                 *, group):
    i = pl.program_id(2)
    tq = q_ref.shape[0]
    nblk = k_ref.shape[0] // MOBA_BLOCK
    lane = _iota((tq, LANES), 1)
    head_a = lane < ATT_HEAD_DIM

    @pl.when(i == 0)
    def _build_augmented_keys_values():
        blane = _iota((MOBA_BLOCK, LANES), 1)
        brow_k = _iota((LANES, MOBA_BLOCK), 0)
        in_a = blane < ATT_HEAD_DIM
        one = jnp.ones((MOBA_BLOCK, LANES), BF16)

        def fill(j, carry):
            rows = pl.ds(pl.multiple_of(j * MOBA_BLOCK, MOBA_BLOCK), MOBA_BLOCK)
            kaug_ref[:LANES, rows] = k_ref[rows, :].astype(F32).T.astype(BF16)
            kaug_ref[LANES:, rows] = jnp.where(brow_k == j, 1.0, 0.0).astype(BF16)
            vj = v_ref[rows, :]
            vaug_ref[0, rows, :] = jnp.where(in_a, vj, one)
            vaug_ref[1, rows, :] = jnp.where(in_a, one, vj)
            kj = jnp.abs(k_ref[rows, :].astype(F32))
            return jnp.maximum(carry, jnp.max(kj, axis=0, keepdims=True))

        kabs = lax.fori_loop(0, nblk, fill, jnp.zeros((1, LANES), F32))
        kabs_ref[...] = jnp.broadcast_to(kabs, kabs_ref.shape)

    q = q_ref[...]
    zero = jnp.zeros_like(q)
    nrow = -(-nblk // SUBLANES) * SUBLANES
    km_parts = _split_bf16(km_ref[:nrow, :], 3)
    brow = _iota((nrow, tq), 0)

    def select_bias(qh):
        gate = sum(lax.dot_general(part, qh, _NT, preferred_element_type=F32)
                   for part in km_parts)
        gate = jnp.where(brow < i, gate, NEG_INF)
        picked = brow < 0
        for _ in range(MOBA_TOPK):
            best = jnp.max(gate, axis=0, keepdims=True)
            first = jnp.min(jnp.where(gate == best, brow, nrow), axis=0, keepdims=True)
            hit = (brow == first) & (best > NEG_INF)
            picked = picked | hit
            gate = jnp.where(hit, NEG_INF, gate)
        bias = jnp.where(picked, 0.0, MASK_BIAS)
        if nrow < LANES:
            bias = jnp.concatenate([bias, jnp.full((LANES - nrow, tq), MASK_BIAS, F32)], axis=0)
        return bias.T.astype(BF16)

    qh = (jnp.where(head_a, q, zero), jnp.where(head_a, zero, q))
    qa = [jnp.concatenate([x, select_bias(x)], axis=1) for x in qh]

    own = pl.ds(pl.multiple_of(i * MOBA_BLOCK, MOBA_BLOCK), MOBA_BLOCK)
    kd = k_ref[own, :]
    causal = _iota((tq, MOBA_BLOCK), 1) <= _iota((tq, MOBA_BLOCK), 0)
    state = []
    for h in range(2):
        s = jnp.where(causal, lax.dot_general(qh[h], kd, _NT, preferred_element_type=F32),
                      NEG_INF)
        m = jnp.max(s, axis=1, keepdims=True)
        p = jnp.exp2(s - m).astype(BF16)
        state += [m, jnp.dot(p, vaug_ref[h, own, :], preferred_element_type=F32)]

    span = group * MOBA_BLOCK

    def exact_body(g, carry):
        rows = pl.ds(pl.multiple_of(g * span, span), span)
        kg = kaug_ref[:, rows]
        new = []
        for h in range(2):
            m, acc = carry[2 * h], carry[2 * h + 1]
            s = jnp.dot(qa[h], kg, preferred_element_type=F32)
            m_new = jnp.maximum(m, jnp.max(s, axis=1, keepdims=True))
            p = jnp.exp2(s - m_new).astype(BF16)
            acc = jnp.exp2(m - m_new) * acc + jnp.dot(p, vaug_ref[h, rows, :],
                                                     preferred_element_type=F32)
            new += [m_new, acc]
        return tuple(new)

    def lagged_body(g, carry):
        rows = pl.ds(pl.multiple_of(g * span, span), span)
        kg = kaug_ref[:, rows]
        new = []
        for h in range(2):
            m, acc = carry[2 * h], carry[2 * h + 1]
            s = jnp.dot(qa[h], kg, preferred_element_type=F32)
            p = jnp.exp2(s - m).astype(BF16)
            m_new = jnp.maximum(m, jnp.max(s, axis=1, keepdims=True))
            acc = jnp.exp2(m - m_new) * (acc + jnp.dot(p, vaug_ref[h, rows, :],
                                                       preferred_element_type=F32))
            new += [m_new, acc]
        return tuple(new)

    kabs = kabs_ref[0:1, :]
    rise = [jnp.sum(jnp.abs(qh[h].astype(F32)) * kabs, axis=1, keepdims=True) - state[2 * h]
            for h in range(2)]
    lag_ok = jnp.max(jnp.maximum(rise[0], rise[1])) <= MAX_LAG_RISE

    n_groups = lax.div(i + (group - 1), group)
    _, acc_a, _, acc_b = lax.cond(
        lag_ok,
        lambda: lax.fori_loop(0, n_groups, lagged_body, tuple(state)),
        lambda: lax.fori_loop(0, n_groups, exact_body, tuple(state)))
    num = jnp.where(head_a, acc_a, acc_b)
    den = jnp.where(head_a, pltpu.roll(acc_a, ATT_HEAD_DIM, 1), pltpu.roll(acc_b, ATT_HEAD_DIM, 1))
    o = num / den
    sq = o * o
    ms_a = jnp.sum(jnp.where(head_a, sq, 0.0), axis=1, keepdims=True)
    ms_b = jnp.sum(jnp.where(head_a, 0.0, sq), axis=1, keepdims=True)
    mean_sq = jnp.where(head_a, ms_a, ms_b) * (1.0 / ATT_HEAD_DIM)
    o_ref[...] = (o * lax.rsqrt(mean_sq + RMS_EPS) * gain_ref[...]).astype(BF16)


def _moba_attention(q, k, v, kmean, gain, B, S):
    T, W = q.shape
    nq = S // MOBA_BLOCK
    npairs = W // LANES
    group = 4 if nq % 4 == 0 else 1
    return pl.pallas_call(
        functools.partial(_attn_kernel, group=group),
        grid=(B, npairs, nq),
        in_specs=[pl.BlockSpec((MOBA_BLOCK, LANES), lambda b, h, i: (b * nq + i, h)),
                  pl.BlockSpec((S, LANES), lambda b, h, i: (b, h)),
                  pl.BlockSpec((S, LANES), lambda b, h, i: (b, h)),
                  pl.BlockSpec((None, LANES, LANES), lambda b, h, i: (b, 0, h)),
                  pl.BlockSpec((1, LANES), lambda b, h, i: (0, h))],
        out_specs=pl.BlockSpec((MOBA_BLOCK, LANES), lambda b, h, i: (b * nq + i, h)),
        out_shape=jax.ShapeDtypeStruct((T, W), BF16),
        scratch_shapes=[pltpu.VMEM((2 * LANES, S), BF16), pltpu.VMEM((2, S, LANES), BF16),
                        pltpu.VMEM((SUBLANES, LANES), F32)],
        compiler_params=_params("parallel", "parallel", "arbitrary"),
        name="moba_attention",
    )(q, k, v, kmean, gain)


def _mlstm_kernel(mqk_ref, mv_ref, mo_ref, g_ref, cw_ref, cb_ref, gain_ref, o_ref,
                  c_ref, m_ref, tail_ref, kt_ref):
    L = ML_CHUNK

    @pl.when(pl.program_id(1) == 0)
    def _reset():
        c_ref[...] = jnp.zeros_like(c_ref)
        m_ref[...] = jnp.zeros_like(m_ref)
        tail_ref[...] = jnp.zeros_like(tail_ref)

    cur = mqk_ref[...].astype(F32)
    xp = jnp.concatenate([tail_ref[...], cur], axis=0)
    base = SUBLANES - (CONV_WIDTH - 1)
    y = cb_ref[...] + cw_ref[0:1, :] * xp[base:base + L]
    for j in range(1, CONV_WIDTH):
        y = y + cw_ref[j:j + 1, :] * xp[base + j:base + j + L]
    tail_ref[...] = cur[L - SUBLANES:]
    y = y * _sigmoid(y)

    lane = _iota((L, LANES), 1)
    row = _iota((L, LANES), 0)
    tri = lane <= row
    g8 = g_ref[...].T[:SUBLANES]
    grow = _iota((SUBLANES, L), 0)
    logf = -(jnp.maximum(-g8, 0.0) + jnp.log1p(jnp.exp(-jnp.abs(g8))))
    upper = (row <= lane).astype(F32)
    bcum = jnp.dot(jnp.where(grow >= ML_HEADS, logf, 0.0), upper, precision=HIGHEST,
                   preferred_element_type=F32)
    rows8 = jnp.where(grow < ML_HEADS, g8, bcum)
    cols = jnp.concatenate([rows8, jnp.zeros((LANES - SUBLANES, L), F32)], axis=0).T

    for p in range(ML_HEADS // 2):
        kt_ref[p] = y[:, ML_QK_WIDTH + p * LANES:ML_QK_WIDTH + (p + 1) * LANES].T
    ones_col = jnp.where(lane == 0, 1.0, 0.0).astype(BF16)
    for h in range(ML_HEADS):
        pair, odd = divmod(h, 2)
        in_head = (lane >= odd * ML_QK_DIM) & (lane < (odd + 1) * ML_QK_DIM)
        dim_in_head = (row >= odd * ML_QK_DIM) & (row < (odd + 1) * ML_QK_DIM)
        yq = y[:, pair * LANES:(pair + 1) * LANES]
        qh = jnp.where(in_head, yq * (ML_QK_DIM ** -0.5), 0.0).astype(BF16)
        kt = jnp.where(dim_in_head, kt_ref[pair], 0.0).astype(BF16)
        vh = mv_ref[:, h * ML_V_DIM:(h + 1) * ML_V_DIM]
        vaug = jnp.concatenate([vh, ones_col], axis=1)

        i_c = cols[:, h:h + 1]
        b_c = cols[:, ML_HEADS + h:ML_HEADS + h + 1]
        i_r = rows8[h:h + 1, :]
        b_r = rows8[ML_HEADS + h:ML_HEADS + h + 1, :]
        m_prev = m_ref[h:h + 1, 0:1]

        d_log = jnp.where(tri, b_c - b_r + i_r, NEG_INF)
        inter = b_c + m_prev
        m_t = jnp.maximum(inter, jnp.max(d_log, axis=1, keepdims=True))
        w_intra = jnp.exp(d_log - m_t)
        w_inter = jnp.exp(inter - m_t)

        state = c_ref[h]
        s = jnp.dot(qh, kt, preferred_element_type=F32) * w_intra
        numden = (jnp.dot(s.astype(BF16), vaug, preferred_element_type=F32)
                  + w_inter * jnp.dot(qh, state.astype(BF16), preferred_element_type=F32))
        num = numden[:, :ML_V_DIM]
        den = numden[:, ML_V_DIM:ML_V_DIM + 1]
        hcur = num / jnp.maximum(jnp.abs(den), jnp.exp(-m_t))

        b_last = b_c[L - 1:L, :]
        decay = b_last - b_c + i_c
        m_new = jnp.maximum(b_last + m_prev, jnp.max(decay, axis=0, keepdims=True))
        w_state = jnp.exp(decay - m_new)
        carry_scale = jnp.exp(b_last + m_prev - m_new)
        vw = (vaug.astype(F32) * w_state).astype(BF16)
        c_ref[h] = carry_scale * state + jnp.dot(kt, vw, preferred_element_type=F32)
        m_ref[h:h + 1, :] = jnp.broadcast_to(m_new, (1, LANES))

        mean_sq = jnp.mean(hcur * hcur, axis=1, keepdims=True)
        sl = slice(h * ML_V_DIM, (h + 1) * ML_V_DIM)
        gate = _sigmoid(mo_ref[:, sl].astype(F32))
        o_ref[:, sl] = (hcur * lax.rsqrt(mean_sq + RMS_EPS) * gain_ref[:, sl]
                        * gate).astype(BF16)


def _mlstm(mqk, mv, mo, gates, conv_w, conv_b, gain, B, S):
    T, W = mv.shape
    L = ML_CHUNK
    nc = S // L
    row = lambda b, c: (b * nc + c, 0)
    const = lambda b, c: (0, 0)
    return pl.pallas_call(
        _mlstm_kernel,
        grid=(B, nc),
        in_specs=[pl.BlockSpec((L, W), row),
                  pl.BlockSpec((L, W), row),
                  pl.BlockSpec((L, W), row),
                  pl.BlockSpec((L, LANES), row),
                  pl.BlockSpec(conv_w.shape, const),
                  pl.BlockSpec((1, W), const),
                  pl.BlockSpec((1, W), const)],
        out_specs=pl.BlockSpec((L, W), row),
        out_shape=jax.ShapeDtypeStruct((T, W), BF16),
        scratch_shapes=[pltpu.VMEM((ML_HEADS, LANES, 2 * ML_V_DIM), F32),
                        pltpu.VMEM((SUBLANES, LANES), F32),
                        pltpu.VMEM((SUBLANES, W), F32),
                        pltpu.VMEM((ML_HEADS // 2, LANES, L), F32)],
        compiler_params=_params("parallel", "arbitrary"),
        name="mlstm",
    )(mqk, mv, mo, gates, conv_w, conv_b, gain)


def _outproj_kernel(att_ref, ml_ref, wa_ref, wb_ref, x_ref, g_ref, b_ref, o_ref, *, alpha):
    mix = (jnp.dot(att_ref[...], wa_ref[...], preferred_element_type=F32)
           + jnp.dot(ml_ref[...], wb_ref[...], preferred_element_type=F32))
    o_ref[...] = _layer_norm(alpha * x_ref[...] + mix, g_ref[...], b_ref[...])


def _outproj(att, ml, w_out, x2, ln_g, ln_b, alpha):
    T, D = x2.shape
    W = att.shape[1]
    tm = min(T, 512)
    row = lambda i: (i, 0)
    const = lambda i: (0, 0)
    return pl.pallas_call(
        functools.partial(_outproj_kernel, alpha=alpha),
        grid=(T // tm,),
        in_specs=[pl.BlockSpec((tm, W), row),
                  pl.BlockSpec((tm, W), row),
                  pl.BlockSpec((W, D), lambda i: (0, 0)),
                  pl.BlockSpec((W, D), lambda i: (1, 0)),
                  pl.BlockSpec((tm, D), row),
                  pl.BlockSpec((1, D), const),
                  pl.BlockSpec((1, D), const)],
        out_specs=pl.BlockSpec((tm, D), row),
        out_shape=jax.ShapeDtypeStruct((T, D), F32),
        compiler_params=_params("parallel"),
        name="outproj_ln",
    )(att, ml, w_out, w_out, x2, ln_g, ln_b)


def _swiglu_chunk(xb, wg, wu, wd):
    g = jnp.dot(xb, wg, preferred_element_type=F32)
    u = jnp.dot(xb, wu, preferred_element_type=F32)
    h = (g * _sigmoid(g) * u).astype(BF16)
    return jnp.dot(h, wd, preferred_element_type=F32)


def _dense_ffn_kernel(x_ref, wg_ref, wu_ref, wd_ref, g_ref, b_ref, o_ref, *, alpha, chunk):
    x = x_ref[...]
    xb = x.astype(BF16)
    F = wg_ref.shape[1]
    acc = None
    for a in range(0, F, chunk):
        b = min(a + chunk, F)
        part = _swiglu_chunk(xb, wg_ref[:, a:b], wu_ref[:, a:b], wd_ref[a:b, :])
        acc = part if acc is None else acc + part
    o_ref[...] = _layer_norm(alpha * x + acc, g_ref[...], b_ref[...])


def _hidden_chunk(width, target):
    for unit in (2 * LANES, LANES):
        best = 0
        for c in range(unit, target + 1, unit):
            if width % c == 0:
                best = c
        if best:
            return best
    return width


def _dense_ffn(x2, wg, wu, wd, ln_g, ln_b, alpha):
    T, D = x2.shape
    F = wg.shape[1]
    tm = min(T, 512)
    row = lambda i: (i, 0)
    const = lambda i: (0, 0)
    resident = dict(pipeline_mode=pl.Buffered(1))
    return pl.pallas_call(
        functools.partial(_dense_ffn_kernel, alpha=alpha, chunk=4 * LANES),
        grid=(T // tm,),
        in_specs=[pl.BlockSpec((tm, D), row),
                  pl.BlockSpec((D, F), const, **resident),
                  pl.BlockSpec((D, F), const, **resident),
                  pl.BlockSpec((F, D), const, **resident),
                  pl.BlockSpec((1, D), const),
                  pl.BlockSpec((1, D), const)],
        out_specs=pl.BlockSpec((tm, D), row),
        out_shape=jax.ShapeDtypeStruct((T, D), F32),
        compiler_params=_params("parallel"),
        name="dense_ffn_ln",
    )(x2, wg, wu, wd, ln_g, ln_b)


def _router_kernel(x_ref, rw_ref, rb_ref, ids_ref, wts_ref, cnt_ref, carry_ref, *, n_experts):
    @pl.when(pl.program_id(0) == 0)
    def _reset():
        carry_ref[...] = jnp.zeros_like(carry_ref)

    tm = x_ref.shape[0]
    x_hi, x_lo = _split_bf16(x_ref[...], 2)
    w_hi, w_lo = _split_bf16(rw_ref[...], 2)
    logits = (jnp.dot(x_hi, w_hi, preferred_element_type=F32)
              + (jnp.dot(x_lo, w_hi, preferred_element_type=F32)
                 + jnp.dot(x_hi, w_lo, preferred_element_type=F32))) + rb_ref[...]
    lane = _iota((tm, LANES), 1)
    logits = jnp.where(lane < n_experts, logits, NEG_INF)

    def top(lg):
        best = jnp.max(lg, axis=1, keepdims=True)
        first = jnp.min(jnp.where(lg == best, lane, LANES), axis=1, keepdims=True)
        return best, first

    v1, e1 = top(logits)
    hot1 = lane == e1
    v2, e2 = top(jnp.where(hot1, NEG_INF, logits))
    hot2 = lane == e2
    ex = jnp.exp(v2 - v1)
    w1 = 1.0 / (1.0 + ex)
    w2 = ex / (1.0 + ex)

    assigned = (hot1 | hot2).astype(BF16)
    before = (_iota((tm, tm), 1) < _iota((tm, tm), 0)).astype(BF16)
    carry = carry_ref[0:1, :]
    rank = jnp.dot(before, assigned, preferred_element_type=F32) + carry
    r1 = jnp.sum(jnp.where(hot1, rank, 0.0), axis=1, keepdims=True).astype(I32)
    r2 = jnp.sum(jnp.where(hot2, rank, 0.0), axis=1, keepdims=True).astype(I32)
    total = carry + jnp.sum(assigned.astype(F32), axis=0, keepdims=True)
    carry_ref[...] = jnp.broadcast_to(total, carry_ref.shape)
    cnt_ref[...] = jnp.broadcast_to(total, cnt_ref.shape).astype(I32)

    ids_ref[...] = jnp.where(lane == 0, e1, jnp.where(lane == 1, e2,
                             jnp.where(lane == 2, r1, jnp.where(lane == 3, r2, 0))))
    wts_ref[...] = jnp.where(lane == 0, w1, jnp.where(lane == 1, w2, 0.0))


def _router(x2, router_w, router_b):
    T, D = x2.shape
    E = router_w.shape[1]
    tm = min(T, 512)
    rw = jnp.pad(router_w, ((0, 0), (0, LANES - E)))
    rb = jnp.pad(router_b, (0, LANES - E))[None, :]
    row = lambda i: (i, 0)
    const = lambda i: (0, 0)
    return pl.pallas_call(
        functools.partial(_router_kernel, n_experts=E),
        grid=(T // tm,),
        in_specs=[pl.BlockSpec((tm, D), row),
                  pl.BlockSpec((D, LANES), const),
                  pl.BlockSpec((1, LANES), const)],
        out_specs=[pl.BlockSpec((tm, LANES), row),
                   pl.BlockSpec((tm, LANES), row),
                   pl.BlockSpec((SUBLANES, LANES), const)],
        out_shape=[jax.ShapeDtypeStruct((T, LANES), I32),
                   jax.ShapeDtypeStruct((T, LANES), F32),
                   jax.ShapeDtypeStruct((SUBLANES, LANES), I32)],
        scratch_shapes=[pltpu.VMEM((SUBLANES, LANES), F32)],
        compiler_params=_params("arbitrary"),
        name="router",
    )(x2, rw, rb)


def _slots_kernel(ids_ref, offs_ref, out_ref):
    ids = ids_ref[...]
    lane = _iota(ids.shape, 1)
    offs = offs_ref[...].astype(F32)

    def slot(k):
        start = jnp.sum(jnp.where(lane == ids[:, k:k + 1], offs, 0.0), axis=1, keepdims=True)
        return start.astype(I32) + ids[:, TOP_K + k:TOP_K + k + 1]

    out_ref[...] = jnp.where(lane == 0, slot(0), jnp.where(lane == 1, slot(1), 0))


def _row_slots(ids, offs, tm):
    T = ids.shape[0]
    E = offs.shape[0]
    offs_row = jnp.pad(offs, (0, LANES - E))[None, :]
    slots = pl.pallas_call(
        _slots_kernel,
        grid=(T // tm,),
        in_specs=[pl.BlockSpec((tm, LANES), lambda i: (i, 0)),
                  pl.BlockSpec((1, LANES), lambda i: (0, 0))],
        out_specs=pl.BlockSpec((tm, LANES), lambda i: (i, 0)),
        out_shape=jax.ShapeDtypeStruct((T, LANES), I32),
        compiler_params=_params("parallel"),
        name="moe_slots",
    )(ids, offs_row)
    return slots[:, :TOP_K].reshape(T // tm, tm, TOP_K).transpose(0, 2, 1)


def _for_each_row(tm, fn):
    def group(t, c):
        base = pl.multiple_of(t * DMA_UNROLL, DMA_UNROLL)
        for u in range(DMA_UNROLL):
            for k in range(TOP_K):
                fn(base + u, k)
        return c

    lax.fori_loop(0, tm // DMA_UNROLL, group, 0)


def _scatter_kernel(slot_ref, x_ref, xg_in_ref, xg_ref, sem):
    del xg_in_ref
    tm = x_ref.shape[0]

    def copy(r, k):
        return pltpu.make_async_copy(x_ref.at[pl.ds(r, 1)],
                                     xg_ref.at[pl.ds(slot_ref[0, k, r], 1)], sem)

    _for_each_row(tm, lambda r, k: copy(r, k).start())
    _for_each_row(tm, lambda r, k: copy(r, k).wait())


def _scatter_rows(x2, slots, n_rows):
    T, D = x2.shape
    nt, _, tm = slots.shape
    xg0 = jnp.zeros((n_rows, D), F32)
    return pl.pallas_call(
        _scatter_kernel,
        grid=(nt,),
        in_specs=[pl.BlockSpec((1, TOP_K, tm), lambda i: (i, 0, 0), memory_space=pltpu.SMEM),
                  pl.BlockSpec((tm, D), lambda i: (i, 0)),
                  pl.BlockSpec(memory_space=pl.ANY)],
        out_specs=pl.BlockSpec(memory_space=pl.ANY),
        out_shape=jax.ShapeDtypeStruct((n_rows, D), F32),
        scratch_shapes=[pltpu.SemaphoreType.DMA(())],
        input_output_aliases={2: 0},
        compiler_params=_params("arbitrary"),
        name="moe_scatter",
    )(slots, x2, xg0)


def _expert_kernel(te_ref, na_ref, x_ref, wg_ref, wu_ref, wd_ref, o_ref, xb_ref, acc_ref):
    g = pl.program_id(0)
    f = pl.program_id(1)
    active = g < na_ref[0]

    @pl.when(active & (f == 0))
    def _start():
        xb_ref[...] = x_ref[...].astype(BF16)
        acc_ref[...] = jnp.zeros_like(acc_ref)

    @pl.when(active)
    def _accumulate():
        acc_ref[...] += _swiglu_chunk(xb_ref[...], wg_ref[...], wu_ref[...], wd_ref[...])

    last = f == pl.num_programs(1) - 1

    @pl.when(active & last)
    def _finish():
        o_ref[...] = acc_ref[...]

    @pl.when(jnp.logical_not(active) & last)
    def _unused_tile():
        o_ref[...] = jnp.zeros_like(o_ref)


def _expert_ffn(xg, wg, wu, wd, tile_expert, n_active, tm):
    P, D = xg.shape
    E, _, F = wg.shape
    tf = _hidden_chunk(F, 1792)
    nf = F // tf

    def tile(g, te, na):
        return jnp.maximum(jnp.minimum(g, na[0] - 1), 0)

    def chunk(g, f, na):
        return jnp.where(g < na[0], f, nf - 1)

    return pl.pallas_call(
        _expert_kernel,
        grid_spec=pltpu.PrefetchScalarGridSpec(
            num_scalar_prefetch=2,
            grid=(P // tm, nf),
            in_specs=[pl.BlockSpec((tm, D), lambda g, f, te, na: (tile(g, te, na), 0)),
                      pl.BlockSpec((None, D, tf),
                                   lambda g, f, te, na: (te[tile(g, te, na)], 0, chunk(g, f, na))),
                      pl.BlockSpec((None, D, tf),
                                   lambda g, f, te, na: (te[tile(g, te, na)], 0, chunk(g, f, na))),
                      pl.BlockSpec((None, tf, D),
                                   lambda g, f, te, na: (te[tile(g, te, na)], chunk(g, f, na), 0))],
            out_specs=pl.BlockSpec((tm, D), lambda g, f, te, na: (g, 0)),
            scratch_shapes=[pltpu.VMEM((tm, D), BF16), pltpu.VMEM((tm, D), F32)]),
        out_shape=jax.ShapeDtypeStruct((P, D), F32),
        compiler_params=_params("arbitrary", "arbitrary"),
        name="moe_experts",
    )(tile_expert, n_active, xg, wg, wu, wd)


def _combine_kernel(slot_ref, next_ref, yg_ref, wts_ref, x_ref, g_ref, b_ref, o_ref,
                    buf_ref, sem, *, alpha):
    tm = x_ref.shape[0]
    i = pl.program_id(0)
    cur = lax.rem(i, 2)

    def copy(table, buf, r, k):
        return pltpu.make_async_copy(yg_ref.at[pl.ds(table[0, k, r], 1)],
                                     buf_ref.at[buf, k, pl.ds(r, 1)], sem.at[buf])

    @pl.when(i == 0)
    def _first_tile():
        _for_each_row(tm, lambda r, k: copy(slot_ref, 0, r, k).start())

    @pl.when(i + 1 < pl.num_programs(0))
    def _next_tile():
        _for_each_row(tm, lambda r, k: copy(next_ref, 1 - cur, r, k).start())

    _for_each_row(tm, lambda r, k: copy(slot_ref, cur, r, k).wait())
    w = wts_ref[...]
    ffn = w[:, 0:1] * buf_ref[cur, 0] + w[:, 1:2] * buf_ref[cur, 1]
    o_ref[...] = _layer_norm(alpha * x_ref[...] + ffn, g_ref[...], b_ref[...])


def _combine(yg, slots, wts, x2, ln_g, ln_b, alpha):
    T, D = x2.shape
    nt, _, tm = slots.shape
    row = lambda i: (i, 0)
    const = lambda i: (0, 0)
    smem = dict(memory_space=pltpu.SMEM)
    return pl.pallas_call(
        functools.partial(_combine_kernel, alpha=alpha),
        grid=(nt,),
        in_specs=[pl.BlockSpec((1, TOP_K, tm), lambda i: (i, 0, 0), **smem),
                  pl.BlockSpec((1, TOP_K, tm), lambda i: (jnp.minimum(i + 1, nt - 1), 0, 0), **smem),
                  pl.BlockSpec(memory_space=pl.ANY),
                  pl.BlockSpec((tm, LANES), row),
                  pl.BlockSpec((tm, D), row),
                  pl.BlockSpec((1, D), const),
                  pl.BlockSpec((1, D), const)],
        out_specs=pl.BlockSpec((tm, D), row),
        out_shape=jax.ShapeDtypeStruct((T, D), F32),
        scratch_shapes=[pltpu.VMEM((2, TOP_K, tm, D), F32), pltpu.SemaphoreType.DMA((2,))],
        compiler_params=_params("arbitrary"),
        name="moe_combine_ln",
    )(slots, slots, yg, wts, x2, ln_g, ln_b)


def _moe_ffn(x2, router_w, router_b, wg, wu, wd, ln_g, ln_b, alpha):
    T, D = x2.shape
    E = router_w.shape[1]
    tm_e = min(T, 512)
    tm_r = min(T, 256)
    ids, wts, counts = _router(x2, router_w, router_b)

    cnt = counts[0, :E]
    padded = ((cnt + tm_e - 1) // tm_e) * tm_e
    ends = jnp.cumsum(padded)
    offs = (ends - padded).astype(I32)
    n_rows = TOP_K * T + E * tm_e
    n_tiles = n_rows // tm_e
    tile_start = jnp.arange(n_tiles, dtype=I32) * tm_e
    tile_expert = jnp.minimum(jnp.sum(tile_start[:, None] >= ends[None, :], axis=1), E - 1).astype(I32)
    n_active = (ends[-1:] // tm_e).astype(I32)

    slots = _row_slots(ids, offs, tm_r)
    xg = _scatter_rows(x2, slots, n_rows)
    yg = _expert_ffn(xg, wg, wu, wd, tile_expert, n_active, tm_e)
    return _combine(yg, slots, wts, x2, ln_g, ln_b, alpha)


def kernel(x, positions, w_in, gate_bias, conv_w, conv_b, norm_attn, norm_mlstm, w_out,
           ln1_g, ln1_b, dense_w_gate, dense_w_up, dense_w_down, router_w, router_b,
           moe_w_gate, moe_w_up, moe_w_down, ln2_g, ln2_b):
    B, S, D = x.shape
    depth = w_in.shape[0]
    T = B * S
    alpha = (2.0 * depth) ** 0.25
    n_main = 3 * ATT_WIDTH + 2 * ML_QK_WIDTH + 2 * ML_WIDTH
    nblk = S // MOBA_BLOCK

    tabs = _rope_tables(positions)
    x2 = x.reshape(T, D)
    for l in range(depth):
        w_main = w_in[l, :, :n_main].astype(BF16)
        w_gate = jnp.pad(w_in[l, :, n_main:], ((0, 0), (0, LANES - 2 * ML_HEADS))).astype(BF16)
        gbias = jnp.pad(gate_bias[l], (0, LANES - 2 * ML_HEADS))[None, :]
        q, k, v, mqk, mv, mo, gates, kmean = _inproj(x2, w_main, w_gate, gbias, tabs)
        kmean = jnp.pad(kmean.reshape(B, nblk, ATT_WIDTH), ((0, 0), (0, LANES - nblk), (0, 0)))
        att = _moba_attention(q, k, v, kmean, norm_attn[l][None, :], B, S)
        ml = _mlstm(mqk, mv, mo, gates, conv_w[l], conv_b[l][None, :],
                    norm_mlstm[l][None, :], B, S)
        x2 = _outproj(att, ml, w_out[l].astype(BF16), x2, ln1_g[l][None, :], ln1_b[l][None, :],
                      alpha)
        j = l // 2
        if l % 2 == 0:
            x2 = _dense_ffn(x2, dense_w_gate[j].astype(BF16), dense_w_up[j].astype(BF16),
                            dense_w_down[j].astype(BF16), ln2_g[l][None, :], ln2_b[l][None, :],
                            alpha)
        else:
            x2 = _moe_ffn(x2, router_w[j], router_b[j], moe_w_gate[j].astype(BF16),
                          moe_w_up[j].astype(BF16), moe_w_down[j].astype(BF16),
                          ln2_g[l][None, :], ln2_b[l][None, :], alpha)
    return x2.reshape(B, S, D)
```

```python
import functools

import jax
import jax.numpy as jnp
from jax import lax
from jax.experimental import pallas as pl
from jax.experimental.pallas import tpu as pltpu

F32 = jnp.float32
BF16 = jnp.bfloat16
I32 = jnp.int32

ATT_HEADS = 8
ATT_HEAD_DIM = 64
ATT_WIDTH = ATT_HEADS * ATT_HEAD_DIM
MOBA_BLOCK = 256
MOBA_TOPK = 3
ROPE_THETA = 500000.0
ROPE_DIM = ATT_HEAD_DIM // 4
ML_HEADS = 4
ML_V_DIM = 128
ML_QK_DIM = 64
ML_QK_WIDTH = ML_HEADS * ML_QK_DIM
ML_WIDTH = ML_HEADS * ML_V_DIM
ML_CHUNK = 128
CONV_WIDTH = 4
TOP_K = 2
LN_EPS = 1e-5
RMS_EPS = 1e-6

LANES = 128
SUBLANES = 8
V7X_VMEM_BYTES = 64 * 1024 * 1024
VMEM_LIMIT = (V7X_VMEM_BYTES * 3) // 4

MASK_BIAS = -1e9
MAX_LAG_RISE = 60.0
NEG_INF = float("-inf")
HIGHEST = lax.Precision.HIGHEST
LOG2_E = 1.4426950408889634
DMA_UNROLL = 8

_NT = (((1,), (1,)), ((), ()))


def _params(*sem):
    return pltpu.CompilerParams(dimension_semantics=sem, vmem_limit_bytes=VMEM_LIMIT)


def _iota(shape, dim):
    return lax.broadcasted_iota(I32, shape, dim)


def _sigmoid(x):
    return 1.0 / (1.0 + jnp.exp(-x))


def _layer_norm(y, g, b):
    mu = jnp.mean(y, axis=-1, keepdims=True)
    yc = y - mu
    var = jnp.mean(yc * yc, axis=-1, keepdims=True)
    return yc * lax.rsqrt(var + LN_EPS) * g + b


def _rope_tab_kernel(pos_ref, freq_ref, cos_ref, s1_ref, s2_ref):
    ang = pos_ref[...].astype(F32) * freq_ref[...]
    d = _iota(ang.shape, 1) & (ATT_HEAD_DIM - 1)
    half = ROPE_DIM // 2
    s = jnp.sin(ang)
    cos_ref[...] = jnp.cos(ang)
    s1_ref[...] = jnp.where(d < half, -s, 0.0)
    s2_ref[...] = jnp.where((d >= half) & (d < ROPE_DIM), s, 0.0)


def _rope_tables(positions):
    T = positions.size
    tm = min(T, 1024)
    half = ROPE_DIM // 2
    inv_freq = ROPE_THETA ** (-jnp.arange(half, dtype=F32) / half)
    d = jnp.arange(LANES) % ATT_HEAD_DIM
    freq = jnp.where(d < ROPE_DIM, inv_freq[d % half], 0.0).astype(F32)[None, :]
    tab = jax.ShapeDtypeStruct((T, LANES), F32)
    return pl.pallas_call(
        _rope_tab_kernel,
        grid=(T // tm,),
        in_specs=[pl.BlockSpec((tm, 1), lambda i: (i, 0)),
                  pl.BlockSpec((1, LANES), lambda i: (0, 0))],
        out_specs=[pl.BlockSpec((tm, LANES), lambda i: (i, 0))] * 3,
        out_shape=[tab, tab, tab],
        compiler_params=_params("parallel"),
        name="rope_tables",
    )(positions.reshape(T, 1), freq)


def _inproj_kernel(x_ref, w_ref, wg_ref, gb_ref, cos_ref, s1_ref, s2_ref,
                   q_ref, k_ref, v_ref, mqk_ref, mv_ref, mo_ref, g_ref, km_ref):
    xb = x_ref[...].astype(BF16)
    W = ATT_WIDTH

    def proj(c):
        return jnp.dot(xb, w_ref[:, c * W:(c + 1) * W], preferred_element_type=F32)

    rep = W // LANES
    cos = jnp.concatenate([cos_ref[...]] * rep, axis=1)
    s1 = jnp.concatenate([s1_ref[...]] * rep, axis=1)
    s2 = jnp.concatenate([s2_ref[...]] * rep, axis=1)
    half = ROPE_DIM // 2

    def rope(t):
        return t * cos + pltpu.roll(t, W - half, 1) * s1 + pltpu.roll(t, half, 1) * s2

    q_ref[...] = (rope(proj(0)) * (LOG2_E * ATT_HEAD_DIM ** -0.5)).astype(BF16)
    k = rope(proj(1))
    k_ref[...] = k.astype(BF16)
    for g in range(k.shape[0] // MOBA_BLOCK):
        blk = k[g * MOBA_BLOCK:(g + 1) * MOBA_BLOCK]
        km_ref[0, g:g + 1, :] = jnp.sum(blk, axis=0, keepdims=True) * (1.0 / MOBA_BLOCK)
    v_ref[...] = proj(2).astype(BF16)
    mqk_ref[...] = proj(3).astype(BF16)
    mv_ref[...] = proj(4).astype(BF16)
    mo_ref[...] = proj(5).astype(BF16)
    g_ref[...] = jnp.dot(xb, wg_ref[...], preferred_element_type=F32) + gb_ref[...]


def _inproj(x2, w_main, w_gate, gate_bias, tabs):
    T, D = x2.shape
    tm = min(T, 512)
    nkb = tm // MOBA_BLOCK
    W = ATT_WIDTH
    row = lambda i: (i, 0)
    const = lambda i: (0, 0)
    act = jax.ShapeDtypeStruct((T, W), BF16)
    return pl.pallas_call(
        _inproj_kernel,
        grid=(T // tm,),
        in_specs=[pl.BlockSpec((tm, D), row),
                  pl.BlockSpec(w_main.shape, const),
                  pl.BlockSpec(w_gate.shape, const),
                  pl.BlockSpec((1, LANES), const),
                  pl.BlockSpec((tm, LANES), row),
                  pl.BlockSpec((tm, LANES), row),
                  pl.BlockSpec((tm, LANES), row)],
        out_specs=[pl.BlockSpec((tm, W), row)] * 6
        + [pl.BlockSpec((tm, LANES), row),
           pl.BlockSpec((1, nkb, W), lambda i: (i, 0, 0))],
        out_shape=[act] * 6 + [jax.ShapeDtypeStruct((T, LANES), F32),
                               jax.ShapeDtypeStruct((T // tm, nkb, W), F32)],
        compiler_params=_params("parallel"),
        name="inproj",
    )(x2, w_main, w_gate, gate_bias, *tabs)


def _split_bf16(x, parts):
    out = []
    for _ in range(parts):
        hi = x.astype(BF16)
        out.append(hi)
        x = x - hi.astype(F32)
    return out


def _attn_kernel(q_ref, k_ref, v_ref, km_ref, gain_ref, *rest, group, n_cast, n_zero):
    cast_in, rest = rest[:n_cast], rest[n_cast:]
    o_ref, rest = rest[0], rest[1:]
    cast_out, rest = rest[:n_cast], rest[n_cast:]
    zero_out, (kaug_ref, vaug_ref, kabs_ref) = rest[:n_zero], rest[n_zero:]
    for src_ref, dst_ref in zip(cast_in, cast_out):
        dst_ref[...] = src_ref[...].astype(BF16)
    for dst_ref in zero_out:
        dst_ref[...] = jnp.zeros_like(dst_ref)
    i = pl.program_id(2)
    tq = q_ref.shape[0]
    nblk = k_ref.shape[0] // MOBA_BLOCK
    lane = _iota((tq, LANES), 1)
    head_a = lane < ATT_HEAD_DIM

    @pl.when(i == 0)
    def _build_augmented_keys_values():
        blane = _iota((MOBA_BLOCK, LANES), 1)
        brow_k = _iota((LANES, MOBA_BLOCK), 0)
        in_a = blane < ATT_HEAD_DIM
        one = jnp.ones((MOBA_BLOCK, LANES), BF16)

        def fill(j, carry):
            rows = pl.ds(pl.multiple_of(j * MOBA_BLOCK, MOBA_BLOCK), MOBA_BLOCK)
            kaug_ref[:LANES, rows] = k_ref[rows, :].astype(F32).T.astype(BF16)
            kaug_ref[LANES:, rows] = jnp.where(brow_k == j, 1.0, 0.0).astype(BF16)
            vj = v_ref[rows, :]
            vaug_ref[0, rows, :] = jnp.where(in_a, vj, one)
            vaug_ref[1, rows, :] = jnp.where(in_a, one, vj)
            kj = jnp.abs(k_ref[rows, :].astype(F32))
            return jnp.maximum(carry, jnp.max(kj, axis=0, keepdims=True))

        kabs = lax.fori_loop(0, nblk, fill, jnp.zeros((1, LANES), F32))
        kabs_ref[...] = jnp.broadcast_to(kabs, kabs_ref.shape)

    q = q_ref[...]
    zero = jnp.zeros_like(q)
    nrow = -(-nblk // SUBLANES) * SUBLANES
    km_parts = _split_bf16(km_ref[:nrow, :], 3)
    brow = _iota((nrow, tq), 0)

    def select_bias(qh):
        gate = sum(lax.dot_general(part, qh, _NT, preferred_element_type=F32)
                   for part in km_parts)
        gate = jnp.where(brow < i, gate, NEG_INF)
        picked = brow < 0
        for _ in range(MOBA_TOPK):
            best = jnp.max(gate, axis=0, keepdims=True)
            first = jnp.min(jnp.where(gate == best, brow, nrow), axis=0, keepdims=True)
            hit = (brow == first) & (best > NEG_INF)
            picked = picked | hit
            gate = jnp.where(hit, NEG_INF, gate)
        bias = jnp.where(picked, 0.0, MASK_BIAS)
        if nrow < LANES:
            bias = jnp.concatenate([bias, jnp.full((LANES - nrow, tq), MASK_BIAS, F32)], axis=0)
        return bias.T.astype(BF16)

    qh = (jnp.where(head_a, q, zero), jnp.where(head_a, zero, q))
    qa = [jnp.concatenate([x, select_bias(x)], axis=1) for x in qh]

    own = pl.ds(pl.multiple_of(i * MOBA_BLOCK, MOBA_BLOCK), MOBA_BLOCK)
    kd = k_ref[own, :]
    causal = _iota((tq, MOBA_BLOCK), 1) <= _iota((tq, MOBA_BLOCK), 0)
    state = []
    for h in range(2):
        s = jnp.where(causal, lax.dot_general(qh[h], kd, _NT, preferred_element_type=F32),
                      NEG_INF)
        m = jnp.max(s, axis=1, keepdims=True)
        p = jnp.exp2(s - m).astype(BF16)
        state += [m, jnp.dot(p, vaug_ref[h, own, :], preferred_element_type=F32)]

    span = group * MOBA_BLOCK

    def exact_body(g, carry):
        rows = pl.ds(pl.multiple_of(g * span, span), span)
        kg = kaug_ref[:, rows]
        new = []
        for h in range(2):
            m, acc = carry[2 * h], carry[2 * h + 1]
            s = jnp.dot(qa[h], kg, preferred_element_type=F32)
            m_new = jnp.maximum(m, jnp.max(s, axis=1, keepdims=True))
            p = jnp.exp2(s - m_new).astype(BF16)
            acc = jnp.exp2(m - m_new) * acc + jnp.dot(p, vaug_ref[h, rows, :],
                                                     preferred_element_type=F32)
            new += [m_new, acc]
        return tuple(new)

    def lagged_body(g, carry):
        rows = pl.ds(pl.multiple_of(g * span, span), span)
        kg = kaug_ref[:, rows]
        new = []
        for h in range(2):
            m, acc = carry[2 * h], carry[2 * h + 1]
            s = jnp.dot(qa[h], kg, preferred_element_type=F32)
            p = jnp.exp2(s - m).astype(BF16)
            m_new = jnp.maximum(m, jnp.max(s, axis=1, keepdims=True))
            acc = jnp.exp2(m - m_new) * (acc + jnp.dot(p, vaug_ref[h, rows, :],
                                                       preferred_element_type=F32))
            new += [m_new, acc]
        return tuple(new)

    kabs = kabs_ref[0:1, :]
    rise = [jnp.sum(jnp.abs(qh[h].astype(F32)) * kabs, axis=1, keepdims=True) - state[2 * h]
            for h in range(2)]
    lag_ok = jnp.max(jnp.maximum(rise[0], rise[1])) <= MAX_LAG_RISE

    n_groups = lax.div(i + (group - 1), group)
    _, acc_a, _, acc_b = lax.cond(
        lag_ok,
        lambda: lax.fori_loop(0, n_groups, lagged_body, tuple(state)),
        lambda: lax.fori_loop(0, n_groups, exact_body, tuple(state)))
    num = jnp.where(head_a, acc_a, acc_b)
    den = jnp.where(head_a, pltpu.roll(acc_a, ATT_HEAD_DIM, 1), pltpu.roll(acc_b, ATT_HEAD_DIM, 1))
    o = num / den
    sq = o * o
    ms_a = jnp.sum(jnp.where(head_a, sq, 0.0), axis=1, keepdims=True)
    ms_b = jnp.sum(jnp.where(head_a, 0.0, sq), axis=1, keepdims=True)
    mean_sq = jnp.where(head_a, ms_a, ms_b) * (1.0 / ATT_HEAD_DIM)
    o_ref[...] = (o * lax.rsqrt(mean_sq + RMS_EPS) * gain_ref[...]).astype(BF16)


def _moba_attention(q, k, v, kmean, gain, B, S, to_bf16=(), zeros=()):
    T, W = q.shape
    nq = S // MOBA_BLOCK
    npairs = W // LANES
    group = 4 if nq % 4 == 0 else 1
    steps = B * npairs * nq
    step = lambda b, h, i: ((b * npairs + h) * nq + i, 0)
    ride = lambda a: pl.BlockSpec((a.shape[0] // steps, a.shape[1]), step)
    outs = pl.pallas_call(
        functools.partial(_attn_kernel, group=group, n_cast=len(to_bf16), n_zero=len(zeros)),
        grid=(B, npairs, nq),
        in_specs=[pl.BlockSpec((MOBA_BLOCK, LANES), lambda b, h, i: (b * nq + i, h)),
                  pl.BlockSpec((S, LANES), lambda b, h, i: (b, h)),
                  pl.BlockSpec((S, LANES), lambda b, h, i: (b, h)),
                  pl.BlockSpec((None, LANES, LANES), lambda b, h, i: (b, 0, h)),
                  pl.BlockSpec((1, LANES), lambda b, h, i: (0, h))] + [ride(a) for a in to_bf16],
        out_specs=[pl.BlockSpec((MOBA_BLOCK, LANES), lambda b, h, i: (b * nq + i, h))]
        + [ride(a) for a in to_bf16] + [ride(a) for a in zeros],
        out_shape=[jax.ShapeDtypeStruct((T, W), BF16)]
        + [jax.ShapeDtypeStruct(a.shape, BF16) for a in to_bf16] + list(zeros),
        scratch_shapes=[pltpu.VMEM((2 * LANES, S), BF16), pltpu.VMEM((2, S, LANES), BF16),
                        pltpu.VMEM((SUBLANES, LANES), F32)],
        compiler_params=_params("parallel", "parallel", "arbitrary"),
        name="moba_attention",
    )(q, k, v, kmean, gain, *to_bf16)
    n = len(to_bf16)
    return outs[0], outs[1:1 + n], outs[1 + n:]


def _can_ride_attention(a, B, S):
    steps = B * (ATT_WIDTH // LANES) * (S // MOBA_BLOCK)
    return a.shape[0] % (steps * 2 * SUBLANES) == 0


def _mlstm_kernel(mqk_ref, mv_ref, mo_ref, g_ref, cw_ref, cb_ref, gain_ref, o_ref,
                  c_ref, m_ref, tail_ref, kt_ref):
    L = ML_CHUNK
    R = mqk_ref.shape[0]

    @pl.when(pl.program_id(1) == 0)
    def _reset():
        c_ref[...] = jnp.zeros_like(c_ref)
        m_ref[...] = jnp.zeros_like(m_ref)
        tail_ref[...] = jnp.zeros_like(tail_ref)

    cur = mqk_ref[...].astype(F32)
    xp = jnp.concatenate([tail_ref[...], cur], axis=0)
    base = SUBLANES - (CONV_WIDTH - 1)
    y_all = cb_ref[...] + cw_ref[0:1, :] * xp[base:base + R]
    for j in range(1, CONV_WIDTH):
        y_all = y_all + cw_ref[j:j + 1, :] * xp[base + j:base + j + R]
    tail_ref[...] = cur[R - SUBLANES:]
    y_all = y_all * _sigmoid(y_all)
    for c in range(R // L):
        _mlstm_chunk(y_all[c * L:(c + 1) * L], slice(c * L, (c + 1) * L), mv_ref, mo_ref, g_ref,
                     gain_ref, o_ref, c_ref, m_ref, kt_ref)


def _mlstm_chunk(y, rows, mv_ref, mo_ref, g_ref, gain_ref, o_ref, c_ref, m_ref, kt_ref):
    L = ML_CHUNK

    lane = _iota((L, LANES), 1)
    row = _iota((L, LANES), 0)
    tri = lane <= row
    g8 = g_ref[rows, :].T[:SUBLANES]
    grow = _iota((SUBLANES, L), 0)
    logf = -(jnp.maximum(-g8, 0.0) + jnp.log1p(jnp.exp(-jnp.abs(g8))))
    upper = (row <= lane).astype(F32)
    bcum = jnp.dot(jnp.where(grow >= ML_HEADS, logf, 0.0), upper, precision=HIGHEST,
                   preferred_element_type=F32)
    rows8 = jnp.where(grow < ML_HEADS, g8, bcum)
    cols = jnp.concatenate([rows8, jnp.zeros((LANES - SUBLANES, L), F32)], axis=0).T

    for p in range(ML_HEADS // 2):
        kt_ref[p] = y[:, ML_QK_WIDTH + p * LANES:ML_QK_WIDTH + (p + 1) * LANES].T
    ones_col = jnp.where(lane == 0, 1.0, 0.0).astype(BF16)
    for h in range(ML_HEADS):
        pair, odd = divmod(h, 2)
        in_head = (lane >= odd * ML_QK_DIM) & (lane < (odd + 1) * ML_QK_DIM)
        dim_in_head = (row >= odd * ML_QK_DIM) & (row < (odd + 1) * ML_QK_DIM)
        yq = y[:, pair * LANES:(pair + 1) * LANES]
        qh = jnp.where(in_head, yq * (ML_QK_DIM ** -0.5), 0.0).astype(BF16)
        kt = jnp.where(dim_in_head, kt_ref[pair], 0.0).astype(BF16)
        vh = mv_ref[rows, h * ML_V_DIM:(h + 1) * ML_V_DIM]
        vaug = jnp.concatenate([vh, ones_col], axis=1)

        i_c = cols[:, h:h + 1]
        b_c = cols[:, ML_HEADS + h:ML_HEADS + h + 1]
        i_r = rows8[h:h + 1, :]
        b_r = rows8[ML_HEADS + h:ML_HEADS + h + 1, :]
        m_prev = m_ref[h:h + 1, 0:1]

        d_log = jnp.where(tri, b_c - b_r + i_r, NEG_INF)
        inter = b_c + m_prev
        m_t = jnp.maximum(inter, jnp.max(d_log, axis=1, keepdims=True))
        w_intra = jnp.exp(d_log - m_t)
        w_inter = jnp.exp(inter - m_t)

        state = c_ref[h]
        s = jnp.dot(qh, kt, preferred_element_type=F32) * w_intra
        numden = (jnp.dot(s.astype(BF16), vaug, preferred_element_type=F32)
                  + w_inter * jnp.dot(qh, state.astype(BF16), preferred_element_type=F32))
        num = numden[:, :ML_V_DIM]
        den = numden[:, ML_V_DIM:ML_V_DIM + 1]
        hcur = num / jnp.maximum(jnp.abs(den), jnp.exp(-m_t))

        b_last = b_c[L - 1:L, :]
        decay = b_last - b_c + i_c
        m_new = jnp.maximum(b_last + m_prev, jnp.max(decay, axis=0, keepdims=True))
        w_state = jnp.exp(decay - m_new)
        carry_scale = jnp.exp(b_last + m_prev - m_new)
        vw = (vaug.astype(F32) * w_state).astype(BF16)
        c_ref[h] = carry_scale * state + jnp.dot(kt, vw, preferred_element_type=F32)
        m_ref[h:h + 1, :] = jnp.broadcast_to(m_new, (1, LANES))

        mean_sq = jnp.mean(hcur * hcur, axis=1, keepdims=True)
        sl = slice(h * ML_V_DIM, (h + 1) * ML_V_DIM)
        gate = _sigmoid(mo_ref[rows, sl].astype(F32))
        o_ref[rows, sl] = (hcur * lax.rsqrt(mean_sq + RMS_EPS) * gain_ref[:, sl]
                        * gate).astype(BF16)


def _mlstm(mqk, mv, mo, gates, conv_w, conv_b, gain, B, S):
    T, W = mv.shape
    L = ML_CHUNK
    R = (2 if (S // L) % 2 == 0 else 1) * L
    nc = S // R
    row = lambda b, c: (b * nc + c, 0)
    const = lambda b, c: (0, 0)
    return pl.pallas_call(
        _mlstm_kernel,
        grid=(B, nc),
        in_specs=[pl.BlockSpec((R, W), row),
                  pl.BlockSpec((R, W), row),
                  pl.BlockSpec((R, W), row),
                  pl.BlockSpec((R, LANES), row),
                  pl.BlockSpec(conv_w.shape, const),
                  pl.BlockSpec((1, W), const),
                  pl.BlockSpec((1, W), const)],
        out_specs=pl.BlockSpec((R, W), row),
        out_shape=jax.ShapeDtypeStruct((T, W), BF16),
        scratch_shapes=[pltpu.VMEM((ML_HEADS, LANES, 2 * ML_V_DIM), F32),
                        pltpu.VMEM((SUBLANES, LANES), F32),
                        pltpu.VMEM((SUBLANES, W), F32),
                        pltpu.VMEM((ML_HEADS // 2, LANES, L), F32)],
        compiler_params=_params("parallel", "arbitrary"),
        name="mlstm",
    )(mqk, mv, mo, gates, conv_w, conv_b, gain)


def _outproj_kernel(att_ref, ml_ref, wa_ref, wb_ref, x_ref, g_ref, b_ref, o_ref, *, alpha):
    mix = (jnp.dot(att_ref[...], wa_ref[...], preferred_element_type=F32)
           + jnp.dot(ml_ref[...], wb_ref[...], preferred_element_type=F32))
    o_ref[...] = _layer_norm(alpha * x_ref[...] + mix, g_ref[...], b_ref[...])


def _outproj(att, ml, w_out, x2, ln_g, ln_b, alpha):
    T, D = x2.shape
    W = att.shape[1]
    tm = min(T, 512)
    row = lambda i: (i, 0)
    const = lambda i: (0, 0)
    return pl.pallas_call(
        functools.partial(_outproj_kernel, alpha=alpha),
        grid=(T // tm,),
        in_specs=[pl.BlockSpec((tm, W), row),
                  pl.BlockSpec((tm, W), row),
                  pl.BlockSpec((W, D), lambda i: (0, 0)),
                  pl.BlockSpec((W, D), lambda i: (1, 0)),
                  pl.BlockSpec((tm, D), row),
                  pl.BlockSpec((1, D), const),
                  pl.BlockSpec((1, D), const)],
        out_specs=pl.BlockSpec((tm, D), row),
        out_shape=jax.ShapeDtypeStruct((T, D), F32),
        compiler_params=_params("parallel"),
        name="outproj_ln",
    )(att, ml, w_out, w_out, x2, ln_g, ln_b)


def _swiglu_chunk(xb, wg, wu, wd):
    g = jnp.dot(xb, wg, preferred_element_type=F32)
    u = jnp.dot(xb, wu, preferred_element_type=F32)
    h = (g * _sigmoid(g) * u).astype(BF16)
    return jnp.dot(h, wd, preferred_element_type=F32)


def _dense_ffn_kernel(x_ref, wg_ref, wu_ref, wd_ref, g_ref, b_ref, o_ref, *, alpha, chunk):
    x = x_ref[...]
    xb = x.astype(BF16)
    F = wg_ref.shape[1]
    acc = None
    for a in range(0, F, chunk):
        b = min(a + chunk, F)
        part = _swiglu_chunk(xb, wg_ref[:, a:b], wu_ref[:, a:b], wd_ref[a:b, :])
        acc = part if acc is None else acc + part
    o_ref[...] = _layer_norm(alpha * x + acc, g_ref[...], b_ref[...])


def _hidden_chunk(width, target):
    for unit in (2 * LANES, LANES):
        best = 0
        for c in range(unit, target + 1, unit):
            if width % c == 0:
                best = c
        if best:
            return best
    return width


def _dense_ffn(x2, wg, wu, wd, ln_g, ln_b, alpha):
    T, D = x2.shape
    F = wg.shape[1]
    tm = min(T, 512)
    row = lambda i: (i, 0)
    const = lambda i: (0, 0)
    resident = dict(pipeline_mode=pl.Buffered(1))
    return pl.pallas_call(
        functools.partial(_dense_ffn_kernel, alpha=alpha, chunk=4 * LANES),
        grid=(T // tm,),
        in_specs=[pl.BlockSpec((tm, D), row),
                  pl.BlockSpec((D, F), const, **resident),
                  pl.BlockSpec((D, F), const, **resident),
                  pl.BlockSpec((F, D), const, **resident),
                  pl.BlockSpec((1, D), const),
                  pl.BlockSpec((1, D), const)],
        out_specs=pl.BlockSpec((tm, D), row),
        out_shape=jax.ShapeDtypeStruct((T, D), F32),
        compiler_params=_params("parallel"),
        name="dense_ffn_ln",
    )(x2, wg, wu, wd, ln_g, ln_b)


def _router_kernel(x_ref, rw_ref, rb_ref, ids_ref, wts_ref, cnt_ref, carry_ref, *, n_experts):
    @pl.when(pl.program_id(0) == 0)
    def _reset():
        carry_ref[...] = jnp.zeros_like(carry_ref)

    tm = x_ref.shape[0]
    x_hi, x_lo = _split_bf16(x_ref[...], 2)
    w_hi, w_lo = _split_bf16(rw_ref[...], 2)
    logits = (jnp.dot(x_hi, w_hi, preferred_element_type=F32)
              + (jnp.dot(x_lo, w_hi, preferred_element_type=F32)
                 + jnp.dot(x_hi, w_lo, preferred_element_type=F32))) + rb_ref[...]
    lane = _iota((tm, LANES), 1)
    logits = jnp.where(lane < n_experts, logits, NEG_INF)

    def top(lg):
        best = jnp.max(lg, axis=1, keepdims=True)
        first = jnp.min(jnp.where(lg == best, lane, LANES), axis=1, keepdims=True)
        return best, first

    v1, e1 = top(logits)
    hot1 = lane == e1
    v2, e2 = top(jnp.where(hot1, NEG_INF, logits))
    hot2 = lane == e2
    ex = jnp.exp(v2 - v1)
    w1 = 1.0 / (1.0 + ex)
    w2 = ex / (1.0 + ex)

    assigned = (hot1 | hot2).astype(BF16)
    before = (_iota((tm, tm), 1) < _iota((tm, tm), 0)).astype(BF16)
    carry = carry_ref[0:1, :]
    rank = jnp.dot(before, assigned, preferred_element_type=F32) + carry
    r1 = jnp.sum(jnp.where(hot1, rank, 0.0), axis=1, keepdims=True).astype(I32)
    r2 = jnp.sum(jnp.where(hot2, rank, 0.0), axis=1, keepdims=True).astype(I32)
    total = carry + jnp.sum(assigned.astype(F32), axis=0, keepdims=True)
    carry_ref[...] = jnp.broadcast_to(total, carry_ref.shape)
    cnt_ref[...] = jnp.broadcast_to(total, cnt_ref.shape).astype(I32)

    ids_ref[...] = jnp.where(lane == 0, e1, jnp.where(lane == 1, e2,
                             jnp.where(lane == 2, r1, jnp.where(lane == 3, r2, 0))))
    wts_ref[...] = jnp.where(lane == 0, w1, jnp.where(lane == 1, w2, 0.0))


def _router(x2, router_w, router_b):
    T, D = x2.shape
    E = router_w.shape[1]
    tm = min(T, 512)
    rw = jnp.pad(router_w, ((0, 0), (0, LANES - E)))
    rb = jnp.pad(router_b, (0, LANES - E))[None, :]
    row = lambda i: (i, 0)
    const = lambda i: (0, 0)
    return pl.pallas_call(
        functools.partial(_router_kernel, n_experts=E),
        grid=(T // tm,),
        in_specs=[pl.BlockSpec((tm, D), row),
                  pl.BlockSpec((D, LANES), const),
                  pl.BlockSpec((1, LANES), const)],
        out_specs=[pl.BlockSpec((tm, LANES), row),
                   pl.BlockSpec((tm, LANES), row),
                   pl.BlockSpec((SUBLANES, LANES), const)],
        out_shape=[jax.ShapeDtypeStruct((T, LANES), I32),
                   jax.ShapeDtypeStruct((T, LANES), F32),
                   jax.ShapeDtypeStruct((SUBLANES, LANES), I32)],
        scratch_shapes=[pltpu.VMEM((SUBLANES, LANES), F32)],
        compiler_params=_params("arbitrary"),
        name="router",
    )(x2, rw, rb)


def _slots_kernel(ids_ref, offs_ref, out_ref):
    ids = ids_ref[...]
    lane = _iota(ids.shape, 1)
    offs = offs_ref[...].astype(F32)

    def slot(k):
        start = jnp.sum(jnp.where(lane == ids[:, k:k + 1], offs, 0.0), axis=1, keepdims=True)
        return start.astype(I32) + ids[:, TOP_K + k:TOP_K + k + 1]

    out_ref[...] = jnp.where(lane == 0, slot(0), jnp.where(lane == 1, slot(1), 0))


def _row_slots(ids, offs, tm):
    T = ids.shape[0]
    E = offs.shape[0]
    offs_row = jnp.pad(offs, (0, LANES - E))[None, :]
    slots = pl.pallas_call(
        _slots_kernel,
        grid=(T // tm,),
        in_specs=[pl.BlockSpec((tm, LANES), lambda i: (i, 0)),
                  pl.BlockSpec((1, LANES), lambda i: (0, 0))],
        out_specs=pl.BlockSpec((tm, LANES), lambda i: (i, 0)),
        out_shape=jax.ShapeDtypeStruct((T, LANES), I32),
        compiler_params=_params("parallel"),
        name="moe_slots",
    )(ids, offs_row)
    return slots[:, :TOP_K].reshape(T // tm, tm, TOP_K).transpose(0, 2, 1)


def _for_each_row(tm, fn):
    def group(t, c):
        base = pl.multiple_of(t * DMA_UNROLL, DMA_UNROLL)
        for u in range(DMA_UNROLL):
            for k in range(TOP_K):
                fn(base + u, k)
        return c

    lax.fori_loop(0, tm // DMA_UNROLL, group, 0)


def _scatter_kernel(slot_ref, x_ref, xg_in_ref, xg_ref, sem):
    del xg_in_ref
    tm = x_ref.shape[0]

    def copy(r, k):
        return pltpu.make_async_copy(x_ref.at[pl.ds(r, 1)],
                                     xg_ref.at[pl.ds(slot_ref[0, k, r], 1)], sem)

    _for_each_row(tm, lambda r, k: copy(r, k).start(priority=k % 2))
    _for_each_row(tm, lambda r, k: copy(r, k).wait())


def _scatter_rows(x2, slots, xg0):
    T, D = x2.shape
    nt, _, tm = slots.shape
    n_rows = xg0.shape[0]
    return pl.pallas_call(
        _scatter_kernel,
        grid=(nt,),
        in_specs=[pl.BlockSpec((1, TOP_K, tm), lambda i: (i, 0, 0), memory_space=pltpu.SMEM),
                  pl.BlockSpec((tm, D), lambda i: (i, 0)),
                  pl.BlockSpec(memory_space=pl.ANY)],
        out_specs=pl.BlockSpec(memory_space=pl.ANY),
        out_shape=jax.ShapeDtypeStruct((n_rows, D), F32),
        scratch_shapes=[pltpu.SemaphoreType.DMA(())],
        input_output_aliases={2: 0},
        compiler_params=_params("arbitrary"),
        name="moe_scatter",
    )(slots, x2, xg0)


def _expert_kernel(te_ref, na_ref, x_ref, wg_ref, wu_ref, wd_ref, o_ref, xb_ref, acc_ref):
    g = pl.program_id(0)
    f = pl.program_id(1)
    active = g < na_ref[0]

    @pl.when(active & (f == 0))
    def _start():
        xb_ref[...] = x_ref[...].astype(BF16)
        acc_ref[...] = jnp.zeros_like(acc_ref)

    @pl.when(active)
    def _accumulate():
        acc_ref[...] += _swiglu_chunk(xb_ref[...], wg_ref[...], wu_ref[...], wd_ref[...])

    last = f == pl.num_programs(1) - 1

    @pl.when(active & last)
    def _finish():
        o_ref[...] = acc_ref[...]

    @pl.when(jnp.logical_not(active) & last)
    def _unused_tile():
        o_ref[...] = jnp.zeros_like(o_ref)


def _expert_ffn(xg, wg, wu, wd, tile_expert, n_active, tm):
    P, D = xg.shape
    E, _, F = wg.shape
    tf = _hidden_chunk(F, 1792)
    nf = F // tf

    def tile(g, te, na):
        return jnp.maximum(jnp.minimum(g, na[0] - 1), 0)

    def chunk(g, f, na):
        return jnp.where(g < na[0], f, nf - 1)

    return pl.pallas_call(
        _expert_kernel,
        grid_spec=pltpu.PrefetchScalarGridSpec(
            num_scalar_prefetch=2,
            grid=(P // tm, nf),
            in_specs=[pl.BlockSpec((tm, D), lambda g, f, te, na: (tile(g, te, na), 0)),
                      pl.BlockSpec((None, D, tf),
                                   lambda g, f, te, na: (te[tile(g, te, na)], 0, chunk(g, f, na))),
                      pl.BlockSpec((None, D, tf),
                                   lambda g, f, te, na: (te[tile(g, te, na)], 0, chunk(g, f, na))),
                      pl.BlockSpec((None, tf, D),
                                   lambda g, f, te, na: (te[tile(g, te, na)], chunk(g, f, na), 0))],
            out_specs=pl.BlockSpec((tm, D), lambda g, f, te, na: (g, 0)),
            scratch_shapes=[pltpu.VMEM((tm, D), BF16), pltpu.VMEM((tm, D), F32)]),
        out_shape=jax.ShapeDtypeStruct((P, D), F32),
        compiler_params=_params("arbitrary", "arbitrary"),
        name="moe_experts",
    )(tile_expert, n_active, xg, wg, wu, wd)


def _combine_kernel(slot_ref, next_ref, yg_ref, wts_ref, x_ref, g_ref, b_ref, o_ref,
                    buf_ref, sem, *, alpha):
    tm = x_ref.shape[0]
    i = pl.program_id(0)
    cur = lax.rem(i, 2)

    def copy(table, buf, r, k):
        return pltpu.make_async_copy(yg_ref.at[pl.ds(table[0, k, r], 1)],
                                     buf_ref.at[buf, k, pl.ds(r, 1)], sem.at[buf])

    @pl.when(i == 0)
    def _first_tile():
        _for_each_row(tm, lambda r, k: copy(slot_ref, 0, r, k).start(priority=k % 2))

    @pl.when(i + 1 < pl.num_programs(0))
    def _next_tile():
        _for_each_row(tm, lambda r, k: copy(next_ref, 1 - cur, r, k).start(priority=k % 2))

    _for_each_row(tm, lambda r, k: copy(slot_ref, cur, r, k).wait())
    w = wts_ref[...]
    ffn = w[:, 0:1] * buf_ref[cur, 0] + w[:, 1:2] * buf_ref[cur, 1]
    o_ref[...] = _layer_norm(alpha * x_ref[...] + ffn, g_ref[...], b_ref[...])


def _combine(yg, slots, wts, x2, ln_g, ln_b, alpha):
    T, D = x2.shape
    nt, _, tm = slots.shape
    row = lambda i: (i, 0)
    const = lambda i: (0, 0)
    smem = dict(memory_space=pltpu.SMEM)
    return pl.pallas_call(
        functools.partial(_combine_kernel, alpha=alpha),
        grid=(nt,),
        in_specs=[pl.BlockSpec((1, TOP_K, tm), lambda i: (i, 0, 0), **smem),
                  pl.BlockSpec((1, TOP_K, tm), lambda i: (jnp.minimum(i + 1, nt - 1), 0, 0), **smem),
                  pl.BlockSpec(memory_space=pl.ANY),
                  pl.BlockSpec((tm, LANES), row),
                  pl.BlockSpec((tm, D), row),
                  pl.BlockSpec((1, D), const),
                  pl.BlockSpec((1, D), const)],
        out_specs=pl.BlockSpec((tm, D), row),
        out_shape=jax.ShapeDtypeStruct((T, D), F32),
        scratch_shapes=[pltpu.VMEM((2, TOP_K, tm, D), F32), pltpu.SemaphoreType.DMA((2,))],
        compiler_params=_params("arbitrary"),
        name="moe_combine_ln",
    )(slots, slots, yg, wts, x2, ln_g, ln_b)


def _moe_tile(T):
    return min(T, 512)


def _moe_rows(T, E):
    return TOP_K * T + E * _moe_tile(T)


def _moe_ffn(x2, router_w, router_b, wg, wu, wd, ln_g, ln_b, alpha, xg0):
    T, D = x2.shape
    E = router_w.shape[1]
    tm_e = _moe_tile(T)
    tm_r = min(T, 256)
    ids, wts, counts = _router(x2, router_w, router_b)

    cnt = counts[0, :E]
    padded = ((cnt + tm_e - 1) // tm_e) * tm_e
    ends = jnp.cumsum(padded)
    offs = (ends - padded).astype(I32)
    n_rows = xg0.shape[0]
    n_tiles = n_rows // tm_e
    tile_start = jnp.arange(n_tiles, dtype=I32) * tm_e
    tile_expert = jnp.minimum(jnp.sum(tile_start[:, None] >= ends[None, :], axis=1), E - 1).astype(I32)
    n_active = (ends[-1:] // tm_e).astype(I32)

    slots = _row_slots(ids, offs, tm_r)
    xg = _scatter_rows(x2, slots, xg0)
    yg = _expert_ffn(xg, wg, wu, wd, tile_expert, n_active, tm_e)
    return _combine(yg, slots, wts, x2, ln_g, ln_b, alpha)


def kernel(x, positions, w_in, gate_bias, conv_w, conv_b, norm_attn, norm_mlstm, w_out,
           ln1_g, ln1_b, dense_w_gate, dense_w_up, dense_w_down, router_w, router_b,
           moe_w_gate, moe_w_up, moe_w_down, ln2_g, ln2_b):
    B, S, D = x.shape
    depth = w_in.shape[0]
    T = B * S
    alpha = (2.0 * depth) ** 0.25
    n_main = 3 * ATT_WIDTH + 2 * ML_QK_WIDTH + 2 * ML_WIDTH
    nblk = S // MOBA_BLOCK

    tabs = _rope_tables(positions)
    x2 = x.reshape(T, D)
    moe_bf16 = None
    for l in range(depth):
        w_main = w_in[l, :, :n_main].astype(BF16)
        w_gate = jnp.pad(w_in[l, :, n_main:], ((0, 0), (0, LANES - 2 * ML_HEADS))).astype(BF16)
        gbias = jnp.pad(gate_bias[l], (0, LANES - 2 * ML_HEADS))[None, :]
        q, k, v, mqk, mv, mo, gates, kmean = _inproj(x2, w_main, w_gate, gbias, tabs)
        kmean = jnp.pad(kmean.reshape(B, nblk, ATT_WIDTH), ((0, 0), (0, LANES - nblk), (0, 0)))
        ride = []
        if l % 2 == 0 and l + 1 < depth:
            ride = [w[l // 2].reshape(-1, w.shape[-1]) for w in (moe_w_gate, moe_w_up, moe_w_down)]
            if not all(_can_ride_attention(a, B, S) for a in ride):
                ride = []
        clear = []
        if l % 2 == 1:
            clear = [jax.ShapeDtypeStruct((_moe_rows(T, router_w.shape[-1]), D), F32)]
            if not _can_ride_attention(clear[0], B, S):
                clear = []
        att, rode, cleared = _moba_attention(q, k, v, kmean, norm_attn[l][None, :], B, S,
                                             ride, clear)
        if ride:
            moe_bf16 = [a.reshape(w.shape[1:]) for a, w in zip(rode, (moe_w_gate, moe_w_up, moe_w_down))]
        ml = _mlstm(mqk, mv, mo, gates, conv_w[l], conv_b[l][None, :],
                    norm_mlstm[l][None, :], B, S)
        x2 = _outproj(att, ml, w_out[l].astype(BF16), x2, ln1_g[l][None, :], ln1_b[l][None, :],
                      alpha)
        j = l // 2
        if l % 2 == 0:
            x2 = _dense_ffn(x2, dense_w_gate[j].astype(BF16), dense_w_up[j].astype(BF16),
                            dense_w_down[j].astype(BF16), ln2_g[l][None, :], ln2_b[l][None, :],
                            alpha)
        else:
            if moe_bf16 is None:
                moe_bf16 = [w[j].astype(BF16) for w in (moe_w_gate, moe_w_up, moe_w_down)]
            xg0 = cleared[0] if cleared else jnp.zeros((_moe_rows(T, router_w.shape[-1]), D), F32)
            x2 = _moe_ffn(x2, router_w[j], router_b[j], *moe_bf16,
                          ln2_g[l][None, :], ln2_b[l][None, :], alpha, xg0)
            moe_bf16 = None
    return x2.reshape(B, S, D)
```

```python
import functools

import jax
import jax.numpy as jnp
from jax import lax
from jax.experimental import pallas as pl
from jax.experimental.pallas import tpu as pltpu

F32 = jnp.float32
BF16 = jnp.bfloat16
I32 = jnp.int32

ATT_HEADS = 8
ATT_HEAD_DIM = 64
ATT_WIDTH = ATT_HEADS * ATT_HEAD_DIM
MOBA_BLOCK = 256
MOBA_TOPK = 3
ROPE_THETA = 500000.0
ROPE_DIM = ATT_HEAD_DIM // 4
ML_HEADS = 4
ML_V_DIM = 128
ML_QK_DIM = 64
ML_QK_WIDTH = ML_HEADS * ML_QK_DIM
ML_WIDTH = ML_HEADS * ML_V_DIM
ML_CHUNK = 128
CONV_WIDTH = 4
TOP_K = 2
LN_EPS = 1e-5
RMS_EPS = 1e-6

LANES = 128
SUBLANES = 8
V7X_VMEM_BYTES = 64 * 1024 * 1024
VMEM_LIMIT = (V7X_VMEM_BYTES * 3) // 4

MASK_BIAS = -1e9
MAX_LAG_RISE = 60.0
NEG_INF = float("-inf")
HIGHEST = lax.Precision.HIGHEST
LOG2_E = 1.4426950408889634
DMA_UNROLL = 8

_NT = (((1,), (1,)), ((), ()))


def _params(*sem):
    return pltpu.CompilerParams(dimension_semantics=sem, vmem_limit_bytes=VMEM_LIMIT)


def _iota(shape, dim):
    return lax.broadcasted_iota(I32, shape, dim)


def _sigmoid(x):
    return 1.0 / (1.0 + jnp.exp(-x))


def _layer_norm(y, g, b):
    mu = jnp.mean(y, axis=-1, keepdims=True)
    yc = y - mu
    var = jnp.mean(yc * yc, axis=-1, keepdims=True)
    return yc * lax.rsqrt(var + LN_EPS) * g + b


def _rope_tab_kernel(pos_ref, freq_ref, cos_ref, s1_ref, s2_ref):
    ang = pos_ref[...].astype(F32) * freq_ref[...]
    d = _iota(ang.shape, 1) & (ATT_HEAD_DIM - 1)
    half = ROPE_DIM // 2
    s = jnp.sin(ang)
    cos_ref[...] = jnp.cos(ang)
    s1_ref[...] = jnp.where(d < half, -s, 0.0)
    s2_ref[...] = jnp.where((d >= half) & (d < ROPE_DIM), s, 0.0)


def _rope_tables(positions):
    T = positions.size
    tm = min(T, 1024)
    half = ROPE_DIM // 2
    inv_freq = ROPE_THETA ** (-jnp.arange(half, dtype=F32) / half)
    d = jnp.arange(LANES) % ATT_HEAD_DIM
    freq = jnp.where(d < ROPE_DIM, inv_freq[d % half], 0.0).astype(F32)[None, :]
    tab = jax.ShapeDtypeStruct((T, LANES), F32)
    return pl.pallas_call(
        _rope_tab_kernel,
        grid=(T // tm,),
        in_specs=[pl.BlockSpec((tm, 1), lambda i: (i, 0)),
                  pl.BlockSpec((1, LANES), lambda i: (0, 0))],
        out_specs=[pl.BlockSpec((tm, LANES), lambda i: (i, 0))] * 3,
        out_shape=[tab, tab, tab],
        compiler_params=_params("parallel"),
        name="rope_tables",
    )(positions.reshape(T, 1), freq)


def _inproj_kernel(x_ref, w_ref, wg_ref, gb_ref, cos_ref, s1_ref, s2_ref,
                   q_ref, k_ref, v_ref, mqk_ref, mv_ref, mo_ref, g_ref, km_ref):
    xb = x_ref[...].astype(BF16)
    W = ATT_WIDTH

    def proj(c):
        return jnp.dot(xb, w_ref[:, c * W:(c + 1) * W], preferred_element_type=F32)

    rep = W // LANES
    cos = jnp.concatenate([cos_ref[...]] * rep, axis=1)
    s1 = jnp.concatenate([s1_ref[...]] * rep, axis=1)
    s2 = jnp.concatenate([s2_ref[...]] * rep, axis=1)
    half = ROPE_DIM // 2

    def rope(t):
        return t * cos + pltpu.roll(t, W - half, 1) * s1 + pltpu.roll(t, half, 1) * s2

    q_ref[...] = (rope(proj(0)) * (LOG2_E * ATT_HEAD_DIM ** -0.5)).astype(BF16)
    k = rope(proj(1))
    k_ref[...] = k.astype(BF16)
    for g in range(k.shape[0] // MOBA_BLOCK):
        blk = k[g * MOBA_BLOCK:(g + 1) * MOBA_BLOCK]
        km_ref[0, g:g + 1, :] = jnp.sum(blk, axis=0, keepdims=True) * (1.0 / MOBA_BLOCK)
    v_ref[...] = proj(2).astype(BF16)
    mqk_ref[...] = proj(3).astype(BF16)
    mv_ref[...] = proj(4).astype(BF16)
    mo_ref[...] = proj(5).astype(BF16)
    g_ref[...] = jnp.dot(xb, wg_ref[...], preferred_element_type=F32) + gb_ref[...]


def _inproj(x2, w_main, w_gate, gate_bias, tabs):
    T, D = x2.shape
    tm = min(T, 512)
    nkb = tm // MOBA_BLOCK
    W = ATT_WIDTH
    row = lambda i: (i, 0)
    const = lambda i: (0, 0)
    act = jax.ShapeDtypeStruct((T, W), BF16)
    return pl.pallas_call(
        _inproj_kernel,
        grid=(T // tm,),
        in_specs=[pl.BlockSpec((tm, D), row),
                  pl.BlockSpec(w_main.shape, const),
                  pl.BlockSpec(w_gate.shape, const),
                  pl.BlockSpec((1, LANES), const),
                  pl.BlockSpec((tm, LANES), row),
                  pl.BlockSpec((tm, LANES), row),
                  pl.BlockSpec((tm, LANES), row)],
        out_specs=[pl.BlockSpec((tm, W), row)] * 6
        + [pl.BlockSpec((tm, LANES), row),
           pl.BlockSpec((1, nkb, W), lambda i: (i, 0, 0))],
        out_shape=[act] * 6 + [jax.ShapeDtypeStruct((T, LANES), F32),
                               jax.ShapeDtypeStruct((T // tm, nkb, W), F32)],
        compiler_params=_params("parallel"),
        name="inproj",
    )(x2, w_main, w_gate, gate_bias, *tabs)


def _split_bf16(x, parts):
    out = []
    for _ in range(parts):
        hi = x.astype(BF16)
        out.append(hi)
        x = x - hi.astype(F32)
    return out


def _attn_kernel(q_ref, k_ref, v_ref, km_ref, gain_ref, *rest, group, n_cast, n_zero):
    cast_in, rest = rest[:n_cast], rest[n_cast:]
    o_ref, rest = rest[0], rest[1:]
    cast_out, rest = rest[:n_cast], rest[n_cast:]
    zero_out, (kaug_ref, vaug_ref, kabs_ref) = rest[:n_zero], rest[n_zero:]
    for src_ref, dst_ref in zip(cast_in, cast_out):
        dst_ref[...] = src_ref[...].astype(BF16)
    for dst_ref in zero_out:
        dst_ref[...] = jnp.zeros_like(dst_ref)
    i = pl.program_id(2)
    tq = q_ref.shape[0]
    nblk = k_ref.shape[0] // MOBA_BLOCK
    lane = _iota((tq, LANES), 1)
    head_a = lane < ATT_HEAD_DIM

    @pl.when(i == 0)
    def _build_augmented_keys_values():
        blane = _iota((MOBA_BLOCK, LANES), 1)
        brow_k = _iota((LANES, MOBA_BLOCK), 0)
        in_a = blane < ATT_HEAD_DIM
        one = jnp.ones((MOBA_BLOCK, LANES), BF16)

        def fill(j, carry):
            rows = pl.ds(pl.multiple_of(j * MOBA_BLOCK, MOBA_BLOCK), MOBA_BLOCK)
            kaug_ref[:LANES, rows] = k_ref[rows, :].astype(F32).T.astype(BF16)
            kaug_ref[LANES:, rows] = jnp.where(brow_k == j, 1.0, 0.0).astype(BF16)
            vj = v_ref[rows, :]
            vaug_ref[0, rows, :] = jnp.where(in_a, vj, one)
            vaug_ref[1, rows, :] = jnp.where(in_a, one, vj)
            kj = jnp.abs(k_ref[rows, :].astype(F32))
            return jnp.maximum(carry, jnp.max(kj, axis=0, keepdims=True))

        kabs = lax.fori_loop(0, nblk, fill, jnp.zeros((1, LANES), F32))
        kabs_ref[...] = jnp.broadcast_to(kabs, kabs_ref.shape)

    q = q_ref[...]
    zero = jnp.zeros_like(q)
    nrow = -(-nblk // SUBLANES) * SUBLANES
    km_parts = _split_bf16(km_ref[:nrow, :], 3)
    brow = _iota((nrow, tq), 0)

    def select_bias(qh):
        gate = sum(lax.dot_general(part, qh, _NT, preferred_element_type=F32)
                   for part in km_parts)
        gate = jnp.where(brow < i, gate, NEG_INF)
        picked = brow < 0
        for _ in range(MOBA_TOPK):
            best = jnp.max(gate, axis=0, keepdims=True)
            first = jnp.min(jnp.where(gate == best, brow, nrow), axis=0, keepdims=True)
            hit = (brow == first) & (best > NEG_INF)
            picked = picked | hit
            gate = jnp.where(hit, NEG_INF, gate)
        bias = jnp.where(picked, 0.0, MASK_BIAS)
        if nrow < LANES:
            bias = jnp.concatenate([bias, jnp.full((LANES - nrow, tq), MASK_BIAS, F32)], axis=0)
        return bias.T.astype(BF16)

    qh = (jnp.where(head_a, q, zero), jnp.where(head_a, zero, q))
    qa = [jnp.concatenate([x, select_bias(x)], axis=1) for x in qh]

    own = pl.ds(pl.multiple_of(i * MOBA_BLOCK, MOBA_BLOCK), MOBA_BLOCK)
    kd = k_ref[own, :]
    causal = _iota((tq, MOBA_BLOCK), 1) <= _iota((tq, MOBA_BLOCK), 0)
    state = []
    for h in range(2):
        s = jnp.where(causal, lax.dot_general(qh[h], kd, _NT, preferred_element_type=F32),
                      NEG_INF)
        m = jnp.max(s, axis=1, keepdims=True)
        p = jnp.exp2(s - m).astype(BF16)
        state += [m, jnp.dot(p, vaug_ref[h, own, :], preferred_element_type=F32)]

    span = group * MOBA_BLOCK

    def exact_body(g, carry):
        rows = pl.ds(pl.multiple_of(g * span, span), span)
        kg = kaug_ref[:, rows]
        new = []
        for h in range(2):
            m, acc = carry[2 * h], carry[2 * h + 1]
            s = jnp.dot(qa[h], kg, preferred_element_type=F32)
            m_new = jnp.maximum(m, jnp.max(s, axis=1, keepdims=True))
            p = jnp.exp2(s - m_new).astype(BF16)
            acc = jnp.exp2(m - m_new) * acc + jnp.dot(p, vaug_ref[h, rows, :],
                                                     preferred_element_type=F32)
            new += [m_new, acc]
        return tuple(new)

    def lagged_body(g, carry):
        rows = pl.ds(pl.multiple_of(g * span, span), span)
        kg = kaug_ref[:, rows]
        new = []
        for h in range(2):
            m, acc = carry[2 * h], carry[2 * h + 1]
            s = jnp.dot(qa[h], kg, preferred_element_type=F32)
            p = jnp.exp2(s - m).astype(BF16)
            m_new = jnp.maximum(m, jnp.max(s, axis=1, keepdims=True))
            acc = jnp.exp2(m - m_new) * (acc + jnp.dot(p, vaug_ref[h, rows, :],
                                                       preferred_element_type=F32))
            new += [m_new, acc]
        return tuple(new)

    kabs = kabs_ref[0:1, :]
    rise = [jnp.sum(jnp.abs(qh[h].astype(F32)) * kabs, axis=1, keepdims=True) - state[2 * h]
            for h in range(2)]
    lag_ok = jnp.max(jnp.maximum(rise[0], rise[1])) <= MAX_LAG_RISE

    n_groups = lax.div(i + (group - 1), group)
    _, acc_a, _, acc_b = lax.cond(
        lag_ok,
        lambda: lax.fori_loop(0, n_groups, lagged_body, tuple(state)),
        lambda: lax.fori_loop(0, n_groups, exact_body, tuple(state)))
    num = jnp.where(head_a, acc_a, acc_b)
    den = jnp.where(head_a, pltpu.roll(acc_a, ATT_HEAD_DIM, 1), pltpu.roll(acc_b, ATT_HEAD_DIM, 1))
    o = num / den
    sq = o * o
    ms_a = jnp.sum(jnp.where(head_a, sq, 0.0), axis=1, keepdims=True)
    ms_b = jnp.sum(jnp.where(head_a, 0.0, sq), axis=1, keepdims=True)
    mean_sq = jnp.where(head_a, ms_a, ms_b) * (1.0 / ATT_HEAD_DIM)
    o_ref[...] = (o * lax.rsqrt(mean_sq + RMS_EPS) * gain_ref[...]).astype(BF16)


def _moba_attention(q, k, v, kmean, gain, B, S, to_bf16=(), zeros=()):
    T, W = q.shape
    nq = S // MOBA_BLOCK
    npairs = W // LANES
    group = 4 if nq % 4 == 0 else 1
    steps = B * npairs * nq
    step = lambda b, h, i: ((b * npairs + h) * nq + i, 0)
    ride = lambda a: pl.BlockSpec((a.shape[0] // steps, a.shape[1]), step)
    outs = pl.pallas_call(
        functools.partial(_attn_kernel, group=group, n_cast=len(to_bf16), n_zero=len(zeros)),
        grid=(B, npairs, nq),
        in_specs=[pl.BlockSpec((MOBA_BLOCK, LANES), lambda b, h, i: (b * nq + i, h)),
                  pl.BlockSpec((S, LANES), lambda b, h, i: (b, h)),
                  pl.BlockSpec((S, LANES), lambda b, h, i: (b, h)),
                  pl.BlockSpec((None, LANES, LANES), lambda b, h, i: (b, 0, h)),
                  pl.BlockSpec((1, LANES), lambda b, h, i: (0, h))] + [ride(a) for a in to_bf16],
        out_specs=[pl.BlockSpec((MOBA_BLOCK, LANES), lambda b, h, i: (b * nq + i, h))]
        + [ride(a) for a in to_bf16] + [ride(a) for a in zeros],
        out_shape=[jax.ShapeDtypeStruct((T, W), BF16)]
        + [jax.ShapeDtypeStruct(a.shape, BF16) for a in to_bf16] + list(zeros),
        scratch_shapes=[pltpu.VMEM((2 * LANES, S), BF16), pltpu.VMEM((2, S, LANES), BF16),
                        pltpu.VMEM((SUBLANES, LANES), F32)],
        compiler_params=_params("parallel", "parallel", "arbitrary"),
        name="moba_attention",
    )(q, k, v, kmean, gain, *to_bf16)
    n = len(to_bf16)
    return outs[0], outs[1:1 + n], outs[1 + n:]


def _can_ride_attention(a, B, S):
    steps = B * (ATT_WIDTH // LANES) * (S // MOBA_BLOCK)
    return a.shape[0] % (steps * 2 * SUBLANES) == 0


def _mlstm_kernel(mqk_ref, mv_ref, mo_ref, g_ref, cw_ref, cb_ref, gain_ref, o_ref,
                  c_ref, m_ref, tail_ref, kt_ref):
    L = ML_CHUNK
    R = mqk_ref.shape[0]

    @pl.when(pl.program_id(1) == 0)
    def _reset():
        c_ref[...] = jnp.zeros_like(c_ref)
        m_ref[...] = jnp.zeros_like(m_ref)
        tail_ref[...] = jnp.zeros_like(tail_ref)

    cur = mqk_ref[...].astype(F32)
    xp = jnp.concatenate([tail_ref[...], cur], axis=0)
    base = SUBLANES - (CONV_WIDTH - 1)
    y_all = cb_ref[...] + cw_ref[0:1, :] * xp[base:base + R]
    for j in range(1, CONV_WIDTH):
        y_all = y_all + cw_ref[j:j + 1, :] * xp[base + j:base + j + R]
    tail_ref[...] = cur[R - SUBLANES:]
    y_all = y_all * _sigmoid(y_all)
    for c in range(R // L):
        _mlstm_chunk(y_all[c * L:(c + 1) * L], slice(c * L, (c + 1) * L), mv_ref, mo_ref, g_ref,
                     gain_ref, o_ref, c_ref, m_ref, kt_ref)


def _mlstm_chunk(y, rows, mv_ref, mo_ref, g_ref, gain_ref, o_ref, c_ref, m_ref, kt_ref):
    L = ML_CHUNK

    lane = _iota((L, LANES), 1)
    row = _iota((L, LANES), 0)
    tri = lane <= row
    g8 = g_ref[rows, :].T[:SUBLANES]
    grow = _iota((SUBLANES, L), 0)
    logf = -(jnp.maximum(-g8, 0.0) + jnp.log1p(jnp.exp(-jnp.abs(g8))))
    upper = (row <= lane).astype(F32)
    bcum = jnp.dot(jnp.where(grow >= ML_HEADS, logf, 0.0), upper, precision=HIGHEST,
                   preferred_element_type=F32)
    rows8 = jnp.where(grow < ML_HEADS, g8, bcum)
    cols = jnp.concatenate([rows8, jnp.zeros((LANES - SUBLANES, L), F32)], axis=0).T

    for p in range(ML_HEADS // 2):
        kt_ref[p] = y[:, ML_QK_WIDTH + p * LANES:ML_QK_WIDTH + (p + 1) * LANES].T
    ones_col = jnp.where(lane == 0, 1.0, 0.0).astype(BF16)
    for h in range(ML_HEADS):
        pair, odd = divmod(h, 2)
        in_head = (lane >= odd * ML_QK_DIM) & (lane < (odd + 1) * ML_QK_DIM)
        dim_in_head = (row >= odd * ML_QK_DIM) & (row < (odd + 1) * ML_QK_DIM)
        yq = y[:, pair * LANES:(pair + 1) * LANES]
        qh = jnp.where(in_head, yq * (ML_QK_DIM ** -0.5), 0.0).astype(BF16)
        kt = jnp.where(dim_in_head, kt_ref[pair], 0.0).astype(BF16)
        vh = mv_ref[rows, h * ML_V_DIM:(h + 1) * ML_V_DIM]
        vaug = jnp.concatenate([vh, ones_col], axis=1)

        i_c = cols[:, h:h + 1]
        b_c = cols[:, ML_HEADS + h:ML_HEADS + h + 1]
        i_r = rows8[h:h + 1, :]
        b_r = rows8[ML_HEADS + h:ML_HEADS + h + 1, :]
        m_prev = m_ref[h:h + 1, 0:1]

        d_log = jnp.where(tri, b_c - b_r + i_r, NEG_INF)
        inter = b_c + m_prev
        m_t = jnp.maximum(inter, jnp.max(d_log, axis=1, keepdims=True))
        w_intra = jnp.exp(d_log - m_t)
        w_inter = jnp.exp(inter - m_t)

        state = c_ref[h]
        s = jnp.dot(qh, kt, preferred_element_type=F32) * w_intra
        numden = (jnp.dot(s.astype(BF16), vaug, preferred_element_type=F32)
                  + w_inter * jnp.dot(qh, state.astype(BF16), preferred_element_type=F32))
        num = numden[:, :ML_V_DIM]
        den = numden[:, ML_V_DIM:ML_V_DIM + 1]
        hcur = num / jnp.maximum(jnp.abs(den), jnp.exp(-m_t))

        b_last = b_c[L - 1:L, :]
        decay = b_last - b_c + i_c
        m_new = jnp.maximum(b_last + m_prev, jnp.max(decay, axis=0, keepdims=True))
        w_state = jnp.exp(decay - m_new)
        carry_scale = jnp.exp(b_last + m_prev - m_new)
        vw = (vaug.astype(F32) * w_state).astype(BF16)
        c_ref[h] = carry_scale * state + jnp.dot(kt, vw, preferred_element_type=F32)
        m_ref[h:h + 1, :] = jnp.broadcast_to(m_new, (1, LANES))

        mean_sq = jnp.mean(hcur * hcur, axis=1, keepdims=True)
        sl = slice(h * ML_V_DIM, (h + 1) * ML_V_DIM)
        gate = _sigmoid(mo_ref[rows, sl].astype(F32))
        o_ref[rows, sl] = (hcur * lax.rsqrt(mean_sq + RMS_EPS) * gain_ref[:, sl]
                        * gate).astype(BF16)


def _mlstm(mqk, mv, mo, gates, conv_w, conv_b, gain, B, S):
    T, W = mv.shape
    L = ML_CHUNK
    R = L
    nc = S // R
    row = lambda b, c: (b * nc + c, 0)
    const = lambda b, c: (0, 0)
    return pl.pallas_call(
        _mlstm_kernel,
        grid=(B, nc),
        in_specs=[pl.BlockSpec((R, W), row),
                  pl.BlockSpec((R, W), row),
                  pl.BlockSpec((R, W), row),
                  pl.BlockSpec((R, LANES), row),
                  pl.BlockSpec(conv_w.shape, const),
                  pl.BlockSpec((1, W), const),
                  pl.BlockSpec((1, W), const)],
        out_specs=pl.BlockSpec((R, W), row),
        out_shape=jax.ShapeDtypeStruct((T, W), BF16),
        scratch_shapes=[pltpu.VMEM((ML_HEADS, LANES, 2 * ML_V_DIM), F32),
                        pltpu.VMEM((SUBLANES, LANES), F32),
                        pltpu.VMEM((SUBLANES, W), F32),
                        pltpu.VMEM((ML_HEADS // 2, LANES, L), F32)],
        compiler_params=_params("parallel", "arbitrary"),
        name="mlstm",
    )(mqk, mv, mo, gates, conv_w, conv_b, gain)


def _outproj_kernel(att_ref, ml_ref, wa_ref, wb_ref, x_ref, g_ref, b_ref, o_ref, *, alpha):
    mix = (jnp.dot(att_ref[...], wa_ref[...], preferred_element_type=F32)
           + jnp.dot(ml_ref[...], wb_ref[...], preferred_element_type=F32))
    o_ref[...] = _layer_norm(alpha * x_ref[...] + mix, g_ref[...], b_ref[...])


def _outproj(att, ml, w_out, x2, ln_g, ln_b, alpha):
    T, D = x2.shape
    W = att.shape[1]
    tm = min(T, 512)
    row = lambda i: (i, 0)
    const = lambda i: (0, 0)
    return pl.pallas_call(
        functools.partial(_outproj_kernel, alpha=alpha),
        grid=(T // tm,),
        in_specs=[pl.BlockSpec((tm, W), row),
                  pl.BlockSpec((tm, W), row),
                  pl.BlockSpec((W, D), lambda i: (0, 0)),
                  pl.BlockSpec((W, D), lambda i: (1, 0)),
                  pl.BlockSpec((tm, D), row),
                  pl.BlockSpec((1, D), const),
                  pl.BlockSpec((1, D), const)],
        out_specs=pl.BlockSpec((tm, D), row),
        out_shape=jax.ShapeDtypeStruct((T, D), F32),
        compiler_params=_params("parallel"),
        name="outproj_ln",
    )(att, ml, w_out, w_out, x2, ln_g, ln_b)


def _swiglu_chunk(xb, wg, wu, wd):
    g = jnp.dot(xb, wg, preferred_element_type=F32)
    u = jnp.dot(xb, wu, preferred_element_type=F32)
    h = (g * _sigmoid(g) * u).astype(BF16)
    return jnp.dot(h, wd, preferred_element_type=F32)


def _dense_ffn_kernel(x_ref, wg_ref, wu_ref, wd_ref, g_ref, b_ref, o_ref, *, alpha, chunk):
    x = x_ref[...]
    xb = x.astype(BF16)
    F = wg_ref.shape[1]
    acc = None
    for a in range(0, F, chunk):
        b = min(a + chunk, F)
        part = _swiglu_chunk(xb, wg_ref[:, a:b], wu_ref[:, a:b], wd_ref[a:b, :])
        acc = part if acc is None else acc + part
    o_ref[...] = _layer_norm(alpha * x + acc, g_ref[...], b_ref[...])


def _hidden_chunk(width, target):
    for unit in (2 * LANES, LANES):
        best = 0
        for c in range(unit, target + 1, unit):
            if width % c == 0:
                best = c
        if best:
            return best
    return width


def _dense_ffn(x2, wg, wu, wd, ln_g, ln_b, alpha):
    T, D = x2.shape
    F = wg.shape[1]
    tm = min(T, 512)
    row = lambda i: (i, 0)
    const = lambda i: (0, 0)
    resident = dict(pipeline_mode=pl.Buffered(1))
    return pl.pallas_call(
        functools.partial(_dense_ffn_kernel, alpha=alpha, chunk=4 * LANES),
        grid=(T // tm,),
        in_specs=[pl.BlockSpec((tm, D), row),
                  pl.BlockSpec((D, F), const, **resident),
                  pl.BlockSpec((D, F), const, **resident),
                  pl.BlockSpec((F, D), const, **resident),
                  pl.BlockSpec((1, D), const),
                  pl.BlockSpec((1, D), const)],
        out_specs=pl.BlockSpec((tm, D), row),
        out_shape=jax.ShapeDtypeStruct((T, D), F32),
        compiler_params=_params("parallel"),
        name="dense_ffn_ln",
    )(x2, wg, wu, wd, ln_g, ln_b)


def _router_kernel(x_ref, rw_ref, rb_ref, ids_ref, wts_ref, cnt_ref, carry_ref, *, n_experts):
    @pl.when(pl.program_id(0) == 0)
    def _reset():
        carry_ref[...] = jnp.zeros_like(carry_ref)

    tm = x_ref.shape[0]
    x_hi, x_lo = _split_bf16(x_ref[...], 2)
    w_hi, w_lo = _split_bf16(rw_ref[...], 2)
    logits = (jnp.dot(x_hi, w_hi, preferred_element_type=F32)
              + (jnp.dot(x_lo, w_hi, preferred_element_type=F32)
                 + jnp.dot(x_hi, w_lo, preferred_element_type=F32))) + rb_ref[...]
    lane = _iota((tm, LANES), 1)
    logits = jnp.where(lane < n_experts, logits, NEG_INF)

    def top(lg):
        best = jnp.max(lg, axis=1, keepdims=True)
        first = jnp.min(jnp.where(lg == best, lane, LANES), axis=1, keepdims=True)
        return best, first

    v1, e1 = top(logits)
    hot1 = lane == e1
    v2, e2 = top(jnp.where(hot1, NEG_INF, logits))
    hot2 = lane == e2
    ex = jnp.exp(v2 - v1)
    w1 = 1.0 / (1.0 + ex)
    w2 = ex / (1.0 + ex)

    assigned = (hot1 | hot2).astype(BF16)
    before = (_iota((tm, tm), 1) < _iota((tm, tm), 0)).astype(BF16)
    carry = carry_ref[0:1, :]
    rank = jnp.dot(before, assigned, preferred_element_type=F32) + carry
    r1 = jnp.sum(jnp.where(hot1, rank, 0.0), axis=1, keepdims=True).astype(I32)
    r2 = jnp.sum(jnp.where(hot2, rank, 0.0), axis=1, keepdims=True).astype(I32)
    total = carry + jnp.sum(assigned.astype(F32), axis=0, keepdims=True)
    carry_ref[...] = jnp.broadcast_to(total, carry_ref.shape)
    cnt_ref[...] = jnp.broadcast_to(total, cnt_ref.shape).astype(I32)

    ids_ref[...] = jnp.where(lane == 0, e1, jnp.where(lane == 1, e2,
                             jnp.where(lane == 2, r1, jnp.where(lane == 3, r2, 0))))
    wts_ref[...] = jnp.where(lane == 0, w1, jnp.where(lane == 1, w2, 0.0))


def _router(x2, router_w, router_b):
    T, D = x2.shape
    E = router_w.shape[1]
    tm = min(T, 512)
    rw = jnp.pad(router_w, ((0, 0), (0, LANES - E)))
    rb = jnp.pad(router_b, (0, LANES - E))[None, :]
    row = lambda i: (i, 0)
    const = lambda i: (0, 0)
    return pl.pallas_call(
        functools.partial(_router_kernel, n_experts=E),
        grid=(T // tm,),
        in_specs=[pl.BlockSpec((tm, D), row),
                  pl.BlockSpec((D, LANES), const),
                  pl.BlockSpec((1, LANES), const)],
        out_specs=[pl.BlockSpec((tm, LANES), row),
                   pl.BlockSpec((tm, LANES), row),
                   pl.BlockSpec((SUBLANES, LANES), const)],
        out_shape=[jax.ShapeDtypeStruct((T, LANES), I32),
                   jax.ShapeDtypeStruct((T, LANES), F32),
                   jax.ShapeDtypeStruct((SUBLANES, LANES), I32)],
        scratch_shapes=[pltpu.VMEM((SUBLANES, LANES), F32)],
        compiler_params=_params("arbitrary"),
        name="router",
    )(x2, rw, rb)


def _slots_kernel(ids_ref, tiles_ref, out_ref, *, tile_rows):
    ids = ids_ref[...]
    lane = _iota(ids.shape, 1)
    tiles = tiles_ref[...]

    def slot(k):
        pick = jnp.where(lane == ids[:, k:k + 1], 1.0, 0.0).astype(BF16)
        start = jnp.dot(pick, tiles, preferred_element_type=F32)
        return start.astype(I32) * tile_rows + ids[:, TOP_K + k:TOP_K + k + 1]

    out_ref[...] = jnp.where(lane == 0, slot(0), jnp.where(lane == 1, slot(1), 0))


def _row_slots(ids, offs, tm, tile_rows, n_tiles):
    T = ids.shape[0]
    E = offs.shape[0]
    assert n_tiles <= 256, "start tiles must stay exact in bf16"
    start_tile = jnp.pad(offs // tile_rows, (0, LANES - E)).astype(BF16)
    tiles = jnp.broadcast_to(start_tile[:, None], (LANES, LANES))
    slots = pl.pallas_call(
        functools.partial(_slots_kernel, tile_rows=tile_rows),
        grid=(T // tm,),
        in_specs=[pl.BlockSpec((tm, LANES), lambda i: (i, 0)),
                  pl.BlockSpec((LANES, LANES), lambda i: (0, 0))],
        out_specs=pl.BlockSpec((tm, LANES), lambda i: (i, 0)),
        out_shape=jax.ShapeDtypeStruct((T, LANES), I32),
        compiler_params=_params("parallel"),
        name="moe_slots",
    )(ids, tiles)
    return slots[:, :TOP_K].reshape(T // tm, tm, TOP_K).transpose(0, 2, 1)


def _for_each_row(tm, fn):
    def group(t, c):
        base = pl.multiple_of(t * DMA_UNROLL, DMA_UNROLL)
        for u in range(DMA_UNROLL):
            for k in range(TOP_K):
                fn(base + u, k)
        return c

    lax.fori_loop(0, tm // DMA_UNROLL, group, 0)


def _scatter_kernel(slot_ref, x_ref, xg_in_ref, xg_ref, sem):
    del xg_in_ref
    tm = x_ref.shape[0]

    def copy(r, k):
        return pltpu.make_async_copy(x_ref.at[pl.ds(r, 1)],
                                     xg_ref.at[pl.ds(slot_ref[0, k, r], 1)], sem)

    _for_each_row(tm, lambda r, k: copy(r, k).start())
    _for_each_row(tm, lambda r, k: copy(r, k).wait())


def _scatter_rows(x2, slots, xg0):
    T, D = x2.shape
    nt, _, tm = slots.shape
    n_rows = xg0.shape[0]
    return pl.pallas_call(
        _scatter_kernel,
        grid=(nt,),
        in_specs=[pl.BlockSpec((1, TOP_K, tm), lambda i: (i, 0, 0), memory_space=pltpu.SMEM),
                  pl.BlockSpec((tm, D), lambda i: (i, 0)),
                  pl.BlockSpec(memory_space=pl.ANY)],
        out_specs=pl.BlockSpec(memory_space=pl.ANY),
        out_shape=jax.ShapeDtypeStruct((n_rows, D), F32),
        scratch_shapes=[pltpu.SemaphoreType.DMA(())],
        input_output_aliases={2: 0},
        compiler_params=_params("arbitrary"),
        name="moe_scatter",
    )(slots, x2, xg0)


def _expert_kernel(te_ref, na_ref, x_ref, wg_ref, wu_ref, wd_ref, o_ref, xb_ref, acc_ref):
    g = pl.program_id(0)
    f = pl.program_id(1)
    active = g < na_ref[0]

    @pl.when(active & (f == 0))
    def _start():
        xb_ref[...] = x_ref[...].astype(BF16)
        acc_ref[...] = jnp.zeros_like(acc_ref)

    @pl.when(active)
    def _accumulate():
        acc_ref[...] += _swiglu_chunk(xb_ref[...], wg_ref[...], wu_ref[...], wd_ref[...])

    last = f == pl.num_programs(1) - 1

    @pl.when(active & last)
    def _finish():
        o_ref[...] = acc_ref[...]

    @pl.when(jnp.logical_not(active) & last)
    def _unused_tile():
        o_ref[...] = jnp.zeros_like(o_ref)


def _expert_ffn(xg, wg, wu, wd, tile_expert, n_active, tm):
    P, D = xg.shape
    E, _, F = wg.shape
    tf = _hidden_chunk(F, 1792)
    nf = F // tf

    def tile(g, te, na):
        return jnp.maximum(jnp.minimum(g, na[0] - 1), 0)

    def chunk(g, f, na):
        return jnp.where(g < na[0], f, nf - 1)

    return pl.pallas_call(
        _expert_kernel,
        grid_spec=pltpu.PrefetchScalarGridSpec(
            num_scalar_prefetch=2,
            grid=(P // tm, nf),
            in_specs=[pl.BlockSpec((tm, D), lambda g, f, te, na: (tile(g, te, na), 0)),
                      pl.BlockSpec((None, D, tf),
                                   lambda g, f, te, na: (te[tile(g, te, na)], 0, chunk(g, f, na))),
                      pl.BlockSpec((None, D, tf),
                                   lambda g, f, te, na: (te[tile(g, te, na)], 0, chunk(g, f, na))),
                      pl.BlockSpec((None, tf, D),
                                   lambda g, f, te, na: (te[tile(g, te, na)], chunk(g, f, na), 0))],
            out_specs=pl.BlockSpec((tm, D), lambda g, f, te, na: (g, 0)),
            scratch_shapes=[pltpu.VMEM((tm, D), BF16), pltpu.VMEM((tm, D), F32)]),
        out_shape=jax.ShapeDtypeStruct((P, D), F32),
        compiler_params=_params("arbitrary", "arbitrary"),
        name="moe_experts",
    )(tile_expert, n_active, xg, wg, wu, wd)


def _combine_kernel(slot_ref, next_ref, yg_ref, wts_ref, x_ref, g_ref, b_ref, o_ref,
                    buf_ref, sem, *, alpha):
    tm = x_ref.shape[0]
    i = pl.program_id(0)
    cur = lax.rem(i, 2)

    def copy(table, buf, r, k):
        return pltpu.make_async_copy(yg_ref.at[pl.ds(table[0, k, r], 1)],
                                     buf_ref.at[buf, k, pl.ds(r, 1)], sem.at[buf])

    @pl.when(i == 0)
    def _first_tile():
        _for_each_row(tm, lambda r, k: copy(slot_ref, 0, r, k).start())

    @pl.when(i + 1 < pl.num_programs(0))
    def _next_tile():
        _for_each_row(tm, lambda r, k: copy(next_ref, 1 - cur, r, k).start())

    _for_each_row(tm, lambda r, k: copy(slot_ref, cur, r, k).wait())
    w = wts_ref[...]
    ffn = w[:, 0:1] * buf_ref[cur, 0] + w[:, 1:2] * buf_ref[cur, 1]
    o_ref[...] = _layer_norm(alpha * x_ref[...] + ffn, g_ref[...], b_ref[...])


def _combine(yg, slots, wts, x2, ln_g, ln_b, alpha):
    T, D = x2.shape
    nt, _, tm = slots.shape
    row = lambda i: (i, 0)
    const = lambda i: (0, 0)
    smem = dict(memory_space=pltpu.SMEM)
    return pl.pallas_call(
        functools.partial(_combine_kernel, alpha=alpha),
        grid=(nt,),
        in_specs=[pl.BlockSpec((1, TOP_K, tm), lambda i: (i, 0, 0), **smem),
                  pl.BlockSpec((1, TOP_K, tm), lambda i: (jnp.minimum(i + 1, nt - 1), 0, 0), **smem),
                  pl.BlockSpec(memory_space=pl.ANY),
                  pl.BlockSpec((tm, LANES), row),
                  pl.BlockSpec((tm, D), row),
                  pl.BlockSpec((1, D), const),
                  pl.BlockSpec((1, D), const)],
        out_specs=pl.BlockSpec((tm, D), row),
        out_shape=jax.ShapeDtypeStruct((T, D), F32),
        scratch_shapes=[pltpu.VMEM((2, TOP_K, tm, D), F32), pltpu.SemaphoreType.DMA((2,))],
        compiler_params=_params("arbitrary"),
        name="moe_combine_ln",
    )(slots, slots, yg, wts, x2, ln_g, ln_b)


def _moe_tile(T):
    return min(T, 512)


def _moe_rows(T, E):
    return TOP_K * T + E * _moe_tile(T)


def _moe_ffn(x2, router_w, router_b, wg, wu, wd, ln_g, ln_b, alpha, xg0):
    T, D = x2.shape
    E = router_w.shape[1]
    tm_e = _moe_tile(T)
    tm_r = min(T, 256)
    ids, wts, counts = _router(x2, router_w, router_b)

    cnt = counts[0, :E]
    padded = ((cnt + tm_e - 1) // tm_e) * tm_e
    ends = jnp.cumsum(padded)
    offs = (ends - padded).astype(I32)
    n_rows = xg0.shape[0]
    n_tiles = n_rows // tm_e
    tile_start = jnp.arange(n_tiles, dtype=I32) * tm_e
    tile_expert = jnp.minimum(jnp.sum(tile_start[:, None] >= ends[None, :], axis=1), E - 1).astype(I32)
    n_active = (ends[-1:] // tm_e).astype(I32)

    slots = _row_slots(ids, offs, tm_r, tm_e, n_tiles)
    xg = _scatter_rows(x2, slots, xg0)
    yg = _expert_ffn(xg, wg, wu, wd, tile_expert, n_active, tm_e)
    return _combine(yg, slots, wts, x2, ln_g, ln_b, alpha)


def kernel(x, positions, w_in, gate_bias, conv_w, conv_b, norm_attn, norm_mlstm, w_out,
           ln1_g, ln1_b, dense_w_gate, dense_w_up, dense_w_down, router_w, router_b,
           moe_w_gate, moe_w_up, moe_w_down, ln2_g, ln2_b):
    B, S, D = x.shape
    depth = w_in.shape[0]
    T = B * S
    alpha = (2.0 * depth) ** 0.25
    n_main = 3 * ATT_WIDTH + 2 * ML_QK_WIDTH + 2 * ML_WIDTH
    nblk = S // MOBA_BLOCK

    tabs = _rope_tables(positions)
    x2 = x.reshape(T, D)
    moe_bf16 = None
    for l in range(depth):
        w_main = w_in[l, :, :n_main].astype(BF16)
        w_gate = jnp.pad(w_in[l, :, n_main:], ((0, 0), (0, LANES - 2 * ML_HEADS))).astype(BF16)
        gbias = jnp.pad(gate_bias[l], (0, LANES - 2 * ML_HEADS))[None, :]
        q, k, v, mqk, mv, mo, gates, kmean = _inproj(x2, w_main, w_gate, gbias, tabs)
        kmean = jnp.pad(kmean.reshape(B, nblk, ATT_WIDTH), ((0, 0), (0, LANES - nblk), (0, 0)))
        ride = []
        if l % 2 == 0 and l + 1 < depth:
            ride = [w[l // 2].reshape(-1, w.shape[-1]) for w in (moe_w_gate, moe_w_up, moe_w_down)]
            if not all(_can_ride_attention(a, B, S) for a in ride):
                ride = []
        clear = []
        if l % 2 == 1:
            clear = [jax.ShapeDtypeStruct((_moe_rows(T, router_w.shape[-1]), D), F32)]
            if not _can_ride_attention(clear[0], B, S):
                clear = []
        att, rode, cleared = _moba_attention(q, k, v, kmean, norm_attn[l][None, :], B, S,
                                             ride, clear)
        if ride:
            moe_bf16 = [a.reshape(w.shape[1:]) for a, w in zip(rode, (moe_w_gate, moe_w_up, moe_w_down))]
        ml = _mlstm(mqk, mv, mo, gates, conv_w[l], conv_b[l][None, :],
                    norm_mlstm[l][None, :], B, S)
        x2 = _outproj(att, ml, w_out[l].astype(BF16), x2, ln1_g[l][None, :], ln1_b[l][None, :],
                      alpha)
        j = l // 2
        if l % 2 == 0:
            x2 = _dense_ffn(x2, dense_w_gate[j].astype(BF16), dense_w_up[j].astype(BF16),
                            dense_w_down[j].astype(BF16), ln2_g[l][None, :], ln2_b[l][None, :],
                            alpha)
        else:
            if moe_bf16 is None:
                moe_bf16 = [w[j].astype(BF16) for w in (moe_w_gate, moe_w_up, moe_w_down)]
            xg0 = cleared[0] if cleared else jnp.zeros((_moe_rows(T, router_w.shape[-1]), D), F32)
            x2 = _moe_ffn(x2, router_w[j], router_b[j], *moe_bf16,
                          ln2_g[l][None, :], ln2_b[l][None, :], alpha, xg0)
            moe_bf16 = None
    return x2.reshape(B, S, D)
```

```python
import functools

import jax
import jax.numpy as jnp
from jax import lax
from jax.experimental import pallas as pl
from jax.experimental.pallas import tpu as pltpu

F32 = jnp.float32
BF16 = jnp.bfloat16
I32 = jnp.int32

ATT_HEADS = 8
ATT_HEAD_DIM = 64
ATT_WIDTH = ATT_HEADS * ATT_HEAD_DIM
MOBA_BLOCK = 256
MOBA_TOPK = 3
ROPE_THETA = 500000.0
ROPE_DIM = ATT_HEAD_DIM // 4
ML_HEADS = 4
ML_V_DIM = 128
ML_QK_DIM = 64
ML_QK_WIDTH = ML_HEADS * ML_QK_DIM
ML_WIDTH = ML_HEADS * ML_V_DIM
ML_CHUNK = 128
CONV_WIDTH = 4
TOP_K = 2
LN_EPS = 1e-5
RMS_EPS = 1e-6

LANES = 128
SUBLANES = 8
V7X_VMEM_BYTES = 64 * 1024 * 1024
VMEM_LIMIT = (V7X_VMEM_BYTES * 3) // 4

MASK_BIAS = -1e9
MAX_LAG_RISE = 60.0
NEG_INF = float("-inf")
HIGHEST = lax.Precision.HIGHEST
LOG2_E = 1.4426950408889634
DMA_UNROLL = 8

_NT = (((1,), (1,)), ((), ()))


def _params(*sem):
    return pltpu.CompilerParams(dimension_semantics=sem, vmem_limit_bytes=VMEM_LIMIT)


def _iota(shape, dim):
    return lax.broadcasted_iota(I32, shape, dim)


def _sigmoid(x):
    return 1.0 / (1.0 + jnp.exp(-x))


def _layer_norm(y, g, b):
    mu = jnp.mean(y, axis=-1, keepdims=True)
    yc = y - mu
    var = jnp.mean(yc * yc, axis=-1, keepdims=True)
    return yc * lax.rsqrt(var + LN_EPS) * g + b


def _rope_tab_kernel(pos_ref, freq_ref, cos_ref, s1_ref, s2_ref):
    ang = pos_ref[...].astype(F32) * freq_ref[...]
    d = _iota(ang.shape, 1) & (ATT_HEAD_DIM - 1)
    half = ROPE_DIM // 2
    s = jnp.sin(ang)
    cos_ref[...] = jnp.cos(ang)
    s1_ref[...] = jnp.where(d < half, -s, 0.0)
    s2_ref[...] = jnp.where((d >= half) & (d < ROPE_DIM), s, 0.0)


def _rope_tables(positions):
    T = positions.size
    tm = min(T, 1024)
    half = ROPE_DIM // 2
    inv_freq = ROPE_THETA ** (-jnp.arange(half, dtype=F32) / half)
    d = jnp.arange(LANES) % ATT_HEAD_DIM
    freq = jnp.where(d < ROPE_DIM, inv_freq[d % half], 0.0).astype(F32)[None, :]
    tab = jax.ShapeDtypeStruct((T, LANES), F32)
    return pl.pallas_call(
        _rope_tab_kernel,
        grid=(T // tm,),
        in_specs=[pl.BlockSpec((tm, 1), lambda i: (i, 0)),
                  pl.BlockSpec((1, LANES), lambda i: (0, 0))],
        out_specs=[pl.BlockSpec((tm, LANES), lambda i: (i, 0))] * 3,
        out_shape=[tab, tab, tab],
        compiler_params=_params("parallel"),
        name="rope_tables",
    )(positions.reshape(T, 1), freq)


def _inproj_kernel(x_ref, w_ref, wg_ref, gb_ref, cos_ref, s1_ref, s2_ref,
                   q_ref, k_ref, v_ref, mqk_ref, mv_ref, mo_ref, g_ref, km_ref):
    xb = x_ref[...].astype(BF16)
    W = ATT_WIDTH

    def proj(c):
        return jnp.dot(xb, w_ref[:, c * W:(c + 1) * W], preferred_element_type=F32)

    rep = W // LANES
    cos = jnp.concatenate([cos_ref[...]] * rep, axis=1)
    s1 = jnp.concatenate([s1_ref[...]] * rep, axis=1)
    s2 = jnp.concatenate([s2_ref[...]] * rep, axis=1)
    half = ROPE_DIM // 2

    def rope(t):
        return t * cos + pltpu.roll(t, W - half, 1) * s1 + pltpu.roll(t, half, 1) * s2

    q_ref[...] = (rope(proj(0)) * (LOG2_E * ATT_HEAD_DIM ** -0.5)).astype(BF16)
    k = rope(proj(1))
    k_ref[...] = k.astype(BF16)
    for g in range(k.shape[0] // MOBA_BLOCK):
        blk = k[g * MOBA_BLOCK:(g + 1) * MOBA_BLOCK]
        km_ref[0, g:g + 1, :] = jnp.sum(blk, axis=0, keepdims=True) * (1.0 / MOBA_BLOCK)
    v_ref[...] = proj(2).astype(BF16)
    mqk_ref[...] = proj(3).astype(BF16)
    mv_ref[...] = proj(4).astype(BF16)
    mo_ref[...] = proj(5).astype(BF16)
    g_ref[...] = jnp.dot(xb, wg_ref[...], preferred_element_type=F32) + gb_ref[...]


def _inproj(x2, w_main, w_gate, gate_bias, tabs):
    T, D = x2.shape
    tm = min(T, 512)
    nkb = tm // MOBA_BLOCK
    W = ATT_WIDTH
    row = lambda i: (i, 0)
    const = lambda i: (0, 0)
    act = jax.ShapeDtypeStruct((T, W), BF16)
    return pl.pallas_call(
        _inproj_kernel,
        grid=(T // tm,),
        in_specs=[pl.BlockSpec((tm, D), row),
                  pl.BlockSpec(w_main.shape, const),
                  pl.BlockSpec(w_gate.shape, const),
                  pl.BlockSpec((1, LANES), const),
                  pl.BlockSpec((tm, LANES), row),
                  pl.BlockSpec((tm, LANES), row),
                  pl.BlockSpec((tm, LANES), row)],
        out_specs=[pl.BlockSpec((tm, W), row)] * 6
        + [pl.BlockSpec((tm, LANES), row),
           pl.BlockSpec((1, nkb, W), lambda i: (i, 0, 0))],
        out_shape=[act] * 6 + [jax.ShapeDtypeStruct((T, LANES), F32),
                               jax.ShapeDtypeStruct((T // tm, nkb, W), F32)],
        compiler_params=_params("parallel"),
        name="inproj",
    )(x2, w_main, w_gate, gate_bias, *tabs)


def _split_bf16(x, parts):
    out = []
    for _ in range(parts):
        hi = x.astype(BF16)
        out.append(hi)
        x = x - hi.astype(F32)
    return out


def _attn_kernel(q_ref, k_ref, v_ref, km_ref, gain_ref, *rest, group, n_cast, n_zero):
    cast_in, rest = rest[:n_cast], rest[n_cast:]
    o_ref, rest = rest[0], rest[1:]
    cast_out, rest = rest[:n_cast], rest[n_cast:]
    zero_out, (kaug_ref, vaug_ref, kabs_ref) = rest[:n_zero], rest[n_zero:]
    for src_ref, dst_ref in zip(cast_in, cast_out):
        dst_ref[...] = src_ref[...].astype(BF16)
    for dst_ref in zero_out:
        dst_ref[...] = jnp.zeros_like(dst_ref)
    i = pl.program_id(2)
    tq = q_ref.shape[0]
    nblk = k_ref.shape[0] // MOBA_BLOCK
    lane = _iota((tq, LANES), 1)
    head_a = lane < ATT_HEAD_DIM

    @pl.when(i == 0)
    def _build_augmented_keys_values():
        blane = _iota((MOBA_BLOCK, LANES), 1)
        brow_k = _iota((LANES, MOBA_BLOCK), 0)
        in_a = blane < ATT_HEAD_DIM
        one = jnp.ones((MOBA_BLOCK, LANES), BF16)

        def fill(j, carry):
            rows = pl.ds(pl.multiple_of(j * MOBA_BLOCK, MOBA_BLOCK), MOBA_BLOCK)
            kaug_ref[:LANES, rows] = k_ref[rows, :].astype(F32).T.astype(BF16)
            kaug_ref[LANES:, rows] = jnp.where(brow_k == j, 1.0, 0.0).astype(BF16)
            vj = v_ref[rows, :]
            vaug_ref[0, rows, :] = jnp.where(in_a, vj, one)
            vaug_ref[1, rows, :] = jnp.where(in_a, one, vj)
            kj = jnp.abs(k_ref[rows, :].astype(F32))
            return jnp.maximum(carry, jnp.max(kj, axis=0, keepdims=True))

        kabs = lax.fori_loop(0, nblk, fill, jnp.zeros((1, LANES), F32))
        kabs_ref[...] = jnp.broadcast_to(kabs, kabs_ref.shape)

    q = q_ref[...]
    zero = jnp.zeros_like(q)
    nrow = -(-nblk // SUBLANES) * SUBLANES
    km_stack = jnp.concatenate(_split_bf16(km_ref[:nrow, :], 3), axis=0)
    brow = _iota((nrow, tq), 0)

    def select_bias(qh):
        parts = lax.dot_general(km_stack, qh, _NT, preferred_element_type=F32)
        gate = parts[:nrow] + parts[nrow:2 * nrow] + parts[2 * nrow:]
        gate = jnp.where(brow < i, gate, NEG_INF)
        picked = brow < 0
        for _ in range(MOBA_TOPK):
            best = jnp.max(gate, axis=0, keepdims=True)
            first = jnp.min(jnp.where(gate == best, brow, nrow), axis=0, keepdims=True)
            hit = (brow == first) & (best > NEG_INF)
            picked = picked | hit
            gate = jnp.where(hit, NEG_INF, gate)
        bias = jnp.where(picked, 0.0, MASK_BIAS)
        if nrow < LANES:
            bias = jnp.concatenate([bias, jnp.full((LANES - nrow, tq), MASK_BIAS, F32)], axis=0)
        return bias.T.astype(BF16)

    qh = (jnp.where(head_a, q, zero), jnp.where(head_a, zero, q))
    qa = [jnp.concatenate([x, select_bias(x)], axis=1) for x in qh]

    own = pl.ds(pl.multiple_of(i * MOBA_BLOCK, MOBA_BLOCK), MOBA_BLOCK)
    kd = kaug_ref[:LANES, own]
    causal = _iota((tq, MOBA_BLOCK), 1) <= _iota((tq, MOBA_BLOCK), 0)
    state = []
    for h in range(2):
        s = jnp.where(causal, jnp.dot(qh[h], kd, preferred_element_type=F32), NEG_INF)
        m = jnp.max(s, axis=1, keepdims=True)
        p = jnp.exp2(s - m).astype(BF16)
        state += [m, jnp.dot(p, vaug_ref[h, own, :], preferred_element_type=F32)]

    span = group * MOBA_BLOCK

    def exact_body(g, carry):
        rows = pl.ds(pl.multiple_of(g * span, span), span)
        kg = kaug_ref[:, rows]
        new = []
        for h in range(2):
            m, acc = carry[2 * h], carry[2 * h + 1]
            s = jnp.dot(qa[h], kg, preferred_element_type=F32)
            m_new = jnp.maximum(m, jnp.max(s, axis=1, keepdims=True))
            p = jnp.exp2(s - m_new).astype(BF16)
            acc = jnp.exp2(m - m_new) * acc + jnp.dot(p, vaug_ref[h, rows, :],
                                                     preferred_element_type=F32)
            new += [m_new, acc]
        return tuple(new)

    def lagged_body(g, carry):
        rows = pl.ds(pl.multiple_of(g * span, span), span)
        kg = kaug_ref[:, rows]
        new = []
        for h in range(2):
            m, acc = carry[2 * h], carry[2 * h + 1]
            s = jnp.dot(qa[h], kg, preferred_element_type=F32)
            p = jnp.exp2(s - m).astype(BF16)
            m_new = jnp.maximum(m, jnp.max(s, axis=1, keepdims=True))
            acc = jnp.exp2(m - m_new) * (acc + jnp.dot(p, vaug_ref[h, rows, :],
                                                       preferred_element_type=F32))
            new += [m_new, acc]
        return tuple(new)

    kabs = kabs_ref[0:1, :]
    rise = [jnp.sum(jnp.abs(qh[h].astype(F32)) * kabs, axis=1, keepdims=True) - state[2 * h]
            for h in range(2)]
    lag_ok = jnp.max(jnp.maximum(rise[0], rise[1])) <= MAX_LAG_RISE

    n_groups = lax.div(i + (group - 1), group)
    _, acc_a, _, acc_b = lax.cond(
        lag_ok,
        lambda: lax.fori_loop(0, n_groups, lagged_body, tuple(state)),
        lambda: lax.fori_loop(0, n_groups, exact_body, tuple(state)))
    num = jnp.where(head_a, acc_a, acc_b)
    den = jnp.where(head_a, pltpu.roll(acc_a, ATT_HEAD_DIM, 1), pltpu.roll(acc_b, ATT_HEAD_DIM, 1))
    o = num / den
    sq = o * o
    ms_a = jnp.sum(jnp.where(head_a, sq, 0.0), axis=1, keepdims=True)
    ms_b = jnp.sum(jnp.where(head_a, 0.0, sq), axis=1, keepdims=True)
    mean_sq = jnp.where(head_a, ms_a, ms_b) * (1.0 / ATT_HEAD_DIM)
    o_ref[...] = (o * lax.rsqrt(mean_sq + RMS_EPS) * gain_ref[...]).astype(BF16)


def _moba_attention(q, k, v, kmean, gain, B, S, to_bf16=(), zeros=()):
    T, W = q.shape
    nq = S // MOBA_BLOCK
    npairs = W // LANES
    group = 4 if nq % 4 == 0 else 1
    steps = B * npairs * nq
    step = lambda b, h, i: ((b * npairs + h) * nq + i, 0)
    ride = lambda a: pl.BlockSpec((a.shape[0] // steps, a.shape[1]), step)
    outs = pl.pallas_call(
        functools.partial(_attn_kernel, group=group, n_cast=len(to_bf16), n_zero=len(zeros)),
        grid=(B, npairs, nq),
        in_specs=[pl.BlockSpec((MOBA_BLOCK, LANES), lambda b, h, i: (b * nq + i, h)),
                  pl.BlockSpec((S, LANES), lambda b, h, i: (b, h)),
                  pl.BlockSpec((S, LANES), lambda b, h, i: (b, h)),
                  pl.BlockSpec((None, LANES, LANES), lambda b, h, i: (b, 0, h)),
                  pl.BlockSpec((1, LANES), lambda b, h, i: (0, h))] + [ride(a) for a in to_bf16],
        out_specs=[pl.BlockSpec((MOBA_BLOCK, LANES), lambda b, h, i: (b * nq + i, h))]
        + [ride(a) for a in to_bf16] + [ride(a) for a in zeros],
        out_shape=[jax.ShapeDtypeStruct((T, W), BF16)]
        + [jax.ShapeDtypeStruct(a.shape, BF16) for a in to_bf16] + list(zeros),
        scratch_shapes=[pltpu.VMEM((2 * LANES, S), BF16), pltpu.VMEM((2, S, LANES), BF16),
                        pltpu.VMEM((SUBLANES, LANES), F32)],
        compiler_params=_params("parallel", "parallel", "arbitrary"),
        name="moba_attention",
    )(q, k, v, kmean, gain, *to_bf16)
    n = len(to_bf16)
    return outs[0], outs[1:1 + n], outs[1 + n:]


def _can_ride_attention(a, B, S):
    steps = B * (ATT_WIDTH // LANES) * (S // MOBA_BLOCK)
    return a.shape[0] % (steps * 2 * SUBLANES) == 0


def _mlstm_kernel(mqk_ref, mv_ref, mo_ref, g_ref, cw_ref, cb_ref, gain_ref, o_ref,
                  c_ref, m_ref, tail_ref, kt_ref):
    L = ML_CHUNK
    R = mqk_ref.shape[0]

    @pl.when(pl.program_id(1) == 0)
    def _reset():
        c_ref[...] = jnp.zeros_like(c_ref)
        m_ref[...] = jnp.zeros_like(m_ref)
        tail_ref[...] = jnp.zeros_like(tail_ref)

    cur = mqk_ref[...].astype(F32)
    xp = jnp.concatenate([tail_ref[...], cur], axis=0)
    base = SUBLANES - (CONV_WIDTH - 1)
    y_all = cb_ref[...] + cw_ref[0:1, :] * xp[base:base + R]
    for j in range(1, CONV_WIDTH):
        y_all = y_all + cw_ref[j:j + 1, :] * xp[base + j:base + j + R]
    tail_ref[...] = cur[R - SUBLANES:]
    y_all = y_all * _sigmoid(y_all)
    for c in range(R // L):
        _mlstm_chunk(y_all[c * L:(c + 1) * L], slice(c * L, (c + 1) * L), mv_ref, mo_ref, g_ref,
                     gain_ref, o_ref, c_ref, m_ref, kt_ref)


def _mlstm_chunk(y, rows, mv_ref, mo_ref, g_ref, gain_ref, o_ref, c_ref, m_ref, kt_ref):
    L = ML_CHUNK

    lane = _iota((L, LANES), 1)
    row = _iota((L, LANES), 0)
    tri = lane <= row
    g8 = g_ref[rows, :].T[:SUBLANES]
    grow = _iota((SUBLANES, L), 0)
    logf = -(jnp.maximum(-g8, 0.0) + jnp.log1p(jnp.exp(-jnp.abs(g8))))
    upper = (row <= lane).astype(F32)
    bcum = jnp.dot(jnp.where(grow >= ML_HEADS, logf, 0.0), upper, precision=HIGHEST,
                   preferred_element_type=F32)
    rows8 = jnp.where(grow < ML_HEADS, g8, bcum)
    cols = jnp.concatenate([rows8, jnp.zeros((LANES - SUBLANES, L), F32)], axis=0).T

    for p in range(ML_HEADS // 2):
        kt_ref[p] = y[:, ML_QK_WIDTH + p * LANES:ML_QK_WIDTH + (p + 1) * LANES].T
    ones_col = jnp.where(lane == 0, 1.0, 0.0).astype(BF16)
    for h in range(ML_HEADS):
        pair, odd = divmod(h, 2)
        in_head = (lane >= odd * ML_QK_DIM) & (lane < (odd + 1) * ML_QK_DIM)
        dim_in_head = (row >= odd * ML_QK_DIM) & (row < (odd + 1) * ML_QK_DIM)
        yq = y[:, pair * LANES:(pair + 1) * LANES]
        qh = jnp.where(in_head, yq * (ML_QK_DIM ** -0.5), 0.0).astype(BF16)
        kt = jnp.where(dim_in_head, kt_ref[pair], 0.0).astype(BF16)
        vh = mv_ref[rows, h * ML_V_DIM:(h + 1) * ML_V_DIM]
        vaug = jnp.concatenate([vh, ones_col], axis=1)

        i_c = cols[:, h:h + 1]
        b_c = cols[:, ML_HEADS + h:ML_HEADS + h + 1]
        i_r = rows8[h:h + 1, :]
        b_r = rows8[ML_HEADS + h:ML_HEADS + h + 1, :]
        m_prev = m_ref[h:h + 1, 0:1]

        d_log = jnp.where(tri, b_c - b_r + i_r, NEG_INF)
        inter = b_c + m_prev
        m_t = jnp.maximum(inter, jnp.max(d_log, axis=1, keepdims=True))
        w_intra = jnp.exp(d_log - m_t)
        w_inter = jnp.exp(inter - m_t)

        state = c_ref[h]
        s = jnp.dot(qh, kt, preferred_element_type=F32) * w_intra
        numden = (jnp.dot(s.astype(BF16), vaug, preferred_element_type=F32)
                  + w_inter * jnp.dot(qh, state.astype(BF16), preferred_element_type=F32))
        num = numden[:, :ML_V_DIM]
        den = numden[:, ML_V_DIM:ML_V_DIM + 1]
        hcur = num / jnp.maximum(jnp.abs(den), jnp.exp(-m_t))

        b_last = b_c[L - 1:L, :]
        decay = b_last - b_c + i_c
        m_new = jnp.maximum(b_last + m_prev, jnp.max(decay, axis=0, keepdims=True))
        w_state = jnp.exp(decay - m_new)
        carry_scale = jnp.exp(b_last + m_prev - m_new)
        vw = (vaug.astype(F32) * w_state).astype(BF16)
        c_ref[h] = carry_scale * state + jnp.dot(kt, vw, preferred_element_type=F32)
        m_ref[h:h + 1, :] = jnp.broadcast_to(m_new, (1, LANES))

        mean_sq = jnp.mean(hcur * hcur, axis=1, keepdims=True)
        sl = slice(h * ML_V_DIM, (h + 1) * ML_V_DIM)
        gate = _sigmoid(mo_ref[rows, sl].astype(F32))
        o_ref[rows, sl] = (hcur * lax.rsqrt(mean_sq + RMS_EPS) * gain_ref[:, sl]
                        * gate).astype(BF16)


def _mlstm(mqk, mv, mo, gates, conv_w, conv_b, gain, B, S):
    T, W = mv.shape
    L = ML_CHUNK
    R = L
    nc = S // R
    row = lambda b, c: (b * nc + c, 0)
    const = lambda b, c: (0, 0)
    return pl.pallas_call(
        _mlstm_kernel,
        grid=(B, nc),
        in_specs=[pl.BlockSpec((R, W), row),
                  pl.BlockSpec((R, W), row),
                  pl.BlockSpec((R, W), row),
                  pl.BlockSpec((R, LANES), row),
                  pl.BlockSpec(conv_w.shape, const),
                  pl.BlockSpec((1, W), const),
                  pl.BlockSpec((1, W), const)],
        out_specs=pl.BlockSpec((R, W), row),
        out_shape=jax.ShapeDtypeStruct((T, W), BF16),
        scratch_shapes=[pltpu.VMEM((ML_HEADS, LANES, 2 * ML_V_DIM), F32),
                        pltpu.VMEM((SUBLANES, LANES), F32),
                        pltpu.VMEM((SUBLANES, W), F32),
                        pltpu.VMEM((ML_HEADS // 2, LANES, L), F32)],
        compiler_params=_params("parallel", "arbitrary"),
        name="mlstm",
    )(mqk, mv, mo, gates, conv_w, conv_b, gain)


def _outproj_kernel(att_ref, ml_ref, wa_ref, wb_ref, x_ref, g_ref, b_ref, o_ref, *, alpha):
    mix = (jnp.dot(att_ref[...], wa_ref[...], preferred_element_type=F32)
           + jnp.dot(ml_ref[...], wb_ref[...], preferred_element_type=F32))
    o_ref[...] = _layer_norm(alpha * x_ref[...] + mix, g_ref[...], b_ref[...])


def _outproj(att, ml, w_out, x2, ln_g, ln_b, alpha):
    T, D = x2.shape
    W = att.shape[1]
    tm = min(T, 512)
    row = lambda i: (i, 0)
    const = lambda i: (0, 0)
    return pl.pallas_call(
        functools.partial(_outproj_kernel, alpha=alpha),
        grid=(T // tm,),
        in_specs=[pl.BlockSpec((tm, W), row),
                  pl.BlockSpec((tm, W), row),
                  pl.BlockSpec((W, D), lambda i: (0, 0)),
                  pl.BlockSpec((W, D), lambda i: (1, 0)),
                  pl.BlockSpec((tm, D), row),
                  pl.BlockSpec((1, D), const),
                  pl.BlockSpec((1, D), const)],
        out_specs=pl.BlockSpec((tm, D), row),
        out_shape=jax.ShapeDtypeStruct((T, D), F32),
        compiler_params=_params("parallel"),
        name="outproj_ln",
    )(att, ml, w_out, w_out, x2, ln_g, ln_b)


def _swiglu_chunk(xb, wg, wu, wd):
    g = jnp.dot(xb, wg, preferred_element_type=F32)
    u = jnp.dot(xb, wu, preferred_element_type=F32)
    h = (g * _sigmoid(g) * u).astype(BF16)
    return jnp.dot(h, wd, preferred_element_type=F32)


def _dense_ffn_kernel(x_ref, wg_ref, wu_ref, wd_ref, g_ref, b_ref, o_ref, *, alpha, chunk):
    x = x_ref[...]
    xb = x.astype(BF16)
    F = wg_ref.shape[1]
    acc = None
    for a in range(0, F, chunk):
        b = min(a + chunk, F)
        part = _swiglu_chunk(xb, wg_ref[:, a:b], wu_ref[:, a:b], wd_ref[a:b, :])
        acc = part if acc is None else acc + part
    o_ref[...] = _layer_norm(alpha * x + acc, g_ref[...], b_ref[...])


def _hidden_chunk(width, target):
    for unit in (2 * LANES, LANES):
        best = 0
        for c in range(unit, target + 1, unit):
            if width % c == 0:
                best = c
        if best:
            return best
    return width


def _dense_ffn(x2, wg, wu, wd, ln_g, ln_b, alpha):
    T, D = x2.shape
    F = wg.shape[1]
    tm = min(T, 512)
    row = lambda i: (i, 0)
    const = lambda i: (0, 0)
    resident = dict(pipeline_mode=pl.Buffered(1))
    return pl.pallas_call(
        functools.partial(_dense_ffn_kernel, alpha=alpha, chunk=4 * LANES),
        grid=(T // tm,),
        in_specs=[pl.BlockSpec((tm, D), row),
                  pl.BlockSpec((D, F), const, **resident),
                  pl.BlockSpec((D, F), const, **resident),
                  pl.BlockSpec((F, D), const, **resident),
                  pl.BlockSpec((1, D), const),
                  pl.BlockSpec((1, D), const)],
        out_specs=pl.BlockSpec((tm, D), row),
        out_shape=jax.ShapeDtypeStruct((T, D), F32),
        compiler_params=_params("parallel"),
        name="dense_ffn_ln",
    )(x2, wg, wu, wd, ln_g, ln_b)


def _router_kernel(x_ref, rw_ref, rb_ref, ids_ref, wts_ref, cnt_ref, carry_ref, *, n_experts):
    @pl.when(pl.program_id(0) == 0)
    def _reset():
        carry_ref[...] = jnp.zeros_like(carry_ref)

    tm = x_ref.shape[0]
    x_hi, x_lo = _split_bf16(x_ref[...], 2)
    w_hi, w_lo = _split_bf16(rw_ref[...], 2)
    logits = (jnp.dot(x_hi, w_hi, preferred_element_type=F32)
              + (jnp.dot(x_lo, w_hi, preferred_element_type=F32)
                 + jnp.dot(x_hi, w_lo, preferred_element_type=F32))) + rb_ref[...]
    lane = _iota((tm, LANES), 1)
    logits = jnp.where(lane < n_experts, logits, NEG_INF)

    def top(lg):
        best = jnp.max(lg, axis=1, keepdims=True)
        first = jnp.min(jnp.where(lg == best, lane, LANES), axis=1, keepdims=True)
        return best, first

    v1, e1 = top(logits)
    hot1 = lane == e1
    v2, e2 = top(jnp.where(hot1, NEG_INF, logits))
    hot2 = lane == e2
    ex = jnp.exp(v2 - v1)
    w1 = 1.0 / (1.0 + ex)
    w2 = ex / (1.0 + ex)

    assigned = (hot1 | hot2).astype(BF16)
    before = (_iota((tm, tm), 1) < _iota((tm, tm), 0)).astype(BF16)
    carry = carry_ref[0:1, :]
    rank = jnp.dot(before, assigned, preferred_element_type=F32) + carry
    r1 = jnp.sum(jnp.where(hot1, rank, 0.0), axis=1, keepdims=True).astype(I32)
    r2 = jnp.sum(jnp.where(hot2, rank, 0.0), axis=1, keepdims=True).astype(I32)
    total = carry + jnp.sum(assigned.astype(F32), axis=0, keepdims=True)
    carry_ref[...] = jnp.broadcast_to(total, carry_ref.shape)
    cnt_ref[...] = jnp.broadcast_to(total, cnt_ref.shape).astype(I32)

    ids_ref[...] = jnp.where(lane == 0, e1, jnp.where(lane == 1, e2,
                             jnp.where(lane == 2, r1, jnp.where(lane == 3, r2, 0))))
    wts_ref[...] = jnp.where(lane == 0, w1, jnp.where(lane == 1, w2, 0.0))


def _router(x2, router_w, router_b):
    T, D = x2.shape
    E = router_w.shape[1]
    tm = min(T, 512)
    rw = jnp.pad(router_w, ((0, 0), (0, LANES - E)))
    rb = jnp.pad(router_b, (0, LANES - E))[None, :]
    row = lambda i: (i, 0)
    const = lambda i: (0, 0)
    return pl.pallas_call(
        functools.partial(_router_kernel, n_experts=E),
        grid=(T // tm,),
        in_specs=[pl.BlockSpec((tm, D), row),
                  pl.BlockSpec((D, LANES), const),
                  pl.BlockSpec((1, LANES), const)],
        out_specs=[pl.BlockSpec((tm, LANES), row),
                   pl.BlockSpec((tm, LANES), row),
                   pl.BlockSpec((SUBLANES, LANES), const)],
        out_shape=[jax.ShapeDtypeStruct((T, LANES), I32),
                   jax.ShapeDtypeStruct((T, LANES), F32),
                   jax.ShapeDtypeStruct((SUBLANES, LANES), I32)],
        scratch_shapes=[pltpu.VMEM((SUBLANES, LANES), F32)],
        compiler_params=_params("arbitrary"),
        name="router",
    )(x2, rw, rb)


def _slots_kernel(ids_ref, tiles_ref, out_ref, *, tile_rows):
    ids = ids_ref[...]
    lane = _iota(ids.shape, 1)
    tiles = tiles_ref[...]

    def slot(k):
        pick = jnp.where(lane == ids[:, k:k + 1], 1.0, 0.0).astype(BF16)
        start = jnp.dot(pick, tiles, preferred_element_type=F32)
        return start.astype(I32) * tile_rows + ids[:, TOP_K + k:TOP_K + k + 1]

    out_ref[...] = jnp.where(lane == 0, slot(0), jnp.where(lane == 1, slot(1), 0))


def _row_slots(ids, offs, tm, tile_rows, n_tiles):
    T = ids.shape[0]
    E = offs.shape[0]
    ts = min(T, 2048)
    assert n_tiles <= 256, "start tiles must stay exact in bf16"
    start_tile = jnp.pad(offs // tile_rows, (0, LANES - E)).astype(BF16)
    tiles = jnp.broadcast_to(start_tile[:, None], (LANES, LANES))
    slots = pl.pallas_call(
        functools.partial(_slots_kernel, tile_rows=tile_rows),
        grid=(T // ts,),
        in_specs=[pl.BlockSpec((ts, LANES), lambda i: (i, 0)),
                  pl.BlockSpec((LANES, LANES), lambda i: (0, 0))],
        out_specs=pl.BlockSpec((ts, LANES), lambda i: (i, 0)),
        out_shape=jax.ShapeDtypeStruct((T, LANES), I32),
        compiler_params=_params("parallel"),
        name="moe_slots",
    )(ids, tiles)
    return slots[:, :TOP_K].reshape(T // tm, tm, TOP_K).transpose(0, 2, 1)


def _for_each_row(tm, fn):
    def group(t, c):
        base = pl.multiple_of(t * DMA_UNROLL, DMA_UNROLL)
        for u in range(DMA_UNROLL):
            for k in range(TOP_K):
                fn(base + u, k)
        return c

    lax.fori_loop(0, tm // DMA_UNROLL, group, 0)


def _scatter_kernel(slot_ref, x_ref, xg_in_ref, xg_ref, sem):
    del xg_in_ref
    tm = x_ref.shape[0]

    def copy(r, k):
        return pltpu.make_async_copy(x_ref.at[pl.ds(r, 1)],
                                     xg_ref.at[pl.ds(slot_ref[0, k, r], 1)], sem)

    _for_each_row(tm, lambda r, k: copy(r, k).start())
    _for_each_row(tm, lambda r, k: copy(r, k).wait())


def _scatter_rows(x2, slots, xg0):
    T, D = x2.shape
    nt, _, tm = slots.shape
    n_rows = xg0.shape[0]
    return pl.pallas_call(
        _scatter_kernel,
        grid=(nt,),
        in_specs=[pl.BlockSpec((1, TOP_K, tm), lambda i: (i, 0, 0), memory_space=pltpu.SMEM),
                  pl.BlockSpec((tm, D), lambda i: (i, 0)),
                  pl.BlockSpec(memory_space=pl.ANY)],
        out_specs=pl.BlockSpec(memory_space=pl.ANY),
        out_shape=jax.ShapeDtypeStruct((n_rows, D), F32),
        scratch_shapes=[pltpu.SemaphoreType.DMA(())],
        input_output_aliases={2: 0},
        compiler_params=_params("arbitrary"),
        name="moe_scatter",
    )(slots, x2, xg0)


def _expert_kernel(te_ref, na_ref, x_ref, wg_ref, wu_ref, wd_ref, o_ref, xb_ref, acc_ref):
    g = pl.program_id(0)
    f = pl.program_id(1)
    active = g < na_ref[0]

    @pl.when(active & (f == 0))
    def _start():
        xb_ref[...] = x_ref[...].astype(BF16)
        acc_ref[...] = jnp.zeros_like(acc_ref)

    @pl.when(active)
    def _accumulate():
        acc_ref[...] += _swiglu_chunk(xb_ref[...], wg_ref[...], wu_ref[...], wd_ref[...])

    last = f == pl.num_programs(1) - 1

    @pl.when(active & last)
    def _finish():
        o_ref[...] = acc_ref[...]

    @pl.when(jnp.logical_not(active) & last)
    def _unused_tile():
        o_ref[...] = jnp.zeros_like(o_ref)


def _expert_ffn(xg, wg, wu, wd, tile_expert, n_active, tm):
    P, D = xg.shape
    E, _, F = wg.shape
    tf = _hidden_chunk(F, 1792)
    nf = F // tf

    def tile(g, te, na):
        return jnp.maximum(jnp.minimum(g, na[0] - 1), 0)

    def chunk(g, f, na):
        return jnp.where(g < na[0], f, nf - 1)

    return pl.pallas_call(
        _expert_kernel,
        grid_spec=pltpu.PrefetchScalarGridSpec(
            num_scalar_prefetch=2,
            grid=(P // tm, nf),
            in_specs=[pl.BlockSpec((tm, D), lambda g, f, te, na: (tile(g, te, na), 0)),
                      pl.BlockSpec((None, D, tf),
                                   lambda g, f, te, na: (te[tile(g, te, na)], 0, chunk(g, f, na))),
                      pl.BlockSpec((None, D, tf),
                                   lambda g, f, te, na: (te[tile(g, te, na)], 0, chunk(g, f, na))),
                      pl.BlockSpec((None, tf, D),
                                   lambda g, f, te, na: (te[tile(g, te, na)], chunk(g, f, na), 0))],
            out_specs=pl.BlockSpec((tm, D), lambda g, f, te, na: (g, 0)),
            scratch_shapes=[pltpu.VMEM((tm, D), BF16), pltpu.VMEM((tm, D), F32)]),
        out_shape=jax.ShapeDtypeStruct((P, D), F32),
        compiler_params=_params("arbitrary", "arbitrary"),
        name="moe_experts",
    )(tile_expert, n_active, xg, wg, wu, wd)


def _combine_kernel(slot_ref, next_ref, yg_ref, wts_ref, x_ref, g_ref, b_ref, o_ref,
                    buf_ref, sem, *, alpha):
    tm = x_ref.shape[0]
    i = pl.program_id(0)
    cur = lax.rem(i, 2)

    def copy(table, buf, r, k):
        return pltpu.make_async_copy(yg_ref.at[pl.ds(table[0, k, r], 1)],
                                     buf_ref.at[buf, k, pl.ds(r, 1)], sem.at[buf])

    @pl.when(i == 0)
    def _first_tile():
        _for_each_row(tm, lambda r, k: copy(slot_ref, 0, r, k).start())

    @pl.when(i + 1 < pl.num_programs(0))
    def _next_tile():
        _for_each_row(tm, lambda r, k: copy(next_ref, 1 - cur, r, k).start())

    _for_each_row(tm, lambda r, k: copy(slot_ref, cur, r, k).wait())
    w = wts_ref[...]
    ffn = w[:, 0:1] * buf_ref[cur, 0] + w[:, 1:2] * buf_ref[cur, 1]
    o_ref[...] = _layer_norm(alpha * x_ref[...] + ffn, g_ref[...], b_ref[...])


def _combine(yg, slots, wts, x2, ln_g, ln_b, alpha):
    T, D = x2.shape
    nt, _, tm = slots.shape
    row = lambda i: (i, 0)
    const = lambda i: (0, 0)
    smem = dict(memory_space=pltpu.SMEM)
    return pl.pallas_call(
        functools.partial(_combine_kernel, alpha=alpha),
        grid=(nt,),
        in_specs=[pl.BlockSpec((1, TOP_K, tm), lambda i: (i, 0, 0), **smem),
                  pl.BlockSpec((1, TOP_K, tm), lambda i: (jnp.minimum(i + 1, nt - 1), 0, 0), **smem),
                  pl.BlockSpec(memory_space=pl.ANY),
                  pl.BlockSpec((tm, LANES), row),
                  pl.BlockSpec((tm, D), row),
                  pl.BlockSpec((1, D), const),
                  pl.BlockSpec((1, D), const)],
        out_specs=pl.BlockSpec((tm, D), row),
        out_shape=jax.ShapeDtypeStruct((T, D), F32),
        scratch_shapes=[pltpu.VMEM((2, TOP_K, tm, D), F32), pltpu.SemaphoreType.DMA((2,))],
        compiler_params=_params("arbitrary"),
        name="moe_combine_ln",
    )(slots, slots, yg, wts, x2, ln_g, ln_b)


def _moe_tile(T):
    return min(T, 512)


def _moe_rows(T, E):
    return TOP_K * T + E * _moe_tile(T)


def _moe_ffn(x2, router_w, router_b, wg, wu, wd, ln_g, ln_b, alpha, xg0):
    T, D = x2.shape
    E = router_w.shape[1]
    tm_e = _moe_tile(T)
    tm_r = min(T, 256)
    ids, wts, counts = _router(x2, router_w, router_b)

    cnt = counts[0, :E]
    padded = ((cnt + tm_e - 1) // tm_e) * tm_e
    ends = jnp.cumsum(padded)
    offs = (ends - padded).astype(I32)
    n_rows = xg0.shape[0]
    n_tiles = n_rows // tm_e
    tile_start = jnp.arange(n_tiles, dtype=I32) * tm_e
    tile_expert = jnp.minimum(jnp.sum(tile_start[:, None] >= ends[None, :], axis=1), E - 1).astype(I32)
    n_active = (ends[-1:] // tm_e).astype(I32)

    slots = _row_slots(ids, offs, tm_r, tm_e, n_tiles)
    xg = _scatter_rows(x2, slots, xg0)
    yg = _expert_ffn(xg, wg, wu, wd, tile_expert, n_active, tm_e)
    return _combine(yg, slots, wts, x2, ln_g, ln_b, alpha)


def kernel(x, positions, w_in, gate_bias, conv_w, conv_b, norm_attn, norm_mlstm, w_out,
           ln1_g, ln1_b, dense_w_gate, dense_w_up, dense_w_down, router_w, router_b,
           moe_w_gate, moe_w_up, moe_w_down, ln2_g, ln2_b):
    B, S, D = x.shape
    depth = w_in.shape[0]
    T = B * S
    alpha = (2.0 * depth) ** 0.25
    n_main = 3 * ATT_WIDTH + 2 * ML_QK_WIDTH + 2 * ML_WIDTH
    nblk = S // MOBA_BLOCK

    tabs = _rope_tables(positions)
    x2 = x.reshape(T, D)
    moe_bf16 = None
    for l in range(depth):
        w_main = w_in[l, :, :n_main].astype(BF16)
        w_gate = jnp.pad(w_in[l, :, n_main:], ((0, 0), (0, LANES - 2 * ML_HEADS))).astype(BF16)
        gbias = jnp.pad(gate_bias[l], (0, LANES - 2 * ML_HEADS))[None, :]
        q, k, v, mqk, mv, mo, gates, kmean = _inproj(x2, w_main, w_gate, gbias, tabs)
        kmean = jnp.pad(kmean.reshape(B, nblk, ATT_WIDTH), ((0, 0), (0, LANES - nblk), (0, 0)))
        ride = []
        if l % 2 == 0 and l + 1 < depth:
            ride = [w[l // 2].reshape(-1, w.shape[-1]) for w in (moe_w_gate, moe_w_up, moe_w_down)]
            if not all(_can_ride_attention(a, B, S) for a in ride):
                ride = []
        clear = []
        if l % 2 == 1:
            clear = [jax.ShapeDtypeStruct((_moe_rows(T, router_w.shape[-1]), D), F32)]
            if not _can_ride_attention(clear[0], B, S):
                clear = []
        att, rode, cleared = _moba_attention(q, k, v, kmean, norm_attn[l][None, :], B, S,
                                             ride, clear)
        if ride:
            moe_bf16 = [a.reshape(w.shape[1:]) for a, w in zip(rode, (moe_w_gate, moe_w_up, moe_w_down))]
        ml = _mlstm(mqk, mv, mo, gates, conv_w[l], conv_b[l][None, :],
                    norm_mlstm[l][None, :], B, S)
        x2 = _outproj(att, ml, w_out[l].astype(BF16), x2, ln1_g[l][None, :], ln1_b[l][None, :],
                      alpha)
        j = l // 2
        if l % 2 == 0:
            x2 = _dense_ffn(x2, dense_w_gate[j].astype(BF16), dense_w_up[j].astype(BF16),
                            dense_w_down[j].astype(BF16), ln2_g[l][None, :], ln2_b[l][None, :],
                            alpha)
        else:
            if moe_bf16 is None:
                moe_bf16 = [w[j].astype(BF16) for w in (moe_w_gate, moe_w_up, moe_w_down)]
            xg0 = cleared[0] if cleared else jnp.zeros((_moe_rows(T, router_w.shape[-1]), D), F32)
            x2 = _moe_ffn(x2, router_w[j], router_b[j], *moe_bf16,
                          ln2_g[l][None, :], ln2_b[l][None, :], alpha, xg0)
            moe_bf16 = None
    return x2.reshape(B, S, D)
```

```python
import functools

import jax
import jax.numpy as jnp
from jax import lax
from jax.experimental import pallas as pl
from jax.experimental.pallas import tpu as pltpu

F32 = jnp.float32
BF16 = jnp.bfloat16
I32 = jnp.int32

ATT_HEADS = 8
ATT_HEAD_DIM = 64
ATT_WIDTH = ATT_HEADS * ATT_HEAD_DIM
MOBA_BLOCK = 256
MOBA_TOPK = 3
ROPE_THETA = 500000.0
ROPE_DIM = ATT_HEAD_DIM // 4
ML_HEADS = 4
ML_V_DIM = 128
ML_QK_DIM = 64
ML_QK_WIDTH = ML_HEADS * ML_QK_DIM
ML_WIDTH = ML_HEADS * ML_V_DIM
ML_CHUNK = 128
CONV_WIDTH = 4
TOP_K = 2
LN_EPS = 1e-5
RMS_EPS = 1e-6

LANES = 128
SUBLANES = 8
V7X_VMEM_BYTES = 64 * 1024 * 1024
VMEM_LIMIT = (V7X_VMEM_BYTES * 3) // 4

MASK_BIAS = -1e9
MAX_RISE = 60.0
NEG_INF = float("-inf")
HIGHEST = lax.Precision.HIGHEST
LOG2_E = 1.4426950408889634
DMA_UNROLL = 8

_NT = (((1,), (1,)), ((), ()))


def _params(*sem):
    return pltpu.CompilerParams(dimension_semantics=sem, vmem_limit_bytes=VMEM_LIMIT)


def _iota(shape, dim):
    return lax.broadcasted_iota(I32, shape, dim)


def _sigmoid(x):
    return 1.0 / (1.0 + jnp.exp(-x))


def _layer_norm(y, g, b):
    mu = jnp.mean(y, axis=-1, keepdims=True)
    yc = y - mu
    var = jnp.mean(yc * yc, axis=-1, keepdims=True)
    return yc * lax.rsqrt(var + LN_EPS) * g + b


def _rope_tab_kernel(pos_ref, freq_ref, cos_ref, s1_ref, s2_ref):
    ang = pos_ref[...].astype(F32) * freq_ref[...]
    d = _iota(ang.shape, 1) & (ATT_HEAD_DIM - 1)
    half = ROPE_DIM // 2
    s = jnp.sin(ang)
    cos_ref[...] = jnp.cos(ang)
    s1_ref[...] = jnp.where(d < half, -s, 0.0)
    s2_ref[...] = jnp.where((d >= half) & (d < ROPE_DIM), s, 0.0)


def _rope_tables(positions):
    T = positions.size
    tm = min(T, 1024)
    half = ROPE_DIM // 2
    inv_freq = ROPE_THETA ** (-jnp.arange(half, dtype=F32) / half)
    d = jnp.arange(LANES) % ATT_HEAD_DIM
    freq = jnp.where(d < ROPE_DIM, inv_freq[d % half], 0.0).astype(F32)[None, :]
    tab = jax.ShapeDtypeStruct((T, LANES), F32)
    return pl.pallas_call(
        _rope_tab_kernel,
        grid=(T // tm,),
        in_specs=[pl.BlockSpec((tm, 1), lambda i: (i, 0)),
                  pl.BlockSpec((1, LANES), lambda i: (0, 0))],
        out_specs=[pl.BlockSpec((tm, LANES), lambda i: (i, 0))] * 3,
        out_shape=[tab, tab, tab],
        compiler_params=_params("parallel"),
        name="rope_tables",
    )(positions.reshape(T, 1), freq)


def _inproj_kernel(x_ref, w_ref, wg_ref, gb_ref, cos_ref, s1_ref, s2_ref,
                   q_ref, k_ref, v_ref, mqk_ref, mv_ref, mo_ref, g_ref, km_ref):
    xb = x_ref[...].astype(BF16)
    W = ATT_WIDTH

    def proj(c):
        return jnp.dot(xb, w_ref[:, c * W:(c + 1) * W], preferred_element_type=F32)

    rep = W // LANES
    cos = jnp.concatenate([cos_ref[...]] * rep, axis=1)
    s1 = jnp.concatenate([s1_ref[...]] * rep, axis=1)
    s2 = jnp.concatenate([s2_ref[...]] * rep, axis=1)
    half = ROPE_DIM // 2

    def rope(t):
        return t * cos + pltpu.roll(t, W - half, 1) * s1 + pltpu.roll(t, half, 1) * s2

    q_ref[...] = (rope(proj(0)) * (LOG2_E * ATT_HEAD_DIM ** -0.5)).astype(BF16)
    k = rope(proj(1))
    k_ref[...] = k.astype(BF16)
    for g in range(k.shape[0] // MOBA_BLOCK):
        blk = k[g * MOBA_BLOCK:(g + 1) * MOBA_BLOCK]
        km_ref[0, g:g + 1, :] = jnp.sum(blk, axis=0, keepdims=True) * (1.0 / MOBA_BLOCK)
    v_ref[...] = proj(2).astype(BF16)
    mqk_ref[...] = proj(3).astype(BF16)
    mv_ref[...] = proj(4).astype(BF16)
    mo_ref[...] = proj(5).astype(BF16)
    g_ref[...] = jnp.dot(xb, wg_ref[...], preferred_element_type=F32) + gb_ref[...]


def _inproj(x2, w_main, w_gate, gate_bias, tabs):
    T, D = x2.shape
    tm = min(T, 512)
    nkb = tm // MOBA_BLOCK
    W = ATT_WIDTH
    row = lambda i: (i, 0)
    const = lambda i: (0, 0)
    act = jax.ShapeDtypeStruct((T, W), BF16)
    return pl.pallas_call(
        _inproj_kernel,
        grid=(T // tm,),
        in_specs=[pl.BlockSpec((tm, D), row),
                  pl.BlockSpec(w_main.shape, const),
                  pl.BlockSpec(w_gate.shape, const),
                  pl.BlockSpec((1, LANES), const),
                  pl.BlockSpec((tm, LANES), row),
                  pl.BlockSpec((tm, LANES), row),
                  pl.BlockSpec((tm, LANES), row)],
        out_specs=[pl.BlockSpec((tm, W), row)] * 6
        + [pl.BlockSpec((tm, LANES), row),
           pl.BlockSpec((1, nkb, W), lambda i: (i, 0, 0))],
        out_shape=[act] * 6 + [jax.ShapeDtypeStruct((T, LANES), F32),
                               jax.ShapeDtypeStruct((T // tm, nkb, W), F32)],
        compiler_params=_params("parallel"),
        name="inproj",
    )(x2, w_main, w_gate, gate_bias, *tabs)


def _split_bf16(x, parts):
    out = []
    for _ in range(parts):
        hi = x.astype(BF16)
        out.append(hi)
        x = x - hi.astype(F32)
    return out


def _attn_kernel(q_ref, k_ref, v_ref, km_ref, gain_ref, *rest, group, n_cast, n_zero):
    cast_in, rest = rest[:n_cast], rest[n_cast:]
    o_ref, rest = rest[0], rest[1:]
    cast_out, rest = rest[:n_cast], rest[n_cast:]
    zero_out, (kaug_ref, vaug_ref, kabs_ref) = rest[:n_zero], rest[n_zero:]
    for src_ref, dst_ref in zip(cast_in, cast_out):
        dst_ref[...] = src_ref[...].astype(BF16)
    for dst_ref in zero_out:
        dst_ref[...] = jnp.zeros_like(dst_ref)
    i = pl.program_id(2)
    tq = q_ref.shape[0]
    nblk = k_ref.shape[0] // MOBA_BLOCK
    lane = _iota((tq, LANES), 1)
    head_a = lane < ATT_HEAD_DIM

    @pl.when(i == 0)
    def _build_augmented_keys_values():
        blane = _iota((MOBA_BLOCK, LANES), 1)
        brow_k = _iota((LANES, MOBA_BLOCK), 0)
        in_a = blane < ATT_HEAD_DIM
        one = jnp.ones((MOBA_BLOCK, LANES), BF16)

        def fill(j, carry):
            rows = pl.ds(pl.multiple_of(j * MOBA_BLOCK, MOBA_BLOCK), MOBA_BLOCK)
            kaug_ref[:LANES, rows] = k_ref[rows, :].astype(F32).T.astype(BF16)
            kaug_ref[LANES:, rows] = jnp.where(brow_k == j, 1.0, 0.0).astype(BF16)
            vj = v_ref[rows, :]
            vaug_ref[0, rows, :] = jnp.where(in_a, vj, one)
            vaug_ref[1, rows, :] = jnp.where(in_a, one, vj)
            kj = jnp.abs(k_ref[rows, :].astype(F32))
            return jnp.maximum(carry, jnp.max(kj, axis=0, keepdims=True))

        kabs = lax.fori_loop(0, nblk, fill, jnp.zeros((1, LANES), F32))
        kabs_ref[...] = jnp.broadcast_to(kabs, kabs_ref.shape)

    q = q_ref[...]
    zero = jnp.zeros_like(q)
    nrow = -(-nblk // SUBLANES) * SUBLANES
    km_stack = jnp.concatenate(_split_bf16(km_ref[:nrow, :], 3), axis=0)
    brow = _iota((nrow, tq), 0)

    def select_bias(qh):
        parts = lax.dot_general(km_stack, qh, _NT, preferred_element_type=F32)
        gate = parts[:nrow] + parts[nrow:2 * nrow] + parts[2 * nrow:]
        gate = jnp.where(brow < i, gate, NEG_INF)
        picked = brow < 0
        for _ in range(MOBA_TOPK):
            best = jnp.max(gate, axis=0, keepdims=True)
            first = jnp.min(jnp.where(gate == best, brow, nrow), axis=0, keepdims=True)
            hit = (brow == first) & (best > NEG_INF)
            picked = picked | hit
            gate = jnp.where(hit, NEG_INF, gate)
        bias = jnp.where(picked, 0.0, MASK_BIAS)
        if nrow < LANES:
            bias = jnp.concatenate([bias, jnp.full((LANES - nrow, tq), MASK_BIAS, F32)], axis=0)
        return bias.T.astype(BF16)

    qh = (jnp.where(head_a, q, zero), jnp.where(head_a, zero, q))
    qa = [jnp.concatenate([x, select_bias(x)], axis=1) for x in qh]

    own = pl.ds(pl.multiple_of(i * MOBA_BLOCK, MOBA_BLOCK), MOBA_BLOCK)
    kd = kaug_ref[:LANES, own]
    causal = _iota((tq, MOBA_BLOCK), 1) <= _iota((tq, MOBA_BLOCK), 0)
    state = []
    for h in range(2):
        s = jnp.where(causal, jnp.dot(qh[h], kd, preferred_element_type=F32), NEG_INF)
        m = jnp.max(s, axis=1, keepdims=True)
        p = jnp.exp2(s - m).astype(BF16)
        state += [m, jnp.dot(p, vaug_ref[h, own, :], preferred_element_type=F32)]

    span = group * MOBA_BLOCK

    def exact_body(g, carry):
        rows = pl.ds(pl.multiple_of(g * span, span), span)
        kg = kaug_ref[:, rows]
        new = []
        for h in range(2):
            m, acc = carry[2 * h], carry[2 * h + 1]
            s = jnp.dot(qa[h], kg, preferred_element_type=F32)
            m_new = jnp.maximum(m, jnp.max(s, axis=1, keepdims=True))
            p = jnp.exp2(s - m_new).astype(BF16)
            acc = jnp.exp2(m - m_new) * acc + jnp.dot(p, vaug_ref[h, rows, :],
                                                     preferred_element_type=F32)
            new += [m_new, acc]
        return tuple(new)

    def fixed_body(g, carry):
        rows = pl.ds(pl.multiple_of(g * span, span), span)
        kg = kaug_ref[:, rows]
        new = []
        for h in range(2):
            m, acc = carry[2 * h], carry[2 * h + 1]
            s = jnp.dot(qa[h], kg, preferred_element_type=F32)
            p = jnp.exp2(s - m).astype(BF16)
            new += [m, acc + jnp.dot(p, vaug_ref[h, rows, :], preferred_element_type=F32)]
        return tuple(new)

    kabs = kabs_ref[0:1, :]
    rise = [jnp.sum(jnp.abs(qh[h].astype(F32)) * kabs, axis=1, keepdims=True) - state[2 * h]
            for h in range(2)]
    fixed_ok = jnp.max(jnp.maximum(rise[0], rise[1])) <= MAX_RISE

    n_groups = lax.div(i + (group - 1), group)
    _, acc_a, _, acc_b = lax.cond(
        fixed_ok,
        lambda: lax.fori_loop(0, n_groups, fixed_body, tuple(state)),
        lambda: lax.fori_loop(0, n_groups, exact_body, tuple(state)))
    num = jnp.where(head_a, acc_a, acc_b)
    sq = num * num
    ms_a = jnp.sum(jnp.where(head_a, sq, 0.0), axis=1, keepdims=True) * (1.0 / ATT_HEAD_DIM)
    ms_b = jnp.sum(jnp.where(head_a, 0.0, sq), axis=1, keepdims=True) * (1.0 / ATT_HEAD_DIM)
    den_a = acc_a[:, ATT_HEAD_DIM:ATT_HEAD_DIM + 1]
    den_b = acc_b[:, 0:1]
    scale = jnp.where(head_a, lax.rsqrt(ms_a + RMS_EPS * den_a * den_a),
                      lax.rsqrt(ms_b + RMS_EPS * den_b * den_b))
    o_ref[...] = (num * scale * gain_ref[...]).astype(BF16)


def _moba_attention(q, k, v, kmean, gain, B, S, to_bf16=(), zeros=()):
    T, W = q.shape
    nq = S // MOBA_BLOCK
    npairs = W // LANES
    group = 4 if nq % 4 == 0 else 1
    steps = B * npairs * nq
    step = lambda b, h, i: ((b * npairs + h) * nq + i, 0)
    ride = lambda a: pl.BlockSpec((a.shape[0] // steps, a.shape[1]), step)
    outs = pl.pallas_call(
        functools.partial(_attn_kernel, group=group, n_cast=len(to_bf16), n_zero=len(zeros)),
        grid=(B, npairs, nq),
        in_specs=[pl.BlockSpec((MOBA_BLOCK, LANES), lambda b, h, i: (b * nq + i, h)),
                  pl.BlockSpec((S, LANES), lambda b, h, i: (b, h)),
                  pl.BlockSpec((S, LANES), lambda b, h, i: (b, h)),
                  pl.BlockSpec((None, LANES, LANES), lambda b, h, i: (b, 0, h)),
                  pl.BlockSpec((1, LANES), lambda b, h, i: (0, h))] + [ride(a) for a in to_bf16],
        out_specs=[pl.BlockSpec((MOBA_BLOCK, LANES), lambda b, h, i: (b * nq + i, h))]
        + [ride(a) for a in to_bf16] + [ride(a) for a in zeros],
        out_shape=[jax.ShapeDtypeStruct((T, W), BF16)]
        + [jax.ShapeDtypeStruct(a.shape, BF16) for a in to_bf16] + list(zeros),
        scratch_shapes=[pltpu.VMEM((2 * LANES, S), BF16), pltpu.VMEM((2, S, LANES), BF16),
                        pltpu.VMEM((SUBLANES, LANES), F32)],
        compiler_params=_params("parallel", "parallel", "arbitrary"),
        name="moba_attention",
    )(q, k, v, kmean, gain, *to_bf16)
    n = len(to_bf16)
    return outs[0], outs[1:1 + n], outs[1 + n:]


def _can_ride_attention(a, B, S):
    steps = B * (ATT_WIDTH // LANES) * (S // MOBA_BLOCK)
    return a.shape[0] % (steps * 2 * SUBLANES) == 0


def _mlstm_kernel(mqk_ref, mv_ref, mo_ref, g_ref, cw_ref, cb_ref, gain_ref, o_ref,
                  c_ref, m_ref, tail_ref, kt_ref):
    L = ML_CHUNK
    R = mqk_ref.shape[0]

    @pl.when(pl.program_id(1) == 0)
    def _reset():
        c_ref[...] = jnp.zeros_like(c_ref)
        m_ref[...] = jnp.zeros_like(m_ref)
        tail_ref[...] = jnp.zeros_like(tail_ref)

    cur = mqk_ref[...].astype(F32)
    xp = jnp.concatenate([tail_ref[...], cur], axis=0)
    base = SUBLANES - (CONV_WIDTH - 1)
    y_all = cb_ref[...] + cw_ref[0:1, :] * xp[base:base + R]
    for j in range(1, CONV_WIDTH):
        y_all = y_all + cw_ref[j:j + 1, :] * xp[base + j:base + j + R]
    tail_ref[...] = cur[R - SUBLANES:]
    y_all = y_all * _sigmoid(y_all)
    for c in range(R // L):
        _mlstm_chunk(y_all[c * L:(c + 1) * L], slice(c * L, (c + 1) * L), mv_ref, mo_ref, g_ref,
                     gain_ref, o_ref, c_ref, m_ref, kt_ref)


def _mlstm_chunk(y, rows, mv_ref, mo_ref, g_ref, gain_ref, o_ref, c_ref, m_ref, kt_ref):
    L = ML_CHUNK

    lane = _iota((L, LANES), 1)
    row = _iota((L, LANES), 0)
    tri = lane <= row
    g8 = g_ref[rows, :].T[:SUBLANES]
    grow = _iota((SUBLANES, L), 0)
    logf = -(jnp.maximum(-g8, 0.0) + jnp.log1p(jnp.exp(-jnp.abs(g8))))
    upper = (row <= lane).astype(F32)
    bcum = jnp.dot(jnp.where(grow >= ML_HEADS, logf, 0.0), upper, precision=HIGHEST,
                   preferred_element_type=F32)
    rows8 = jnp.where(grow < ML_HEADS, g8, bcum)
    cols = jnp.concatenate([rows8, jnp.zeros((LANES - SUBLANES, L), F32)], axis=0).T

    for p in range(ML_HEADS // 2):
        kt_ref[p] = y[:, ML_QK_WIDTH + p * LANES:ML_QK_WIDTH + (p + 1) * LANES].T
    ones_col = jnp.where(lane == 0, 1.0, 0.0).astype(BF16)
    for h in range(ML_HEADS):
        pair, odd = divmod(h, 2)
        in_head = (lane >= odd * ML_QK_DIM) & (lane < (odd + 1) * ML_QK_DIM)
        dim_in_head = (row >= odd * ML_QK_DIM) & (row < (odd + 1) * ML_QK_DIM)
        yq = y[:, pair * LANES:(pair + 1) * LANES]
        qh = jnp.where(in_head, yq * (ML_QK_DIM ** -0.5), 0.0).astype(BF16)
        kt = jnp.where(dim_in_head, kt_ref[pair], 0.0).astype(BF16)
        vh = mv_ref[rows, h * ML_V_DIM:(h + 1) * ML_V_DIM]
        vaug = jnp.concatenate([vh, ones_col], axis=1)

        i_c = cols[:, h:h + 1]
        b_c = cols[:, ML_HEADS + h:ML_HEADS + h + 1]
        i_r = rows8[h:h + 1, :]
        b_r = rows8[ML_HEADS + h:ML_HEADS + h + 1, :]
        m_prev = m_ref[h:h + 1, 0:1]

        d_log = jnp.where(tri, b_c - b_r + i_r, NEG_INF)
        inter = b_c + m_prev
        m_t = jnp.maximum(inter, jnp.max(d_log, axis=1, keepdims=True))
        w_intra = jnp.exp(d_log - m_t)
        w_inter = jnp.exp(inter - m_t)

        state = c_ref[h]
        s = jnp.dot(qh, kt, preferred_element_type=F32) * w_intra
        numden = (jnp.dot(s.astype(BF16), vaug, preferred_element_type=F32)
                  + w_inter * jnp.dot(qh, state.astype(BF16), preferred_element_type=F32))
        num = numden[:, :ML_V_DIM]
        den = numden[:, ML_V_DIM:ML_V_DIM + 1]
        hcur = num / jnp.maximum(jnp.abs(den), jnp.exp(-m_t))

        b_last = b_c[L - 1:L, :]
        decay = b_last - b_c + i_c
        m_new = jnp.maximum(b_last + m_prev, jnp.max(decay, axis=0, keepdims=True))
        w_state = jnp.exp(decay - m_new)
        carry_scale = jnp.exp(b_last + m_prev - m_new)
        vw = (vaug.astype(F32) * w_state).astype(BF16)
        c_ref[h] = carry_scale * state + jnp.dot(kt, vw, preferred_element_type=F32)
        m_ref[h:h + 1, :] = jnp.broadcast_to(m_new, (1, LANES))

        mean_sq = jnp.mean(hcur * hcur, axis=1, keepdims=True)
        sl = slice(h * ML_V_DIM, (h + 1) * ML_V_DIM)
        gate = _sigmoid(mo_ref[rows, sl].astype(F32))
        o_ref[rows, sl] = (hcur * lax.rsqrt(mean_sq + RMS_EPS) * gain_ref[:, sl]
                        * gate).astype(BF16)


def _mlstm(mqk, mv, mo, gates, conv_w, conv_b, gain, B, S):
    T, W = mv.shape
    L = ML_CHUNK
    R = L
    nc = S // R
    row = lambda b, c: (b * nc + c, 0)
    const = lambda b, c: (0, 0)
    return pl.pallas_call(
        _mlstm_kernel,
        grid=(B, nc),
        in_specs=[pl.BlockSpec((R, W), row),
                  pl.BlockSpec((R, W), row),
                  pl.BlockSpec((R, W), row),
                  pl.BlockSpec((R, LANES), row),
                  pl.BlockSpec(conv_w.shape, const),
                  pl.BlockSpec((1, W), const),
                  pl.BlockSpec((1, W), const)],
        out_specs=pl.BlockSpec((R, W), row),
        out_shape=jax.ShapeDtypeStruct((T, W), BF16),
        scratch_shapes=[pltpu.VMEM((ML_HEADS, LANES, 2 * ML_V_DIM), F32),
                        pltpu.VMEM((SUBLANES, LANES), F32),
                        pltpu.VMEM((SUBLANES, W), F32),
                        pltpu.VMEM((ML_HEADS // 2, LANES, L), F32)],
        compiler_params=_params("parallel", "arbitrary"),
        name="mlstm",
    )(mqk, mv, mo, gates, conv_w, conv_b, gain)


def _outproj_kernel(att_ref, ml_ref, wa_ref, wb_ref, x_ref, g_ref, b_ref, o_ref, *, alpha):
    mix = (jnp.dot(att_ref[...], wa_ref[...], preferred_element_type=F32)
           + jnp.dot(ml_ref[...], wb_ref[...], preferred_element_type=F32))
    o_ref[...] = _layer_norm(alpha * x_ref[...] + mix, g_ref[...], b_ref[...])


def _outproj(att, ml, w_out, x2, ln_g, ln_b, alpha):
    T, D = x2.shape
    W = att.shape[1]
    tm = min(T, 512)
    row = lambda i: (i, 0)
    const = lambda i: (0, 0)
    return pl.pallas_call(
        functools.partial(_outproj_kernel, alpha=alpha),
        grid=(T // tm,),
        in_specs=[pl.BlockSpec((tm, W), row),
                  pl.BlockSpec((tm, W), row),
                  pl.BlockSpec((W, D), lambda i: (0, 0)),
                  pl.BlockSpec((W, D), lambda i: (1, 0)),
                  pl.BlockSpec((tm, D), row),
                  pl.BlockSpec((1, D), const),
                  pl.BlockSpec((1, D), const)],
        out_specs=pl.BlockSpec((tm, D), row),
        out_shape=jax.ShapeDtypeStruct((T, D), F32),
        compiler_params=_params("parallel"),
        name="outproj_ln",
    )(att, ml, w_out, w_out, x2, ln_g, ln_b)


def _swiglu_chunk(xb, wg, wu, wd):
    g = jnp.dot(xb, wg, preferred_element_type=F32)
    u = jnp.dot(xb, wu, preferred_element_type=F32)
    h = (g * _sigmoid(g) * u).astype(BF16)
    return jnp.dot(h, wd, preferred_element_type=F32)


def _dense_ffn_kernel(x_ref, wg_ref, wu_ref, wd_ref, g_ref, b_ref, o_ref, *, alpha, chunk):
    x = x_ref[...]
    xb = x.astype(BF16)
    F = wg_ref.shape[1]
    acc = None
    for a in range(0, F, chunk):
        b = min(a + chunk, F)
        part = _swiglu_chunk(xb, wg_ref[:, a:b], wu_ref[:, a:b], wd_ref[a:b, :])
        acc = part if acc is None else acc + part
    o_ref[...] = _layer_norm(alpha * x + acc, g_ref[...], b_ref[...])


def _hidden_chunk(width, target):
    for unit in (2 * LANES, LANES):
        best = 0
        for c in range(unit, target + 1, unit):
            if width % c == 0:
                best = c
        if best:
            return best
    return width


def _dense_ffn(x2, wg, wu, wd, ln_g, ln_b, alpha):
    T, D = x2.shape
    F = wg.shape[1]
    tm = min(T, 512)
    row = lambda i: (i, 0)
    const = lambda i: (0, 0)
    resident = dict(pipeline_mode=pl.Buffered(1))
    return pl.pallas_call(
        functools.partial(_dense_ffn_kernel, alpha=alpha, chunk=4 * LANES),
        grid=(T // tm,),
        in_specs=[pl.BlockSpec((tm, D), row),
                  pl.BlockSpec((D, F), const, **resident),
                  pl.BlockSpec((D, F), const, **resident),
                  pl.BlockSpec((F, D), const, **resident),
                  pl.BlockSpec((1, D), const),
                  pl.BlockSpec((1, D), const)],
        out_specs=pl.BlockSpec((tm, D), row),
        out_shape=jax.ShapeDtypeStruct((T, D), F32),
        compiler_params=_params("parallel"),
        name="dense_ffn_ln",
    )(x2, wg, wu, wd, ln_g, ln_b)


def _router_kernel(x_ref, rw_ref, rb_ref, ids_ref, wts_ref, cnt_ref, carry_ref, *, n_experts):
    @pl.when(pl.program_id(0) == 0)
    def _reset():
        carry_ref[...] = jnp.zeros_like(carry_ref)

    tm = x_ref.shape[0]
    x_hi, x_lo = _split_bf16(x_ref[...], 2)
    w_hi, w_lo = _split_bf16(rw_ref[...], 2)
    logits = (jnp.dot(x_hi, w_hi, preferred_element_type=F32)
              + (jnp.dot(x_lo, w_hi, preferred_element_type=F32)
                 + jnp.dot(x_hi, w_lo, preferred_element_type=F32))) + rb_ref[...]
    lane = _iota((tm, LANES), 1)
    logits = jnp.where(lane < n_experts, logits, NEG_INF)

    def top(lg):
        best = jnp.max(lg, axis=1, keepdims=True)
        first = jnp.min(jnp.where(lg == best, lane, LANES), axis=1, keepdims=True)
        return best, first

    v1, e1 = top(logits)
    hot1 = lane == e1
    v2, e2 = top(jnp.where(hot1, NEG_INF, logits))
    hot2 = lane == e2
    ex = jnp.exp(v2 - v1)
    w1 = 1.0 / (1.0 + ex)
    w2 = ex / (1.0 + ex)

    assigned = (hot1 | hot2).astype(BF16)
    before = (_iota((tm, tm), 1) < _iota((tm, tm), 0)).astype(BF16)
    carry = carry_ref[0:1, :]
    rank = jnp.dot(before, assigned, preferred_element_type=F32) + carry
    r1 = jnp.sum(jnp.where(hot1, rank, 0.0), axis=1, keepdims=True).astype(I32)
    r2 = jnp.sum(jnp.where(hot2, rank, 0.0), axis=1, keepdims=True).astype(I32)
    total = carry + jnp.sum(assigned.astype(F32), axis=0, keepdims=True)
    carry_ref[...] = jnp.broadcast_to(total, carry_ref.shape)
    cnt_ref[...] = jnp.broadcast_to(total, cnt_ref.shape).astype(I32)

    ids_ref[...] = jnp.where(lane == 0, e1, jnp.where(lane == 1, e2,
                             jnp.where(lane == 2, r1, jnp.where(lane == 3, r2, 0))))
    wts_ref[...] = jnp.where(lane == 0, w1, jnp.where(lane == 1, w2, 0.0))


def _router(x2, router_w, router_b):
    T, D = x2.shape
    E = router_w.shape[1]
    tm = min(T, 512)
    rw = jnp.pad(router_w, ((0, 0), (0, LANES - E)))
    rb = jnp.pad(router_b, (0, LANES - E))[None, :]
    row = lambda i: (i, 0)
    const = lambda i: (0, 0)
    return pl.pallas_call(
        functools.partial(_router_kernel, n_experts=E),
        grid=(T // tm,),
        in_specs=[pl.BlockSpec((tm, D), row),
                  pl.BlockSpec((D, LANES), const),
                  pl.BlockSpec((1, LANES), const)],
        out_specs=[pl.BlockSpec((tm, LANES), row),
                   pl.BlockSpec((tm, LANES), row),
                   pl.BlockSpec((SUBLANES, LANES), const)],
        out_shape=[jax.ShapeDtypeStruct((T, LANES), I32),
                   jax.ShapeDtypeStruct((T, LANES), F32),
                   jax.ShapeDtypeStruct((SUBLANES, LANES), I32)],
        scratch_shapes=[pltpu.VMEM((SUBLANES, LANES), F32)],
        compiler_params=_params("arbitrary"),
        name="router",
    )(x2, rw, rb)


def _slots_kernel(ids_ref, tiles_ref, out_ref, *, tile_rows):
    ids = ids_ref[...]
    lane = _iota(ids.shape, 1)
    tiles = tiles_ref[...]

    def slot(k):
        pick = jnp.where(lane == ids[:, k:k + 1], 1.0, 0.0).astype(BF16)
        start = jnp.dot(pick, tiles, preferred_element_type=F32)
        return start.astype(I32) * tile_rows + ids[:, TOP_K + k:TOP_K + k + 1]

    out_ref[...] = jnp.where(lane == 0, slot(0), jnp.where(lane == 1, slot(1), 0))


def _row_slots(ids, offs, tm, tile_rows, n_tiles):
    T = ids.shape[0]
    E = offs.shape[0]
    ts = min(T, 2048)
    assert n_tiles <= 256, "start tiles must stay exact in bf16"
    start_tile = jnp.pad(offs // tile_rows, (0, LANES - E)).astype(BF16)
    tiles = jnp.broadcast_to(start_tile[:, None], (LANES, LANES))
    slots = pl.pallas_call(
        functools.partial(_slots_kernel, tile_rows=tile_rows),
        grid=(T // ts,),
        in_specs=[pl.BlockSpec((ts, LANES), lambda i: (i, 0)),
                  pl.BlockSpec((LANES, LANES), lambda i: (0, 0))],
        out_specs=pl.BlockSpec((ts, LANES), lambda i: (i, 0)),
        out_shape=jax.ShapeDtypeStruct((T, LANES), I32),
        compiler_params=_params("parallel"),
        name="moe_slots",
    )(ids, tiles)
    return slots[:, :TOP_K].reshape(T // tm, tm, TOP_K).transpose(0, 2, 1)


def _for_each_row(tm, fn):
    def group(t, c):
        base = pl.multiple_of(t * DMA_UNROLL, DMA_UNROLL)
        for u in range(DMA_UNROLL):
            for k in range(TOP_K):
                fn(base + u, k)
        return c

    lax.fori_loop(0, tm // DMA_UNROLL, group, 0)


def _scatter_kernel(slot_ref, x_ref, xg_in_ref, xg_ref, sem):
    del xg_in_ref
    tm = x_ref.shape[0]

    def copy(r, k):
        return pltpu.make_async_copy(x_ref.at[pl.ds(r, 1)],
                                     xg_ref.at[pl.ds(slot_ref[0, k, r], 1)], sem)

    _for_each_row(tm, lambda r, k: copy(r, k).start())
    _for_each_row(tm, lambda r, k: copy(r, k).wait())


def _scatter_rows(x2, slots, xg0):
    T, D = x2.shape
    nt, _, tm = slots.shape
    n_rows = xg0.shape[0]
    return pl.pallas_call(
        _scatter_kernel,
        grid=(nt,),
        in_specs=[pl.BlockSpec((1, TOP_K, tm), lambda i: (i, 0, 0), memory_space=pltpu.SMEM),
                  pl.BlockSpec((tm, D), lambda i: (i, 0)),
                  pl.BlockSpec(memory_space=pl.ANY)],
        out_specs=pl.BlockSpec(memory_space=pl.ANY),
        out_shape=jax.ShapeDtypeStruct((n_rows, D), F32),
        scratch_shapes=[pltpu.SemaphoreType.DMA(())],
        input_output_aliases={2: 0},
        compiler_params=_params("arbitrary"),
        name="moe_scatter",
    )(slots, x2, xg0)


def _expert_kernel(te_ref, na_ref, x_ref, wg_ref, wu_ref, wd_ref, o_ref, xb_ref, acc_ref):
    g = pl.program_id(0)
    f = pl.program_id(1)
    active = g < na_ref[0]

    @pl.when(active & (f == 0))
    def _start():
        xb_ref[...] = x_ref[...].astype(BF16)
        acc_ref[...] = jnp.zeros_like(acc_ref)

    @pl.when(active)
    def _accumulate():
        acc_ref[...] += _swiglu_chunk(xb_ref[...], wg_ref[...], wu_ref[...], wd_ref[...])

    last = f == pl.num_programs(1) - 1

    @pl.when(active & last)
    def _finish():
        o_ref[...] = acc_ref[...]

    @pl.when(jnp.logical_not(active) & last)
    def _unused_tile():
        o_ref[...] = jnp.zeros_like(o_ref)


def _expert_ffn(xg, wg, wu, wd, tile_expert, n_active, tm):
    P, D = xg.shape
    E, _, F = wg.shape
    tf = _hidden_chunk(F, 1792)
    nf = F // tf

    def tile(g, te, na):
        return jnp.maximum(jnp.minimum(g, na[0] - 1), 0)

    def chunk(g, f, na):
        return jnp.where(g < na[0], f, nf - 1)

    return pl.pallas_call(
        _expert_kernel,
        grid_spec=pltpu.PrefetchScalarGridSpec(
            num_scalar_prefetch=2,
            grid=(P // tm, nf),
            in_specs=[pl.BlockSpec((tm, D), lambda g, f, te, na: (tile(g, te, na), 0)),
                      pl.BlockSpec((None, D, tf),
                                   lambda g, f, te, na: (te[tile(g, te, na)], 0, chunk(g, f, na))),
                      pl.BlockSpec((None, D, tf),
                                   lambda g, f, te, na: (te[tile(g, te, na)], 0, chunk(g, f, na))),
                      pl.BlockSpec((None, tf, D),
                                   lambda g, f, te, na: (te[tile(g, te, na)], chunk(g, f, na), 0))],
            out_specs=pl.BlockSpec((tm, D), lambda g, f, te, na: (g, 0)),
            scratch_shapes=[pltpu.VMEM((tm, D), BF16), pltpu.VMEM((tm, D), F32)]),
        out_shape=jax.ShapeDtypeStruct((P, D), F32),
        compiler_params=_params("arbitrary", "arbitrary"),
        name="moe_experts",
    )(tile_expert, n_active, xg, wg, wu, wd)


def _combine_kernel(slot_ref, next_ref, yg_ref, wts_ref, x_ref, g_ref, b_ref, o_ref,
                    buf_ref, sem, *, alpha):
    tm = x_ref.shape[0]
    i = pl.program_id(0)
    cur = lax.rem(i, 2)

    def copy(table, buf, r, k):
        return pltpu.make_async_copy(yg_ref.at[pl.ds(table[0, k, r], 1)],
                                     buf_ref.at[buf, k, pl.ds(r, 1)], sem.at[buf])

    @pl.when(i == 0)
    def _first_tile():
        _for_each_row(tm, lambda r, k: copy(slot_ref, 0, r, k).start())

    @pl.when(i + 1 < pl.num_programs(0))
    def _next_tile():
        _for_each_row(tm, lambda r, k: copy(next_ref, 1 - cur, r, k).start())

    _for_each_row(tm, lambda r, k: copy(slot_ref, cur, r, k).wait())
    w = wts_ref[...]
    ffn = w[:, 0:1] * buf_ref[cur, 0] + w[:, 1:2] * buf_ref[cur, 1]
    o_ref[...] = _layer_norm(alpha * x_ref[...] + ffn, g_ref[...], b_ref[...])


def _combine(yg, slots, wts, x2, ln_g, ln_b, alpha):
    T, D = x2.shape
    nt, _, tm = slots.shape
    row = lambda i: (i, 0)
    const = lambda i: (0, 0)
    smem = dict(memory_space=pltpu.SMEM)
    return pl.pallas_call(
        functools.partial(_combine_kernel, alpha=alpha),
        grid=(nt,),
        in_specs=[pl.BlockSpec((1, TOP_K, tm), lambda i: (i, 0, 0), **smem),
                  pl.BlockSpec((1, TOP_K, tm), lambda i: (jnp.minimum(i + 1, nt - 1), 0, 0), **smem),
                  pl.BlockSpec(memory_space=pl.ANY),
                  pl.BlockSpec((tm, LANES), row),
                  pl.BlockSpec((tm, D), row),
                  pl.BlockSpec((1, D), const),
                  pl.BlockSpec((1, D), const)],
        out_specs=pl.BlockSpec((tm, D), row),
        out_shape=jax.ShapeDtypeStruct((T, D), F32),
        scratch_shapes=[pltpu.VMEM((2, TOP_K, tm, D), F32), pltpu.SemaphoreType.DMA((2,))],
        compiler_params=_params("arbitrary"),
        name="moe_combine_ln",
    )(slots, slots, yg, wts, x2, ln_g, ln_b)


def _moe_tile(T):
    return min(T, 512)


def _moe_rows(T, E):
    return TOP_K * T + E * _moe_tile(T)


def _moe_ffn(x2, router_w, router_b, wg, wu, wd, ln_g, ln_b, alpha, xg0):
    T, D = x2.shape
    E = router_w.shape[1]
    tm_e = _moe_tile(T)
    tm_r = min(T, 256)
    ids, wts, counts = _router(x2, router_w, router_b)

    cnt = counts[0, :E]
    padded = ((cnt + tm_e - 1) // tm_e) * tm_e
    ends = jnp.cumsum(padded)
    offs = (ends - padded).astype(I32)
    n_rows = xg0.shape[0]
    n_tiles = n_rows // tm_e
    tile_start = jnp.arange(n_tiles, dtype=I32) * tm_e
    tile_expert = jnp.minimum(jnp.sum(tile_start[:, None] >= ends[None, :], axis=1), E - 1).astype(I32)
    n_active = (ends[-1:] // tm_e).astype(I32)

    slots = _row_slots(ids, offs, tm_r, tm_e, n_tiles)
    xg = _scatter_rows(x2, slots, xg0)
    yg = _expert_ffn(xg, wg, wu, wd, tile_expert, n_active, tm_e)
    return _combine(yg, slots, wts, x2, ln_g, ln_b, alpha)


def kernel(x, positions, w_in, gate_bias, conv_w, conv_b, norm_attn, norm_mlstm, w_out,
           ln1_g, ln1_b, dense_w_gate, dense_w_up, dense_w_down, router_w, router_b,
           moe_w_gate, moe_w_up, moe_w_down, ln2_g, ln2_b):
    B, S, D = x.shape
    depth = w_in.shape[0]
    T = B * S
    alpha = (2.0 * depth) ** 0.25
    n_main = 3 * ATT_WIDTH + 2 * ML_QK_WIDTH + 2 * ML_WIDTH
    nblk = S // MOBA_BLOCK

    tabs = _rope_tables(positions)
    x2 = x.reshape(T, D)
    moe_bf16 = None
    for l in range(depth):
        w_main = w_in[l, :, :n_main].astype(BF16)
        w_gate = jnp.pad(w_in[l, :, n_main:], ((0, 0), (0, LANES - 2 * ML_HEADS))).astype(BF16)
        gbias = jnp.pad(gate_bias[l], (0, LANES - 2 * ML_HEADS))[None, :]
        q, k, v, mqk, mv, mo, gates, kmean = _inproj(x2, w_main, w_gate, gbias, tabs)
        kmean = jnp.pad(kmean.reshape(B, nblk, ATT_WIDTH), ((0, 0), (0, LANES - nblk), (0, 0)))
        ride = []
        if l % 2 == 0 and l + 1 < depth:
            ride = [w[l // 2].reshape(-1, w.shape[-1]) for w in (moe_w_gate, moe_w_up, moe_w_down)]
            if not all(_can_ride_attention(a, B, S) for a in ride):
                ride = []
        clear = []
        if l % 2 == 1:
            clear = [jax.ShapeDtypeStruct((_moe_rows(T, router_w.shape[-1]), D), F32)]
            if not _can_ride_attention(clear[0], B, S):
                clear = []
        att, rode, cleared = _moba_attention(q, k, v, kmean, norm_attn[l][None, :], B, S,
                                             ride, clear)
        if ride:
            moe_bf16 = [a.reshape(w.shape[1:]) for a, w in zip(rode, (moe_w_gate, moe_w_up, moe_w_down))]
        ml = _mlstm(mqk, mv, mo, gates, conv_w[l], conv_b[l][None, :],
                    norm_mlstm[l][None, :], B, S)
        x2 = _outproj(att, ml, w_out[l].astype(BF16), x2, ln1_g[l][None, :], ln1_b[l][None, :],
                      alpha)
        j = l // 2
        if l % 2 == 0:
            x2 = _dense_ffn(x2, dense_w_gate[j].astype(BF16), dense_w_up[j].astype(BF16),
                            dense_w_down[j].astype(BF16), ln2_g[l][None, :], ln2_b[l][None, :],
                            alpha)
        else:
            if moe_bf16 is None:
                moe_bf16 = [w[j].astype(BF16) for w in (moe_w_gate, moe_w_up, moe_w_down)]
            xg0 = cleared[0] if cleared else jnp.zeros((_moe_rows(T, router_w.shape[-1]), D), F32)
            x2 = _moe_ffn(x2, router_w[j], router_b[j], *moe_bf16,
                          ln2_g[l][None, :], ln2_b[l][None, :], alpha, xg0)
            moe_bf16 = None
    return x2.reshape(B, S, D)
```

```python
import functools

import jax
import jax.numpy as jnp
from jax import lax
from jax.experimental import pallas as pl
from jax.experimental.pallas import tpu as pltpu

F32 = jnp.float32
BF16 = jnp.bfloat16
I32 = jnp.int32

ATT_HEADS = 8
ATT_HEAD_DIM = 64
ATT_WIDTH = ATT_HEADS * ATT_HEAD_DIM
MOBA_BLOCK = 256
MOBA_TOPK = 3
ROPE_THETA = 500000.0
ROPE_DIM = ATT_HEAD_DIM // 4
ML_HEADS = 4
ML_V_DIM = 128
ML_QK_DIM = 64
ML_QK_WIDTH = ML_HEADS * ML_QK_DIM
ML_WIDTH = ML_HEADS * ML_V_DIM
ML_CHUNK = 128
CONV_WIDTH = 4
TOP_K = 2
LN_EPS = 1e-5
RMS_EPS = 1e-6

LANES = 128
SUBLANES = 8
V7X_MXU_WIDTH = 256
V7X_VMEM_BYTES = 64 * 1024 * 1024
VMEM_LIMIT = (V7X_VMEM_BYTES * 3) // 4

ROW_TILE = 512
ROPE_ROW_TILE = 1024
SLOT_ROW_TILE = 2048
DMA_ROW_TILE = 256
EXPERT_ROW_TILE = 512
DENSE_HIDDEN_CHUNK = 2 * V7X_MXU_WIDTH
EXPERT_HIDDEN_CHUNK = 7 * V7X_MXU_WIDTH

MASK_BIAS = -1e9
HEAD_B_BIAS = 64
MAX_RISE = 60.0
NEG_INF = float("-inf")
HIGHEST = lax.Precision.HIGHEST
LOG2_E = 1.4426950408889634
DMA_UNROLL = 8

_NT = (((1,), (1,)), ((), ()))


def _params(*sem):
    return pltpu.CompilerParams(dimension_semantics=sem, vmem_limit_bytes=VMEM_LIMIT)


def _iota(shape, dim):
    return lax.broadcasted_iota(I32, shape, dim)


def _sigmoid(x):
    return 1.0 / (1.0 + jnp.exp(-x))


def _layer_norm(y, g, b):
    mu = jnp.mean(y, axis=-1, keepdims=True)
    yc = y - mu
    var = jnp.mean(yc * yc, axis=-1, keepdims=True)
    return yc * lax.rsqrt(var + LN_EPS) * g + b


def _rope_tab_kernel(pos_ref, freq_ref, cos_ref, s1_ref, s2_ref):
    ang = pos_ref[...].astype(F32) * freq_ref[...]
    d = _iota(ang.shape, 1) & (ATT_HEAD_DIM - 1)
    half = ROPE_DIM // 2
    s = jnp.sin(ang)
    cos_ref[...] = jnp.cos(ang)
    s1_ref[...] = jnp.where(d < half, -s, 0.0)
    s2_ref[...] = jnp.where((d >= half) & (d < ROPE_DIM), s, 0.0)


def _rope_tables(positions):
    T = positions.size
    tm = min(T, ROPE_ROW_TILE)
    half = ROPE_DIM // 2
    inv_freq = ROPE_THETA ** (-jnp.arange(half, dtype=F32) / half)
    d = jnp.arange(LANES) % ATT_HEAD_DIM
    freq = jnp.where(d < ROPE_DIM, inv_freq[d % half], 0.0).astype(F32)[None, :]
    tab = jax.ShapeDtypeStruct((T, LANES), F32)
    return pl.pallas_call(
        _rope_tab_kernel,
        grid=(T // tm,),
        in_specs=[pl.BlockSpec((tm, 1), lambda i: (i, 0)),
                  pl.BlockSpec((1, LANES), lambda i: (0, 0))],
        out_specs=[pl.BlockSpec((tm, LANES), lambda i: (i, 0))] * 3,
        out_shape=[tab, tab, tab],
        compiler_params=_params("parallel"),
        name="rope_tables",
    )(positions.reshape(T, 1), freq)


def _inproj_kernel(x_ref, w_ref, wg_ref, gb_ref, cos_ref, s1_ref, s2_ref,
                   q_ref, k_ref, v_ref, mqk_ref, mv_ref, mo_ref, g_ref, km_ref):
    xb = x_ref[...].astype(BF16)
    W = ATT_WIDTH

    def proj(c):
        return jnp.dot(xb, w_ref[:, c * W:(c + 1) * W], preferred_element_type=F32)

    rep = W // LANES
    cos = jnp.concatenate([cos_ref[...]] * rep, axis=1)
    s1 = jnp.concatenate([s1_ref[...]] * rep, axis=1)
    s2 = jnp.concatenate([s2_ref[...]] * rep, axis=1)
    half = ROPE_DIM // 2

    def rope(t):
        return t * cos + pltpu.roll(t, W - half, 1) * s1 + pltpu.roll(t, half, 1) * s2

    q_ref[...] = (rope(proj(0)) * (LOG2_E * ATT_HEAD_DIM ** -0.5)).astype(BF16)
    k = rope(proj(1))
    k_ref[...] = k.astype(BF16)
    for g in range(k.shape[0] // MOBA_BLOCK):
        blk = k[g * MOBA_BLOCK:(g + 1) * MOBA_BLOCK]
        km_ref[0, g:g + 1, :] = jnp.sum(blk, axis=0, keepdims=True) * (1.0 / MOBA_BLOCK)
    v_ref[...] = proj(2).astype(BF16)
    mqk_ref[...] = proj(3).astype(BF16)
    mv_ref[...] = proj(4).astype(BF16)
    mo_ref[...] = proj(5).astype(BF16)
    g_ref[...] = jnp.dot(xb, wg_ref[...], preferred_element_type=F32) + gb_ref[...]


def _inproj(x2, w_main, w_gate, gate_bias, tabs):
    T, D = x2.shape
    tm = min(T, ROW_TILE)
    nkb = tm // MOBA_BLOCK
    W = ATT_WIDTH
    row = lambda i: (i, 0)
    const = lambda i: (0, 0)
    act = jax.ShapeDtypeStruct((T, W), BF16)
    return pl.pallas_call(
        _inproj_kernel,
        grid=(T // tm,),
        in_specs=[pl.BlockSpec((tm, D), row),
                  pl.BlockSpec(w_main.shape, const),
                  pl.BlockSpec(w_gate.shape, const),
                  pl.BlockSpec((1, LANES), const),
                  pl.BlockSpec((tm, LANES), row),
                  pl.BlockSpec((tm, LANES), row),
                  pl.BlockSpec((tm, LANES), row)],
        out_specs=[pl.BlockSpec((tm, W), row)] * 6
        + [pl.BlockSpec((tm, LANES), row),
           pl.BlockSpec((1, nkb, W), lambda i: (i, 0, 0))],
        out_shape=[act] * 6 + [jax.ShapeDtypeStruct((T, LANES), F32),
                               jax.ShapeDtypeStruct((T // tm, nkb, W), F32)],
        compiler_params=_params("parallel"),
        name="inproj",
    )(x2, w_main, w_gate, gate_bias, *tabs)


def _split_bf16(x, parts):
    out = []
    for _ in range(parts):
        hi = x.astype(BF16)
        out.append(hi)
        x = x - hi.astype(F32)
    return out


def _attn_kernel(q_ref, k_ref, v_ref, km_ref, gain_ref, *rest, group, n_cast, n_zero):
    cast_in, rest = rest[:n_cast], rest[n_cast:]
    o_ref, rest = rest[0], rest[1:]
    cast_out, rest = rest[:n_cast], rest[n_cast:]
    zero_out, (kaug_ref, vaug_ref, kabs_ref) = rest[:n_zero], rest[n_zero:]
    for src_ref, dst_ref in zip(cast_in, cast_out):
        dst_ref[...] = src_ref[...].astype(BF16)
    for dst_ref in zero_out:
        dst_ref[...] = jnp.zeros_like(dst_ref)
    i = pl.program_id(2)
    tq = q_ref.shape[0]
    nblk = k_ref.shape[0] // MOBA_BLOCK
    lane = _iota((tq, LANES), 1)
    head_a = lane < ATT_HEAD_DIM

    @pl.when(i == 0)
    def _build_augmented_keys_values():
        blane = _iota((MOBA_BLOCK, LANES), 1)
        brow_k = _iota((LANES, MOBA_BLOCK), 0)
        in_a = blane < ATT_HEAD_DIM
        one = jnp.ones((MOBA_BLOCK, LANES), BF16)

        def fill(j, carry):
            rows = pl.ds(pl.multiple_of(j * MOBA_BLOCK, MOBA_BLOCK), MOBA_BLOCK)
            kaug_ref[:LANES, rows] = k_ref[rows, :].astype(F32).T.astype(BF16)
            kaug_ref[LANES:, rows] = jnp.where((brow_k == j) | (brow_k == j + HEAD_B_BIAS), 1.0,
                                               0.0).astype(BF16)
            vj = v_ref[rows, :]
            vaug_ref[0, rows, :] = jnp.where(in_a, vj, one)
            vaug_ref[1, rows, :] = jnp.where(in_a, one, vj)
            kj = jnp.abs(k_ref[rows, :].astype(F32))
            return jnp.maximum(carry, jnp.max(kj, axis=0, keepdims=True))

        kabs = lax.fori_loop(0, nblk, fill, jnp.zeros((1, LANES), F32))
        kabs_ref[...] = jnp.broadcast_to(kabs, kabs_ref.shape)

    q = q_ref[...]
    zero = jnp.zeros_like(q)
    nrow = -(-nblk // SUBLANES) * SUBLANES
    km_stack = jnp.concatenate(_split_bf16(km_ref[:nrow, :], 3), axis=0)
    brow = _iota((nrow, tq), 0)

    def select_bias(qh):
        parts = lax.dot_general(km_stack, qh, _NT, preferred_element_type=F32)
        gate = parts[:nrow] + parts[nrow:2 * nrow] + parts[2 * nrow:]
        gate = jnp.where(brow < i, gate, NEG_INF)
        picked = brow < 0
        for _ in range(MOBA_TOPK):
            best = jnp.max(gate, axis=0, keepdims=True)
            first = jnp.min(jnp.where(gate == best, brow, nrow), axis=0, keepdims=True)
            hit = (brow == first) & (best > NEG_INF)
            picked = picked | hit
            gate = jnp.where(hit, NEG_INF, gate)
        bias = jnp.where(picked, 0.0, MASK_BIAS)
        if nrow < HEAD_B_BIAS:
            bias = jnp.concatenate([bias, jnp.zeros((HEAD_B_BIAS - nrow, tq), F32)], axis=0)
        return bias

    qh = (jnp.where(head_a, q, zero), jnp.where(head_a, zero, q))
    bias = jnp.concatenate([select_bias(qh[0]), select_bias(qh[1])], axis=0).T.astype(BF16)
    zero_b = jnp.zeros_like(bias)
    qa = [jnp.concatenate([qh[0], jnp.where(head_a, bias, zero_b)], axis=1),
          jnp.concatenate([qh[1], jnp.where(head_a, zero_b, bias)], axis=1)]

    own = pl.ds(pl.multiple_of(i * MOBA_BLOCK, MOBA_BLOCK), MOBA_BLOCK)
    kd = kaug_ref[:LANES, own]
    causal = _iota((tq, MOBA_BLOCK), 1) <= _iota((tq, MOBA_BLOCK), 0)
    state = []
    for h in range(2):
        s = jnp.where(causal, jnp.dot(qh[h], kd, preferred_element_type=F32), NEG_INF)
        m = jnp.max(s, axis=1, keepdims=True)
        p = jnp.exp2(s - m).astype(BF16)
        state += [m, jnp.dot(p, vaug_ref[h, own, :], preferred_element_type=F32)]

    span = group * MOBA_BLOCK

    def exact_body(g, carry):
        rows = pl.ds(pl.multiple_of(g * span, span), span)
        kg = kaug_ref[:, rows]
        new = []
        for h in range(2):
            m, acc = carry[2 * h], carry[2 * h + 1]
            s = jnp.dot(qa[h], kg, preferred_element_type=F32)
            m_new = jnp.maximum(m, jnp.max(s, axis=1, keepdims=True))
            p = jnp.exp2(s - m_new).astype(BF16)
            acc = jnp.exp2(m - m_new) * acc + jnp.dot(p, vaug_ref[h, rows, :],
                                                     preferred_element_type=F32)
            new += [m_new, acc]
        return tuple(new)

    def fixed_body(g, carry):
        rows = pl.ds(pl.multiple_of(g * span, span), span)
        kg = kaug_ref[:, rows]
        new = []
        for h in range(2):
            m, acc = carry[2 * h], carry[2 * h + 1]
            s = jnp.dot(qa[h], kg, preferred_element_type=F32)
            p = jnp.exp2(s - m).astype(BF16)
            new += [m, acc + jnp.dot(p, vaug_ref[h, rows, :], preferred_element_type=F32)]
        return tuple(new)

    kabs = kabs_ref[0:1, :]
    rise = [jnp.sum(jnp.abs(qh[h].astype(F32)) * kabs, axis=1, keepdims=True) - state[2 * h]
            for h in range(2)]
    fixed_ok = jnp.max(jnp.maximum(rise[0], rise[1])) <= MAX_RISE

    n_groups = lax.div(i + (group - 1), group)
    _, acc_a, _, acc_b = lax.cond(
        fixed_ok,
        lambda: lax.fori_loop(0, n_groups, fixed_body, tuple(state)),
        lambda: lax.fori_loop(0, n_groups, exact_body, tuple(state)))
    num = jnp.where(head_a, acc_a, acc_b)
    sq = num * num
    ms_a = jnp.sum(jnp.where(head_a, sq, 0.0), axis=1, keepdims=True) * (1.0 / ATT_HEAD_DIM)
    ms_b = jnp.sum(jnp.where(head_a, 0.0, sq), axis=1, keepdims=True) * (1.0 / ATT_HEAD_DIM)
    den_a = acc_a[:, ATT_HEAD_DIM:ATT_HEAD_DIM + 1]
    den_b = acc_b[:, 0:1]
    scale = jnp.where(head_a, lax.rsqrt(ms_a + RMS_EPS * den_a * den_a),
                      lax.rsqrt(ms_b + RMS_EPS * den_b * den_b))
    o_ref[...] = (num * scale * gain_ref[...]).astype(BF16)


def _moba_attention(q, k, v, kmean, gain, B, S, to_bf16=(), zeros=()):
    T, W = q.shape
    nq = S // MOBA_BLOCK
    npairs = W // LANES
    group = 4 if nq % 4 == 0 else 1
    assert nq <= HEAD_B_BIAS, "one 128-lane group holds both heads' block biases"
    steps = B * npairs * nq
    step = lambda b, h, i: ((b * npairs + h) * nq + i, 0)
    ride = lambda a: pl.BlockSpec((a.shape[0] // steps, a.shape[1]), step)
    outs = pl.pallas_call(
        functools.partial(_attn_kernel, group=group, n_cast=len(to_bf16), n_zero=len(zeros)),
        grid=(B, npairs, nq),
        in_specs=[pl.BlockSpec((MOBA_BLOCK, LANES), lambda b, h, i: (b * nq + i, h)),
                  pl.BlockSpec((S, LANES), lambda b, h, i: (b, h)),
                  pl.BlockSpec((S, LANES), lambda b, h, i: (b, h)),
                  pl.BlockSpec((None, LANES, LANES), lambda b, h, i: (b, 0, h)),
                  pl.BlockSpec((1, LANES), lambda b, h, i: (0, h))] + [ride(a) for a in to_bf16],
        out_specs=[pl.BlockSpec((MOBA_BLOCK, LANES), lambda b, h, i: (b * nq + i, h))]
        + [ride(a) for a in to_bf16] + [ride(a) for a in zeros],
        out_shape=[jax.ShapeDtypeStruct((T, W), BF16)]
        + [jax.ShapeDtypeStruct(a.shape, BF16) for a in to_bf16] + list(zeros),
        scratch_shapes=[pltpu.VMEM((2 * LANES, S), BF16), pltpu.VMEM((2, S, LANES), BF16),
                        pltpu.VMEM((SUBLANES, LANES), F32)],
        compiler_params=_params("parallel", "parallel", "arbitrary"),
        name="moba_attention",
    )(q, k, v, kmean, gain, *to_bf16)
    n = len(to_bf16)
    return outs[0], outs[1:1 + n], outs[1 + n:]


def _can_ride_attention(a, B, S):
    steps = B * (ATT_WIDTH // LANES) * (S // MOBA_BLOCK)
    return a.shape[0] % (steps * 2 * SUBLANES) == 0


def _mlstm_kernel(mqk_ref, mv_ref, mo_ref, g_ref, cw_ref, cb_ref, gain_ref, o_ref,
                  c_ref, m_ref, tail_ref, kt_ref):
    L = ML_CHUNK
    R = mqk_ref.shape[0]

    @pl.when(pl.program_id(1) == 0)
    def _reset():
        c_ref[...] = jnp.zeros_like(c_ref)
        m_ref[...] = jnp.zeros_like(m_ref)
        tail_ref[...] = jnp.zeros_like(tail_ref)

    cur = mqk_ref[...].astype(F32)
    xp = jnp.concatenate([tail_ref[...], cur], axis=0)
    base = SUBLANES - (CONV_WIDTH - 1)
    y_all = cb_ref[...] + cw_ref[0:1, :] * xp[base:base + R]
    for j in range(1, CONV_WIDTH):
        y_all = y_all + cw_ref[j:j + 1, :] * xp[base + j:base + j + R]
    tail_ref[...] = cur[R - SUBLANES:]
    y_all = y_all * _sigmoid(y_all)
    for c in range(R // L):
        _mlstm_chunk(y_all[c * L:(c + 1) * L], slice(c * L, (c + 1) * L), mv_ref, mo_ref, g_ref,
                     gain_ref, o_ref, c_ref, m_ref, kt_ref)


def _mlstm_chunk(y, rows, mv_ref, mo_ref, g_ref, gain_ref, o_ref, c_ref, m_ref, kt_ref):
    L = ML_CHUNK

    lane = _iota((L, LANES), 1)
    row = _iota((L, LANES), 0)
    tri = lane <= row
    g8 = g_ref[rows, :].T[:SUBLANES]
    grow = _iota((SUBLANES, L), 0)
    logf = -(jnp.maximum(-g8, 0.0) + jnp.log1p(jnp.exp(-jnp.abs(g8))))
    upper = (row <= lane).astype(F32)
    bcum = jnp.dot(jnp.where(grow >= ML_HEADS, logf, 0.0), upper, precision=HIGHEST,
                   preferred_element_type=F32)
    rows8 = jnp.where(grow < ML_HEADS, g8, bcum)
    cols = jnp.concatenate([rows8, jnp.zeros((LANES - SUBLANES, L), F32)], axis=0).T

    for p in range(ML_HEADS // 2):
        kt_ref[p] = y[:, ML_QK_WIDTH + p * LANES:ML_QK_WIDTH + (p + 1) * LANES].T
    ones_col = jnp.where(lane == 0, 1.0, 0.0).astype(BF16)
    for h in range(ML_HEADS):
        pair, odd = divmod(h, 2)
        in_head = (lane >= odd * ML_QK_DIM) & (lane < (odd + 1) * ML_QK_DIM)
        dim_in_head = (row >= odd * ML_QK_DIM) & (row < (odd + 1) * ML_QK_DIM)
        yq = y[:, pair * LANES:(pair + 1) * LANES]
        qh = jnp.where(in_head, yq * (ML_QK_DIM ** -0.5), 0.0).astype(BF16)
        kt = jnp.where(dim_in_head, kt_ref[pair], 0.0).astype(BF16)
        vh = mv_ref[rows, h * ML_V_DIM:(h + 1) * ML_V_DIM]
        vaug = jnp.concatenate([vh, ones_col], axis=1)

        i_c = cols[:, h:h + 1]
        b_c = cols[:, ML_HEADS + h:ML_HEADS + h + 1]
        i_r = rows8[h:h + 1, :]
        b_r = rows8[ML_HEADS + h:ML_HEADS + h + 1, :]
        m_prev = m_ref[h:h + 1, 0:1]

        d_log = jnp.where(tri, b_c - b_r + i_r, NEG_INF)
        inter = b_c + m_prev
        m_t = jnp.maximum(inter, jnp.max(d_log, axis=1, keepdims=True))
        w_intra = jnp.exp(d_log - m_t)
        w_inter = jnp.exp(inter - m_t)

        state = c_ref[h]
        s = jnp.dot(qh, kt, preferred_element_type=F32) * w_intra
        numden = (jnp.dot(s.astype(BF16), vaug, preferred_element_type=F32)
                  + w_inter * jnp.dot(qh, state.astype(BF16), preferred_element_type=F32))
        num = numden[:, :ML_V_DIM]
        den = numden[:, ML_V_DIM:ML_V_DIM + 1]
        hcur = num / jnp.maximum(jnp.abs(den), jnp.exp(-m_t))

        b_last = b_c[L - 1:L, :]
        decay = b_last - b_c + i_c
        m_new = jnp.maximum(b_last + m_prev, jnp.max(decay, axis=0, keepdims=True))
        w_state = jnp.exp(decay - m_new)
        carry_scale = jnp.exp(b_last + m_prev - m_new)
        vw = (vaug.astype(F32) * w_state).astype(BF16)
        c_ref[h] = carry_scale * state + jnp.dot(kt, vw, preferred_element_type=F32)
        m_ref[h:h + 1, :] = jnp.broadcast_to(m_new, (1, LANES))

        mean_sq = jnp.mean(hcur * hcur, axis=1, keepdims=True)
        sl = slice(h * ML_V_DIM, (h + 1) * ML_V_DIM)
        gate = _sigmoid(mo_ref[rows, sl].astype(F32))
        o_ref[rows, sl] = (hcur * lax.rsqrt(mean_sq + RMS_EPS) * gain_ref[:, sl]
                        * gate).astype(BF16)


def _mlstm(mqk, mv, mo, gates, conv_w, conv_b, gain, B, S):
    T, W = mv.shape
    L = ML_CHUNK
    R = L
    nc = S // R
    row = lambda b, c: (b * nc + c, 0)
    const = lambda b, c: (0, 0)
    return pl.pallas_call(
        _mlstm_kernel,
        grid=(B, nc),
        in_specs=[pl.BlockSpec((R, W), row),
                  pl.BlockSpec((R, W), row),
                  pl.BlockSpec((R, W), row),
                  pl.BlockSpec((R, LANES), row),
                  pl.BlockSpec(conv_w.shape, const),
                  pl.BlockSpec((1, W), const),
                  pl.BlockSpec((1, W), const)],
        out_specs=pl.BlockSpec((R, W), row),
        out_shape=jax.ShapeDtypeStruct((T, W), BF16),
        scratch_shapes=[pltpu.VMEM((ML_HEADS, LANES, 2 * ML_V_DIM), F32),
                        pltpu.VMEM((SUBLANES, LANES), F32),
                        pltpu.VMEM((SUBLANES, W), F32),
                        pltpu.VMEM((ML_HEADS // 2, LANES, L), F32)],
        compiler_params=_params("parallel", "arbitrary"),
        name="mlstm",
    )(mqk, mv, mo, gates, conv_w, conv_b, gain)


def _outproj_kernel(att_ref, ml_ref, wa_ref, wb_ref, x_ref, g_ref, b_ref, o_ref, *, alpha):
    mix = (jnp.dot(att_ref[...], wa_ref[...], preferred_element_type=F32)
           + jnp.dot(ml_ref[...], wb_ref[...], preferred_element_type=F32))
    o_ref[...] = _layer_norm(alpha * x_ref[...] + mix, g_ref[...], b_ref[...])


def _outproj(att, ml, w_out, x2, ln_g, ln_b, alpha):
    T, D = x2.shape
    W = att.shape[1]
    tm = min(T, ROW_TILE)
    row = lambda i: (i, 0)
    const = lambda i: (0, 0)
    return pl.pallas_call(
        functools.partial(_outproj_kernel, alpha=alpha),
        grid=(T // tm,),
        in_specs=[pl.BlockSpec((tm, W), row),
                  pl.BlockSpec((tm, W), row),
                  pl.BlockSpec((W, D), lambda i: (0, 0)),
                  pl.BlockSpec((W, D), lambda i: (1, 0)),
                  pl.BlockSpec((tm, D), row),
                  pl.BlockSpec((1, D), const),
                  pl.BlockSpec((1, D), const)],
        out_specs=pl.BlockSpec((tm, D), row),
        out_shape=jax.ShapeDtypeStruct((T, D), F32),
        compiler_params=_params("parallel"),
        name="outproj_ln",
    )(att, ml, w_out, w_out, x2, ln_g, ln_b)


def _swiglu_chunk(xb, wg, wu, wd):
    g = jnp.dot(xb, wg, preferred_element_type=F32)
    u = jnp.dot(xb, wu, preferred_element_type=F32)
    h = (g * _sigmoid(g) * u).astype(BF16)
    return jnp.dot(h, wd, preferred_element_type=F32)


def _dense_ffn_kernel(x_ref, wg_ref, wu_ref, wd_ref, g_ref, b_ref, o_ref, *, alpha, chunk):
    x = x_ref[...]
    xb = x.astype(BF16)
    F = wg_ref.shape[1]
    acc = None
    for a in range(0, F, chunk):
        b = min(a + chunk, F)
        part = _swiglu_chunk(xb, wg_ref[:, a:b], wu_ref[:, a:b], wd_ref[a:b, :])
        acc = part if acc is None else acc + part
    o_ref[...] = _layer_norm(alpha * x + acc, g_ref[...], b_ref[...])


def _hidden_chunk(width, target):
    for unit in (V7X_MXU_WIDTH, LANES):
        best = 0
        for c in range(unit, target + 1, unit):
            if width % c == 0:
                best = c
        if best:
            return best
    return width


def _dense_ffn(x2, wg, wu, wd, ln_g, ln_b, alpha):
    T, D = x2.shape
    F = wg.shape[1]
    tm = min(T, ROW_TILE)
    row = lambda i: (i, 0)
    const = lambda i: (0, 0)
    resident = dict(pipeline_mode=pl.Buffered(1))
    return pl.pallas_call(
        functools.partial(_dense_ffn_kernel, alpha=alpha, chunk=DENSE_HIDDEN_CHUNK),
        grid=(T // tm,),
        in_specs=[pl.BlockSpec((tm, D), row),
                  pl.BlockSpec((D, F), const, **resident),
                  pl.BlockSpec((D, F), const, **resident),
                  pl.BlockSpec((F, D), const, **resident),
                  pl.BlockSpec((1, D), const),
                  pl.BlockSpec((1, D), const)],
        out_specs=pl.BlockSpec((tm, D), row),
        out_shape=jax.ShapeDtypeStruct((T, D), F32),
        compiler_params=_params("parallel"),
        name="dense_ffn_ln",
    )(x2, wg, wu, wd, ln_g, ln_b)


def _router_kernel(x_ref, rw_ref, rb_ref, ids_ref, wts_ref, cnt_ref, carry_ref, *, n_experts):
    @pl.when(pl.program_id(0) == 0)
    def _reset():
        carry_ref[...] = jnp.zeros_like(carry_ref)

    tm = x_ref.shape[0]
    x_hi, x_lo = _split_bf16(x_ref[...], 2)
    w_hi, w_lo = _split_bf16(rw_ref[...], 2)
    logits = (jnp.dot(x_hi, w_hi, preferred_element_type=F32)
              + (jnp.dot(x_lo, w_hi, preferred_element_type=F32)
                 + jnp.dot(x_hi, w_lo, preferred_element_type=F32))) + rb_ref[...]
    lane = _iota((tm, LANES), 1)
    logits = jnp.where(lane < n_experts, logits, NEG_INF)

    def top(lg):
        best = jnp.max(lg, axis=1, keepdims=True)
        first = jnp.min(jnp.where(lg == best, lane, LANES), axis=1, keepdims=True)
        return best, first

    v1, e1 = top(logits)
    hot1 = lane == e1
    v2, e2 = top(jnp.where(hot1, NEG_INF, logits))
    hot2 = lane == e2
    ex = jnp.exp(v2 - v1)
    w1 = 1.0 / (1.0 + ex)
    w2 = ex / (1.0 + ex)

    assigned = (hot1 | hot2).astype(BF16)
    before = (_iota((tm, tm), 1) < _iota((tm, tm), 0)).astype(BF16)
    carry = carry_ref[0:1, :]
    rank = jnp.dot(before, assigned, preferred_element_type=F32) + carry
    r1 = jnp.sum(jnp.where(hot1, rank, 0.0), axis=1, keepdims=True).astype(I32)
    r2 = jnp.sum(jnp.where(hot2, rank, 0.0), axis=1, keepdims=True).astype(I32)
    total = carry + jnp.sum(assigned.astype(F32), axis=0, keepdims=True)
    carry_ref[...] = jnp.broadcast_to(total, carry_ref.shape)
    cnt_ref[...] = jnp.broadcast_to(total, cnt_ref.shape).astype(I32)

    ids_ref[...] = jnp.where(lane == 0, e1, jnp.where(lane == 1, e2,
                             jnp.where(lane == 2, r1, jnp.where(lane == 3, r2, 0))))
    wts_ref[...] = jnp.where(lane == 0, w1, jnp.where(lane == 1, w2, 0.0))


def _router(x2, router_w, router_b):
    T, D = x2.shape
    E = router_w.shape[1]
    tm = min(T, ROW_TILE)
    rw = jnp.pad(router_w, ((0, 0), (0, LANES - E)))
    rb = jnp.pad(router_b, (0, LANES - E))[None, :]
    row = lambda i: (i, 0)
    const = lambda i: (0, 0)
    return pl.pallas_call(
        functools.partial(_router_kernel, n_experts=E),
        grid=(T // tm,),
        in_specs=[pl.BlockSpec((tm, D), row),
                  pl.BlockSpec((D, LANES), const),
                  pl.BlockSpec((1, LANES), const)],
        out_specs=[pl.BlockSpec((tm, LANES), row),
                   pl.BlockSpec((tm, LANES), row),
                   pl.BlockSpec((SUBLANES, LANES), const)],
        out_shape=[jax.ShapeDtypeStruct((T, LANES), I32),
                   jax.ShapeDtypeStruct((T, LANES), F32),
                   jax.ShapeDtypeStruct((SUBLANES, LANES), I32)],
        scratch_shapes=[pltpu.VMEM((SUBLANES, LANES), F32)],
        compiler_params=_params("arbitrary"),
        name="router",
    )(x2, rw, rb)


def _slots_kernel(ids_ref, tiles_ref, out_ref, *, tile_rows):
    ids = ids_ref[...]
    lane = _iota(ids.shape, 1)
    tiles = tiles_ref[...]

    def slot(k):
        pick = jnp.where(lane == ids[:, k:k + 1], 1.0, 0.0).astype(BF16)
        start = jnp.dot(pick, tiles, preferred_element_type=F32)
        return start.astype(I32) * tile_rows + ids[:, TOP_K + k:TOP_K + k + 1]

    out_ref[...] = jnp.where(lane == 0, slot(0), jnp.where(lane == 1, slot(1), 0))


def _row_slots(ids, offs, tm, tile_rows, n_tiles):
    T = ids.shape[0]
    E = offs.shape[0]
    ts = min(T, SLOT_ROW_TILE)
    assert n_tiles <= 256, "start tiles must stay exact in bf16 (8-bit significand)"
    start_tile = jnp.pad(offs // tile_rows, (0, LANES - E)).astype(BF16)
    tiles = jnp.broadcast_to(start_tile[:, None], (LANES, LANES))
    slots = pl.pallas_call(
        functools.partial(_slots_kernel, tile_rows=tile_rows),
        grid=(T // ts,),
        in_specs=[pl.BlockSpec((ts, LANES), lambda i: (i, 0)),
                  pl.BlockSpec((LANES, LANES), lambda i: (0, 0))],
        out_specs=pl.BlockSpec((ts, LANES), lambda i: (i, 0)),
        out_shape=jax.ShapeDtypeStruct((T, LANES), I32),
        compiler_params=_params("parallel"),
        name="moe_slots",
    )(ids, tiles)
    return slots[:, :TOP_K].reshape(T // tm, tm, TOP_K).transpose(0, 2, 1)


def _for_each_row(tm, fn):
    def group(t, c):
        base = pl.multiple_of(t * DMA_UNROLL, DMA_UNROLL)
        for u in range(DMA_UNROLL):
            for k in range(TOP_K):
                fn(base + u, k)
        return c

    lax.fori_loop(0, tm // DMA_UNROLL, group, 0)


def _scatter_kernel(slot_ref, x_ref, xg_in_ref, xg_ref, sem):
    del xg_in_ref
    tm = x_ref.shape[0]

    def copy(r, k):
        return pltpu.make_async_copy(x_ref.at[pl.ds(r, 1)],
                                     xg_ref.at[pl.ds(slot_ref[0, k, r], 1)], sem)

    _for_each_row(tm, lambda r, k: copy(r, k).start())
    _for_each_row(tm, lambda r, k: copy(r, k).wait())


def _scatter_rows(x2, slots, xg0):
    T, D = x2.shape
    nt, _, tm = slots.shape
    n_rows = xg0.shape[0]
    return pl.pallas_call(
        _scatter_kernel,
        grid=(nt,),
        in_specs=[pl.BlockSpec((1, TOP_K, tm), lambda i: (i, 0, 0), memory_space=pltpu.SMEM),
                  pl.BlockSpec((tm, D), lambda i: (i, 0)),
                  pl.BlockSpec(memory_space=pl.ANY)],
        out_specs=pl.BlockSpec(memory_space=pl.ANY),
        out_shape=jax.ShapeDtypeStruct((n_rows, D), F32),
        scratch_shapes=[pltpu.SemaphoreType.DMA(())],
        input_output_aliases={2: 0},
        compiler_params=_params("arbitrary"),
        name="moe_scatter",
    )(slots, x2, xg0)


def _expert_kernel(te_ref, na_ref, x_ref, wg_ref, wu_ref, wd_ref, o_ref, xb_ref, acc_ref):
    g = pl.program_id(0)
    f = pl.program_id(1)
    active = g < na_ref[0]

    @pl.when(active & (f == 0))
    def _start():
        xb_ref[...] = x_ref[...].astype(BF16)
        acc_ref[...] = jnp.zeros_like(acc_ref)

    @pl.when(active)
    def _accumulate():
        acc_ref[...] += _swiglu_chunk(xb_ref[...], wg_ref[...], wu_ref[...], wd_ref[...])

    last = f == pl.num_programs(1) - 1

    @pl.when(active & last)
    def _finish():
        o_ref[...] = acc_ref[...]

    @pl.when(jnp.logical_not(active) & last)
    def _unused_tile():
        o_ref[...] = jnp.zeros_like(o_ref)


def _expert_ffn(xg, wg, wu, wd, tile_expert, n_active, tm):
    P, D = xg.shape
    E, _, F = wg.shape
    tf = _hidden_chunk(F, EXPERT_HIDDEN_CHUNK)
    nf = F // tf

    def tile(g, te, na):
        return jnp.maximum(jnp.minimum(g, na[0] - 1), 0)

    def chunk(g, f, na):
        return jnp.where(g < na[0], f, nf - 1)

    return pl.pallas_call(
        _expert_kernel,
        grid_spec=pltpu.PrefetchScalarGridSpec(
            num_scalar_prefetch=2,
            grid=(P // tm, nf),
            in_specs=[pl.BlockSpec((tm, D), lambda g, f, te, na: (tile(g, te, na), 0)),
                      pl.BlockSpec((None, D, tf),
                                   lambda g, f, te, na: (te[tile(g, te, na)], 0, chunk(g, f, na))),
                      pl.BlockSpec((None, D, tf),
                                   lambda g, f, te, na: (te[tile(g, te, na)], 0, chunk(g, f, na))),
                      pl.BlockSpec((None, tf, D),
                                   lambda g, f, te, na: (te[tile(g, te, na)], chunk(g, f, na), 0))],
            out_specs=pl.BlockSpec((tm, D), lambda g, f, te, na: (g, 0)),
            scratch_shapes=[pltpu.VMEM((tm, D), BF16), pltpu.VMEM((tm, D), F32)]),
        out_shape=jax.ShapeDtypeStruct((P, D), F32),
        compiler_params=_params("arbitrary", "arbitrary"),
        name="moe_experts",
    )(tile_expert, n_active, xg, wg, wu, wd)


def _combine_kernel(slot_ref, next_ref, yg_ref, wts_ref, x_ref, g_ref, b_ref, o_ref,
                    buf_ref, sem, *, alpha):
    tm = x_ref.shape[0]
    i = pl.program_id(0)
    cur = lax.rem(i, 2)

    def copy(table, buf, r, k):
        return pltpu.make_async_copy(yg_ref.at[pl.ds(table[0, k, r], 1)],
                                     buf_ref.at[buf, k, pl.ds(r, 1)], sem.at[buf])

    @pl.when(i == 0)
    def _first_tile():
        _for_each_row(tm, lambda r, k: copy(slot_ref, 0, r, k).start())

    @pl.when(i + 1 < pl.num_programs(0))
    def _next_tile():
        _for_each_row(tm, lambda r, k: copy(next_ref, 1 - cur, r, k).start())

    _for_each_row(tm, lambda r, k: copy(slot_ref, cur, r, k).wait())
    w = wts_ref[...]
    ffn = w[:, 0:1] * buf_ref[cur, 0] + w[:, 1:2] * buf_ref[cur, 1]
    o_ref[...] = _layer_norm(alpha * x_ref[...] + ffn, g_ref[...], b_ref[...])


def _combine(yg, slots, wts, x2, ln_g, ln_b, alpha):
    T, D = x2.shape
    nt, _, tm = slots.shape
    row = lambda i: (i, 0)
    const = lambda i: (0, 0)
    smem = dict(memory_space=pltpu.SMEM)
    return pl.pallas_call(
        functools.partial(_combine_kernel, alpha=alpha),
        grid=(nt,),
        in_specs=[pl.BlockSpec((1, TOP_K, tm), lambda i: (i, 0, 0), **smem),
                  pl.BlockSpec((1, TOP_K, tm), lambda i: (jnp.minimum(i + 1, nt - 1), 0, 0), **smem),
                  pl.BlockSpec(memory_space=pl.ANY),
                  pl.BlockSpec((tm, LANES), row),
                  pl.BlockSpec((tm, D), row),
                  pl.BlockSpec((1, D), const),
                  pl.BlockSpec((1, D), const)],
        out_specs=pl.BlockSpec((tm, D), row),
        out_shape=jax.ShapeDtypeStruct((T, D), F32),
        scratch_shapes=[pltpu.VMEM((2, TOP_K, tm, D), F32), pltpu.SemaphoreType.DMA((2,))],
        compiler_params=_params("arbitrary"),
        name="moe_combine_ln",
    )(slots, slots, yg, wts, x2, ln_g, ln_b)


def _moe_tile(T):
    return min(T, EXPERT_ROW_TILE)


def _moe_rows(T, E):
    return TOP_K * T + E * _moe_tile(T)


def _moe_ffn(x2, router_w, router_b, wg, wu, wd, ln_g, ln_b, alpha, xg0):
    T, D = x2.shape
    E = router_w.shape[1]
    tm_e = _moe_tile(T)
    tm_r = min(T, DMA_ROW_TILE)
    ids, wts, counts = _router(x2, router_w, router_b)

    cnt = counts[0, :E]
    padded = ((cnt + tm_e - 1) // tm_e) * tm_e
    ends = jnp.cumsum(padded)
    offs = (ends - padded).astype(I32)
    n_rows = xg0.shape[0]
    n_tiles = n_rows // tm_e
    tile_start = jnp.arange(n_tiles, dtype=I32) * tm_e
    tile_expert = jnp.minimum(jnp.sum(tile_start[:, None] >= ends[None, :], axis=1), E - 1).astype(I32)
    n_active = (ends[-1:] // tm_e).astype(I32)

    slots = _row_slots(ids, offs, tm_r, tm_e, n_tiles)
    xg = _scatter_rows(x2, slots, xg0)
    yg = _expert_ffn(xg, wg, wu, wd, tile_expert, n_active, tm_e)
    return _combine(yg, slots, wts, x2, ln_g, ln_b, alpha)


def kernel(x, positions, w_in, gate_bias, conv_w, conv_b, norm_attn, norm_mlstm, w_out,
           ln1_g, ln1_b, dense_w_gate, dense_w_up, dense_w_down, router_w, router_b,
           moe_w_gate, moe_w_up, moe_w_down, ln2_g, ln2_b):
    B, S, D = x.shape
    depth = w_in.shape[0]
    T = B * S
    alpha = (2.0 * depth) ** 0.25
    n_main = 3 * ATT_WIDTH + 2 * ML_QK_WIDTH + 2 * ML_WIDTH
    nblk = S // MOBA_BLOCK

    tabs = _rope_tables(positions)
    x2 = x.reshape(T, D)
    moe_bf16 = None
    for l in range(depth):
        w_main = w_in[l, :, :n_main].astype(BF16)
        w_gate = jnp.pad(w_in[l, :, n_main:], ((0, 0), (0, LANES - 2 * ML_HEADS))).astype(BF16)
        gbias = jnp.pad(gate_bias[l], (0, LANES - 2 * ML_HEADS))[None, :]
        q, k, v, mqk, mv, mo, gates, kmean = _inproj(x2, w_main, w_gate, gbias, tabs)
        kmean = jnp.pad(kmean.reshape(B, nblk, ATT_WIDTH), ((0, 0), (0, LANES - nblk), (0, 0)))
        ride = []
        if l % 2 == 0 and l + 1 < depth:
            ride = [w[l // 2].reshape(-1, w.shape[-1]) for w in (moe_w_gate, moe_w_up, moe_w_down)]
            if not all(_can_ride_attention(a, B, S) for a in ride):
                ride = []
        clear = []
        if l % 2 == 1:
            clear = [jax.ShapeDtypeStruct((_moe_rows(T, router_w.shape[-1]), D), F32)]
            if not _can_ride_attention(clear[0], B, S):
                clear = []
        att, rode, cleared = _moba_attention(q, k, v, kmean, norm_attn[l][None, :], B, S,
                                             ride, clear)
        if ride:
            moe_bf16 = [a.reshape(w.shape[1:]) for a, w in zip(rode, (moe_w_gate, moe_w_up, moe_w_down))]
        ml = _mlstm(mqk, mv, mo, gates, conv_w[l], conv_b[l][None, :],
                    norm_mlstm[l][None, :], B, S)
        x2 = _outproj(att, ml, w_out[l].astype(BF16), x2, ln1_g[l][None, :], ln1_b[l][None, :],
                      alpha)
        j = l // 2
        if l % 2 == 0:
            x2 = _dense_ffn(x2, dense_w_gate[j].astype(BF16), dense_w_up[j].astype(BF16),
                            dense_w_down[j].astype(BF16), ln2_g[l][None, :], ln2_b[l][None, :],
                            alpha)
        else:
            if moe_bf16 is None:
                moe_bf16 = [w[j].astype(BF16) for w in (moe_w_gate, moe_w_up, moe_w_down)]
            xg0 = cleared[0] if cleared else jnp.zeros((_moe_rows(T, router_w.shape[-1]), D), F32)
            x2 = _moe_ffn(x2, router_w[j], router_b[j], *moe_bf16,
                          ln2_g[l][None, :], ln2_b[l][None, :], alpha, xg0)
            moe_bf16 = None
    return x2.reshape(B, S, D)
```

```python
import functools

import jax
import jax.numpy as jnp
from jax import lax
from jax.experimental import pallas as pl
from jax.experimental.pallas import tpu as pltpu

F32 = jnp.float32
BF16 = jnp.bfloat16
I32 = jnp.int32

ATT_HEADS = 8
ATT_HEAD_DIM = 64
ATT_WIDTH = ATT_HEADS * ATT_HEAD_DIM
MOBA_BLOCK = 256
MOBA_TOPK = 3
ROPE_THETA = 500000.0
ROPE_DIM = ATT_HEAD_DIM // 4
ML_HEADS = 4
ML_V_DIM = 128
ML_QK_DIM = 64
ML_QK_WIDTH = ML_HEADS * ML_QK_DIM
ML_WIDTH = ML_HEADS * ML_V_DIM
ML_CHUNK = 128
CONV_WIDTH = 4
TOP_K = 2
LN_EPS = 1e-5
RMS_EPS = 1e-6

LANES = 128
SUBLANES = 8
V7X_MXU_WIDTH = 256
V7X_VMEM_BYTES = 64 * 1024 * 1024
VMEM_LIMIT = (V7X_VMEM_BYTES * 3) // 4

ROW_TILE = 512
ROPE_ROW_TILE = 1024
SLOT_ROW_TILE = 2048
DMA_ROW_TILE = 256
EXPERT_ROW_TILE = 512
DENSE_HIDDEN_CHUNK = 2 * V7X_MXU_WIDTH
EXPERT_HIDDEN_CHUNK = 7 * V7X_MXU_WIDTH

MASK_BIAS = -1e9
HEAD_B_BIAS = 64
MAX_RISE = 60.0
NEG_INF = float("-inf")
HIGHEST = lax.Precision.HIGHEST
LOG2_E = 1.4426950408889634
DMA_UNROLL = 8

_NT = (((1,), (1,)), ((), ()))


def _params(*sem):
    return pltpu.CompilerParams(dimension_semantics=sem, vmem_limit_bytes=VMEM_LIMIT)


def _iota(shape, dim):
    return lax.broadcasted_iota(I32, shape, dim)


def _sigmoid(x):
    return 1.0 / (1.0 + jnp.exp(-x))


def _layer_norm(y, g, b):
    mu = jnp.mean(y, axis=-1, keepdims=True)
    yc = y - mu
    var = jnp.mean(yc * yc, axis=-1, keepdims=True)
    return yc * lax.rsqrt(var + LN_EPS) * g + b


def _rope_tab_kernel(pos_ref, freq_ref, cos_ref, s1_ref, s2_ref):
    ang = pos_ref[...].astype(F32) * freq_ref[...]
    d = _iota(ang.shape, 1) & (ATT_HEAD_DIM - 1)
    half = ROPE_DIM // 2
    s = jnp.sin(ang)
    cos_ref[...] = jnp.cos(ang)
    s1_ref[...] = jnp.where(d < half, -s, 0.0)
    s2_ref[...] = jnp.where((d >= half) & (d < ROPE_DIM), s, 0.0)


def _rope_tables(positions):
    T = positions.size
    tm = min(T, ROPE_ROW_TILE)
    half = ROPE_DIM // 2
    inv_freq = ROPE_THETA ** (-jnp.arange(half, dtype=F32) / half)
    d = jnp.arange(LANES) % ATT_HEAD_DIM
    freq = jnp.where(d < ROPE_DIM, inv_freq[d % half], 0.0).astype(F32)[None, :]
    tab = jax.ShapeDtypeStruct((T, LANES), F32)
    return pl.pallas_call(
        _rope_tab_kernel,
        grid=(T // tm,),
        in_specs=[pl.BlockSpec((tm, 1), lambda i: (i, 0)),
                  pl.BlockSpec((1, LANES), lambda i: (0, 0))],
        out_specs=[pl.BlockSpec((tm, LANES), lambda i: (i, 0))] * 3,
        out_shape=[tab, tab, tab],
        compiler_params=_params("parallel"),
        name="rope_tables",
    )(positions.reshape(T, 1), freq)


def _inproj_kernel(x_ref, w_ref, wg_ref, gb_ref, cos_ref, s1_ref, s2_ref,
                   q_ref, k_ref, v_ref, mqk_ref, mv_ref, mo_ref, g_ref, km_ref):
    xb = x_ref[...].astype(BF16)
    W = ATT_WIDTH

    def proj(c):
        return jnp.dot(xb, w_ref[:, c * W:(c + 1) * W], preferred_element_type=F32)

    rep = W // LANES
    cos = jnp.concatenate([cos_ref[...]] * rep, axis=1)
    s1 = jnp.concatenate([s1_ref[...]] * rep, axis=1)
    s2 = jnp.concatenate([s2_ref[...]] * rep, axis=1)
    half = ROPE_DIM // 2

    def rope(t):
        return t * cos + pltpu.roll(t, W - half, 1) * s1 + pltpu.roll(t, half, 1) * s2

    q_ref[...] = (rope(proj(0)) * (LOG2_E * ATT_HEAD_DIM ** -0.5)).astype(BF16)
    k = rope(proj(1))
    k_ref[...] = k.astype(BF16)
    for g in range(k.shape[0] // MOBA_BLOCK):
        blk = k[g * MOBA_BLOCK:(g + 1) * MOBA_BLOCK]
        km_ref[0, g:g + 1, :] = jnp.sum(blk, axis=0, keepdims=True) * (1.0 / MOBA_BLOCK)
    v_ref[...] = proj(2).astype(BF16)
    mqk_ref[...] = proj(3).astype(BF16)
    mv_ref[...] = proj(4).astype(BF16)
    mo_ref[...] = proj(5).astype(BF16)
    g_ref[...] = jnp.dot(xb, wg_ref[...], preferred_element_type=F32) + gb_ref[...]


def _inproj(x2, w_main, w_gate, gate_bias, tabs):
    T, D = x2.shape
    tm = min(T, ROW_TILE)
    nkb = tm // MOBA_BLOCK
    W = ATT_WIDTH
    row = lambda i: (i, 0)
    const = lambda i: (0, 0)
    act = jax.ShapeDtypeStruct((T, W), BF16)
    return pl.pallas_call(
        _inproj_kernel,
        grid=(T // tm,),
        in_specs=[pl.BlockSpec((tm, D), row),
                  pl.BlockSpec(w_main.shape, const),
                  pl.BlockSpec(w_gate.shape, const),
                  pl.BlockSpec((1, LANES), const),
                  pl.BlockSpec((tm, LANES), row),
                  pl.BlockSpec((tm, LANES), row),
                  pl.BlockSpec((tm, LANES), row)],
        out_specs=[pl.BlockSpec((tm, W), row)] * 6
        + [pl.BlockSpec((tm, LANES), row),
           pl.BlockSpec((1, nkb, W), lambda i: (i, 0, 0))],
        out_shape=[act] * 6 + [jax.ShapeDtypeStruct((T, LANES), F32),
                               jax.ShapeDtypeStruct((T // tm, nkb, W), F32)],
        compiler_params=_params("parallel"),
        name="inproj",
    )(x2, w_main, w_gate, gate_bias, *tabs)


def _split_bf16(x, parts):
    out = []
    for _ in range(parts):
        hi = x.astype(BF16)
        out.append(hi)
        x = x - hi.astype(F32)
    return out


def _attn_kernel(q_ref, k_ref, v_ref, km_ref, gain_ref, *rest, group, n_cast, n_zero):
    cast_in, rest = rest[:n_cast], rest[n_cast:]
    o_ref, rest = rest[0], rest[1:]
    cast_out, rest = rest[:n_cast], rest[n_cast:]
    zero_out, (kaug_ref, vaug_ref, kabs_ref) = rest[:n_zero], rest[n_zero:]
    for src_ref, dst_ref in zip(cast_in, cast_out):
        dst_ref[...] = src_ref[...].astype(BF16)
    for dst_ref in zero_out:
        dst_ref[...] = jnp.zeros_like(dst_ref)
    i = pl.program_id(2)
    tq = q_ref.shape[0]
    nblk = k_ref.shape[0] // MOBA_BLOCK
    lane = _iota((tq, LANES), 1)
    head_a = lane < ATT_HEAD_DIM

    @pl.when(i == 0)
    def _build_augmented_keys_values():
        blane = _iota((MOBA_BLOCK, LANES), 1)
        brow_k = _iota((LANES, MOBA_BLOCK), 0)
        in_a = blane < ATT_HEAD_DIM
        one = jnp.ones((MOBA_BLOCK, LANES), BF16)

        def fill(j, carry):
            rows = pl.ds(pl.multiple_of(j * MOBA_BLOCK, MOBA_BLOCK), MOBA_BLOCK)
            kaug_ref[:LANES, rows] = k_ref[rows, :].astype(F32).T.astype(BF16)
            kaug_ref[LANES:, rows] = jnp.where((brow_k == j) | (brow_k == j + HEAD_B_BIAS), 1.0,
                                               0.0).astype(BF16)
            vj = v_ref[rows, :]
            vaug_ref[0, rows, :] = jnp.where(in_a, vj, one)
            vaug_ref[1, rows, :] = jnp.where(in_a, one, vj)
            kj = jnp.abs(k_ref[rows, :].astype(F32))
            return jnp.maximum(carry, jnp.max(kj, axis=0, keepdims=True))

        kabs = lax.fori_loop(0, nblk, fill, jnp.zeros((1, LANES), F32))
        kabs_ref[...] = jnp.broadcast_to(kabs, kabs_ref.shape)

    q = q_ref[...]
    zero = jnp.zeros_like(q)
    nrow = -(-nblk // SUBLANES) * SUBLANES
    km_stack = jnp.concatenate(_split_bf16(km_ref[:nrow, :], 3), axis=0)
    brow = _iota((nrow, tq), 0)

    def select_bias(qh):
        parts = lax.dot_general(km_stack, qh, _NT, preferred_element_type=F32)
        gate = parts[:nrow] + parts[nrow:2 * nrow] + parts[2 * nrow:]
        gate = jnp.where(brow < i, gate, NEG_INF)
        picked = brow < 0
        for _ in range(MOBA_TOPK):
            best = jnp.max(gate, axis=0, keepdims=True)
            first = jnp.min(jnp.where(gate == best, brow, nrow), axis=0, keepdims=True)
            hit = (brow == first) & (best > NEG_INF)
            picked = picked | hit
            gate = jnp.where(hit, NEG_INF, gate)
        bias = jnp.where(picked, 0.0, MASK_BIAS)
        if nrow < HEAD_B_BIAS:
            bias = jnp.concatenate([bias, jnp.zeros((HEAD_B_BIAS - nrow, tq), F32)], axis=0)
        return bias

    qh = (jnp.where(head_a, q, zero), jnp.where(head_a, zero, q))
    bias = jnp.concatenate([select_bias(qh[0]), select_bias(qh[1])], axis=0).T.astype(BF16)
    zero_b = jnp.zeros_like(bias)
    qa = [jnp.concatenate([qh[0], jnp.where(head_a, bias, zero_b)], axis=1),
          jnp.concatenate([qh[1], jnp.where(head_a, zero_b, bias)], axis=1)]

    own = pl.ds(pl.multiple_of(i * MOBA_BLOCK, MOBA_BLOCK), MOBA_BLOCK)
    kd = kaug_ref[:LANES, own]
    causal = _iota((tq, MOBA_BLOCK), 1) <= _iota((tq, MOBA_BLOCK), 0)
    state = []
    for h in range(2):
        s = jnp.where(causal, jnp.dot(qh[h], kd, preferred_element_type=F32), NEG_INF)
        m = jnp.max(s, axis=1, keepdims=True)
        p = jnp.exp2(s - m).astype(BF16)
        state += [m, jnp.dot(p, vaug_ref[h, own, :], preferred_element_type=F32)]

    span = group * MOBA_BLOCK

    def exact_body(g, carry):
        rows = pl.ds(pl.multiple_of(g * span, span), span)
        kg = kaug_ref[:, rows]
        new = []
        for h in range(2):
            m, acc = carry[2 * h], carry[2 * h + 1]
            s = jnp.dot(qa[h], kg, preferred_element_type=F32)
            m_new = jnp.maximum(m, jnp.max(s, axis=1, keepdims=True))
            p = jnp.exp2(s - m_new).astype(BF16)
            acc = jnp.exp2(m - m_new) * acc + jnp.dot(p, vaug_ref[h, rows, :],
                                                     preferred_element_type=F32)
            new += [m_new, acc]
        return tuple(new)

    def fixed_body(g, carry):
        rows = pl.ds(pl.multiple_of(g * span, span), span)
        kg = kaug_ref[:, rows]
        new = []
        for h in range(2):
            m, acc = carry[2 * h], carry[2 * h + 1]
            s = jnp.dot(qa[h], kg, preferred_element_type=F32)
            p = jnp.exp2(s - m).astype(BF16)
            new += [m, acc + jnp.dot(p, vaug_ref[h, rows, :], preferred_element_type=F32)]
        return tuple(new)

    kabs = kabs_ref[0:1, :]
    rise = [jnp.sum(jnp.abs(qh[h].astype(F32)) * kabs, axis=1, keepdims=True) - state[2 * h]
            for h in range(2)]
    fixed_ok = jnp.max(jnp.maximum(rise[0], rise[1])) <= MAX_RISE

    n_groups = lax.div(i + (group - 1), group)
    _, acc_a, _, acc_b = lax.cond(
        fixed_ok,
        lambda: lax.fori_loop(0, n_groups, fixed_body, tuple(state)),
        lambda: lax.fori_loop(0, n_groups, exact_body, tuple(state)))
    num = jnp.where(head_a, acc_a, acc_b)
    sq = num * num
    ms_a = jnp.sum(jnp.where(head_a, sq, 0.0), axis=1, keepdims=True) * (1.0 / ATT_HEAD_DIM)
    ms_b = jnp.sum(jnp.where(head_a, 0.0, sq), axis=1, keepdims=True) * (1.0 / ATT_HEAD_DIM)
    den_a = acc_a[:, ATT_HEAD_DIM:ATT_HEAD_DIM + 1]
    den_b = acc_b[:, 0:1]
    scale = jnp.where(head_a, lax.rsqrt(ms_a + RMS_EPS * den_a * den_a),
                      lax.rsqrt(ms_b + RMS_EPS * den_b * den_b))
    o_ref[...] = (num * scale * gain_ref[...]).astype(BF16)


def _moba_attention(q, k, v, kmean, gain, B, S, to_bf16=(), zeros=()):
    T, W = q.shape
    nq = S // MOBA_BLOCK
    npairs = W // LANES
    group = 4 if nq % 4 == 0 else 1
    assert nq <= HEAD_B_BIAS, "one 128-lane group holds both heads' block biases"
    steps = B * npairs * nq
    step = lambda b, h, i: ((b * npairs + h) * nq + i, 0)
    ride = lambda a: pl.BlockSpec((a.shape[0] // steps, a.shape[1]), step)
    outs = pl.pallas_call(
        functools.partial(_attn_kernel, group=group, n_cast=len(to_bf16), n_zero=len(zeros)),
        grid=(B, npairs, nq),
        in_specs=[pl.BlockSpec((MOBA_BLOCK, LANES), lambda b, h, i: (b * nq + i, h)),
                  pl.BlockSpec((S, LANES), lambda b, h, i: (b, h)),
                  pl.BlockSpec((S, LANES), lambda b, h, i: (b, h)),
                  pl.BlockSpec((None, LANES, LANES), lambda b, h, i: (b, 0, h)),
                  pl.BlockSpec((1, LANES), lambda b, h, i: (0, h))] + [ride(a) for a in to_bf16],
        out_specs=[pl.BlockSpec((MOBA_BLOCK, LANES), lambda b, h, i: (b * nq + i, h))]
        + [ride(a) for a in to_bf16] + [ride(a) for a in zeros],
        out_shape=[jax.ShapeDtypeStruct((T, W), BF16)]
        + [jax.ShapeDtypeStruct(a.shape, BF16) for a in to_bf16] + list(zeros),
        scratch_shapes=[pltpu.VMEM((2 * LANES, S), BF16), pltpu.VMEM((2, S, LANES), BF16),
                        pltpu.VMEM((SUBLANES, LANES), F32)],
        compiler_params=_params("parallel", "parallel", "arbitrary"),
        name="moba_attention",
    )(q, k, v, kmean, gain, *to_bf16)
    n = len(to_bf16)
    return outs[0], outs[1:1 + n], outs[1 + n:]


def _can_ride_attention(a, B, S):
    steps = B * (ATT_WIDTH // LANES) * (S // MOBA_BLOCK)
    return a.shape[0] % (steps * 2 * SUBLANES) == 0


def _mlstm_kernel(mqk_ref, mv_ref, mo_ref, g_ref, cw_ref, cb_ref, gain_ref, o_ref,
                  c_ref, m_ref, tail_ref, kt_ref):
    L = ML_CHUNK
    R = mqk_ref.shape[0]

    @pl.when(pl.program_id(1) == 0)
    def _reset():
        c_ref[...] = jnp.zeros_like(c_ref)
        m_ref[...] = jnp.zeros_like(m_ref)
        tail_ref[...] = jnp.zeros_like(tail_ref)

    cur = mqk_ref[...].astype(F32)
    xp = jnp.concatenate([tail_ref[...], cur], axis=0)
    base = SUBLANES - (CONV_WIDTH - 1)
    y_all = cb_ref[...] + cw_ref[0:1, :] * xp[base:base + R]
    for j in range(1, CONV_WIDTH):
        y_all = y_all + cw_ref[j:j + 1, :] * xp[base + j:base + j + R]
    tail_ref[...] = cur[R - SUBLANES:]
    y_all = y_all * _sigmoid(y_all)
    for c in range(R // L):
        _mlstm_chunk(y_all[c * L:(c + 1) * L], slice(c * L, (c + 1) * L), mv_ref, mo_ref, g_ref,
                     gain_ref, o_ref, c_ref, m_ref, kt_ref)


def _mlstm_chunk(y, rows, mv_ref, mo_ref, g_ref, gain_ref, o_ref, c_ref, m_ref, kt_ref):
    L = ML_CHUNK

    lane = _iota((L, LANES), 1)
    row = _iota((L, LANES), 0)
    tri = lane <= row
    g8 = g_ref[rows, :].T[:SUBLANES]
    grow = _iota((SUBLANES, L), 0)
    logf = -(jnp.maximum(-g8, 0.0) + jnp.log1p(jnp.exp(-jnp.abs(g8))))
    upper = (row <= lane).astype(F32)
    bcum = jnp.dot(jnp.where(grow >= ML_HEADS, logf, 0.0), upper, precision=HIGHEST,
                   preferred_element_type=F32)
    rows8 = jnp.where(grow < ML_HEADS, g8, bcum)
    cols = jnp.concatenate([rows8, jnp.zeros((LANES - SUBLANES, L), F32)], axis=0).T

    for p in range(ML_HEADS // 2):
        kt_ref[p] = y[:, ML_QK_WIDTH + p * LANES:ML_QK_WIDTH + (p + 1) * LANES].T
    ones_col = jnp.where(lane == 0, 1.0, 0.0).astype(BF16)
    for h in range(ML_HEADS):
        pair, odd = divmod(h, 2)
        in_head = (lane >= odd * ML_QK_DIM) & (lane < (odd + 1) * ML_QK_DIM)
        dim_in_head = (row >= odd * ML_QK_DIM) & (row < (odd + 1) * ML_QK_DIM)
        yq = y[:, pair * LANES:(pair + 1) * LANES]
        qh = jnp.where(in_head, yq * (ML_QK_DIM ** -0.5), 0.0).astype(BF16)
        kt = jnp.where(dim_in_head, kt_ref[pair], 0.0).astype(BF16)
        vh = mv_ref[rows, h * ML_V_DIM:(h + 1) * ML_V_DIM]
        vaug = jnp.concatenate([vh, ones_col], axis=1)

        i_c = cols[:, h:h + 1]
        b_c = cols[:, ML_HEADS + h:ML_HEADS + h + 1]
        i_r = rows8[h:h + 1, :]
        b_r = rows8[ML_HEADS + h:ML_HEADS + h + 1, :]
        m_prev = m_ref[h:h + 1, 0:1]

        d_log = jnp.where(tri, b_c - b_r + i_r, NEG_INF)
        inter = b_c + m_prev
        m_t = jnp.maximum(inter, jnp.max(d_log, axis=1, keepdims=True))
        w_intra = jnp.exp(d_log - m_t)
        w_inter = jnp.exp(inter - m_t)

        state = c_ref[h]
        s = jnp.dot(qh, kt, preferred_element_type=F32) * w_intra
        numden = (jnp.dot(s.astype(BF16), vaug, preferred_element_type=F32)
                  + w_inter * jnp.dot(qh, state.astype(BF16), preferred_element_type=F32))
        num = numden[:, :ML_V_DIM]
        den = numden[:, ML_V_DIM:ML_V_DIM + 1]
        hcur = num / jnp.maximum(jnp.abs(den), jnp.exp(-m_t))

        b_last = b_c[L - 1:L, :]
        decay = b_last - b_c + i_c
        m_new = jnp.maximum(b_last + m_prev, jnp.max(decay, axis=0, keepdims=True))
        w_state = jnp.exp(decay - m_new)
        carry_scale = jnp.exp(b_last + m_prev - m_new)
        vw = (vaug.astype(F32) * w_state).astype(BF16)
        c_ref[h] = carry_scale * state + jnp.dot(kt, vw, preferred_element_type=F32)
        m_ref[h:h + 1, :] = jnp.broadcast_to(m_new, (1, LANES))

        mean_sq = jnp.mean(hcur * hcur, axis=1, keepdims=True)
        sl = slice(h * ML_V_DIM, (h + 1) * ML_V_DIM)
        gate = _sigmoid(mo_ref[rows, sl].astype(F32))
        o_ref[rows, sl] = (hcur * lax.rsqrt(mean_sq + RMS_EPS) * gain_ref[:, sl]
                        * gate).astype(BF16)


def _mlstm(mqk, mv, mo, gates, conv_w, conv_b, gain, B, S):
    T, W = mv.shape
    L = ML_CHUNK
    R = L
    nc = S // R
    row = lambda b, c: (b * nc + c, 0)
    const = lambda b, c: (0, 0)
    return pl.pallas_call(
        _mlstm_kernel,
        grid=(B, nc),
        in_specs=[pl.BlockSpec((R, W), row),
                  pl.BlockSpec((R, W), row),
                  pl.BlockSpec((R, W), row),
                  pl.BlockSpec((R, LANES), row),
                  pl.BlockSpec(conv_w.shape, const),
                  pl.BlockSpec((1, W), const),
                  pl.BlockSpec((1, W), const)],
        out_specs=pl.BlockSpec((R, W), row),
        out_shape=jax.ShapeDtypeStruct((T, W), BF16),
        scratch_shapes=[pltpu.VMEM((ML_HEADS, LANES, 2 * ML_V_DIM), F32),
                        pltpu.VMEM((SUBLANES, LANES), F32),
                        pltpu.VMEM((SUBLANES, W), F32),
                        pltpu.VMEM((ML_HEADS // 2, LANES, L), F32)],
        compiler_params=_params("parallel", "arbitrary"),
        name="mlstm",
    )(mqk, mv, mo, gates, conv_w, conv_b, gain)


def _outproj_kernel(att_ref, ml_ref, wa_ref, wb_ref, x_ref, g_ref, b_ref, o_ref, *, alpha):
    mix = (jnp.dot(att_ref[...], wa_ref[...], preferred_element_type=F32)
           + jnp.dot(ml_ref[...], wb_ref[...], preferred_element_type=F32))
    o_ref[...] = _layer_norm(alpha * x_ref[...] + mix, g_ref[...], b_ref[...])


def _outproj(att, ml, w_out, x2, ln_g, ln_b, alpha):
    T, D = x2.shape
    W = att.shape[1]
    tm = min(T, ROW_TILE)
    row = lambda i: (i, 0)
    const = lambda i: (0, 0)
    return pl.pallas_call(
        functools.partial(_outproj_kernel, alpha=alpha),
        grid=(T // tm,),
        in_specs=[pl.BlockSpec((tm, W), row),
                  pl.BlockSpec((tm, W), row),
                  pl.BlockSpec((W, D), lambda i: (0, 0)),
                  pl.BlockSpec((W, D), lambda i: (1, 0)),
                  pl.BlockSpec((tm, D), row),
                  pl.BlockSpec((1, D), const),
                  pl.BlockSpec((1, D), const)],
        out_specs=pl.BlockSpec((tm, D), row),
        out_shape=jax.ShapeDtypeStruct((T, D), F32),
        compiler_params=_params("parallel"),
        name="outproj_ln",
    )(att, ml, w_out, w_out, x2, ln_g, ln_b)


def _swiglu_chunk(xb, wg, wu, wd):
    g = jnp.dot(xb, wg, preferred_element_type=F32)
    u = jnp.dot(xb, wu, preferred_element_type=F32)
    h = (g * _sigmoid(g) * u).astype(BF16)
    return jnp.dot(h, wd, preferred_element_type=F32)


def _hidden_chunk(width, target):
    for unit in (V7X_MXU_WIDTH, LANES):
        best = 0
        for c in range(unit, target + 1, unit):
            if width % c == 0:
                best = c
        if best:
            return best
    return width


def _mixer_dense_kernel(att_ref, ml_ref, wa_ref, wb_ref, x_ref, g1_ref, b1_ref,
                        wg_ref, wu_ref, wd_ref, g2_ref, b2_ref, o_ref, *, alpha, chunk):
    mix = (jnp.dot(att_ref[...], wa_ref[...], preferred_element_type=F32)
           + jnp.dot(ml_ref[...], wb_ref[...], preferred_element_type=F32))
    x1 = _layer_norm(alpha * x_ref[...] + mix, g1_ref[...], b1_ref[...])
    xb = x1.astype(BF16)
    F = wg_ref.shape[1]
    acc = None
    for a in range(0, F, chunk):
        b = min(a + chunk, F)
        part = _swiglu_chunk(xb, wg_ref[:, a:b], wu_ref[:, a:b], wd_ref[a:b, :])
        acc = part if acc is None else acc + part
    o_ref[...] = _layer_norm(alpha * x1 + acc, g2_ref[...], b2_ref[...])


def _mixer_dense_ffn(att, ml, w_out, x2, ln1_g, ln1_b, wg, wu, wd, ln2_g, ln2_b, alpha):
    T, D = x2.shape
    W = att.shape[1]
    F = wg.shape[1]
    tm = min(T, ROW_TILE)
    row = lambda i: (i, 0)
    const = lambda i: (0, 0)
    resident = dict(pipeline_mode=pl.Buffered(1))
    return pl.pallas_call(
        functools.partial(_mixer_dense_kernel, alpha=alpha, chunk=DENSE_HIDDEN_CHUNK),
        grid=(T // tm,),
        in_specs=[pl.BlockSpec((tm, W), row),
                  pl.BlockSpec((tm, W), row),
                  pl.BlockSpec((W, D), lambda i: (0, 0), **resident),
                  pl.BlockSpec((W, D), lambda i: (1, 0), **resident),
                  pl.BlockSpec((tm, D), row),
                  pl.BlockSpec((1, D), const),
                  pl.BlockSpec((1, D), const),
                  pl.BlockSpec((D, F), const, **resident),
                  pl.BlockSpec((D, F), const, **resident),
                  pl.BlockSpec((F, D), const, **resident),
                  pl.BlockSpec((1, D), const),
                  pl.BlockSpec((1, D), const)],
        out_specs=pl.BlockSpec((tm, D), row),
        out_shape=jax.ShapeDtypeStruct((T, D), F32),
        compiler_params=_params("parallel"),
        name="outproj_dense_ffn_ln",
    )(att, ml, w_out, w_out, x2, ln1_g, ln1_b, wg, wu, wd, ln2_g, ln2_b)


def _router_kernel(x_ref, rw_ref, rb_ref, ids_ref, wts_ref, cnt_ref, carry_ref, *, n_experts):
    @pl.when(pl.program_id(0) == 0)
    def _reset():
        carry_ref[...] = jnp.zeros_like(carry_ref)

    tm = x_ref.shape[0]
    x_hi, x_lo = _split_bf16(x_ref[...], 2)
    w_hi, w_lo = _split_bf16(rw_ref[...], 2)
    logits = (jnp.dot(x_hi, w_hi, preferred_element_type=F32)
              + (jnp.dot(x_lo, w_hi, preferred_element_type=F32)
                 + jnp.dot(x_hi, w_lo, preferred_element_type=F32))) + rb_ref[...]
    lane = _iota((tm, LANES), 1)
    logits = jnp.where(lane < n_experts, logits, NEG_INF)

    def top(lg):
        best = jnp.max(lg, axis=1, keepdims=True)
        first = jnp.min(jnp.where(lg == best, lane, LANES), axis=1, keepdims=True)
        return best, first

    v1, e1 = top(logits)
    hot1 = lane == e1
    v2, e2 = top(jnp.where(hot1, NEG_INF, logits))
    hot2 = lane == e2
    ex = jnp.exp(v2 - v1)
    w1 = 1.0 / (1.0 + ex)
    w2 = ex / (1.0 + ex)

    assigned = (hot1 | hot2).astype(BF16)
    before = (_iota((tm, tm), 1) < _iota((tm, tm), 0)).astype(BF16)
    carry = carry_ref[0:1, :]
    rank = jnp.dot(before, assigned, preferred_element_type=F32) + carry
    r1 = jnp.sum(jnp.where(hot1, rank, 0.0), axis=1, keepdims=True).astype(I32)
    r2 = jnp.sum(jnp.where(hot2, rank, 0.0), axis=1, keepdims=True).astype(I32)
    total = carry + jnp.sum(assigned.astype(F32), axis=0, keepdims=True)
    carry_ref[...] = jnp.broadcast_to(total, carry_ref.shape)
    cnt_ref[...] = jnp.broadcast_to(total, cnt_ref.shape).astype(I32)

    ids_ref[...] = jnp.where(lane == 0, e1, jnp.where(lane == 1, e2,
                             jnp.where(lane == 2, r1, jnp.where(lane == 3, r2, 0))))
    wts_ref[...] = jnp.where(lane == 0, w1, jnp.where(lane == 1, w2, 0.0))


def _router(x2, router_w, router_b):
    T, D = x2.shape
    E = router_w.shape[1]
    tm = min(T, ROW_TILE)
    rw = jnp.pad(router_w, ((0, 0), (0, LANES - E)))
    rb = jnp.pad(router_b, (0, LANES - E))[None, :]
    row = lambda i: (i, 0)
    const = lambda i: (0, 0)
    return pl.pallas_call(
        functools.partial(_router_kernel, n_experts=E),
        grid=(T // tm,),
        in_specs=[pl.BlockSpec((tm, D), row),
                  pl.BlockSpec((D, LANES), const),
                  pl.BlockSpec((1, LANES), const)],
        out_specs=[pl.BlockSpec((tm, LANES), row),
                   pl.BlockSpec((tm, LANES), row),
                   pl.BlockSpec((SUBLANES, LANES), const)],
        out_shape=[jax.ShapeDtypeStruct((T, LANES), I32),
                   jax.ShapeDtypeStruct((T, LANES), F32),
                   jax.ShapeDtypeStruct((SUBLANES, LANES), I32)],
        scratch_shapes=[pltpu.VMEM((SUBLANES, LANES), F32)],
        compiler_params=_params("arbitrary"),
        name="router",
    )(x2, rw, rb)


def _slots_kernel(ids_ref, tiles_ref, out_ref, *, tile_rows):
    ids = ids_ref[...]
    lane = _iota(ids.shape, 1)
    tiles = tiles_ref[...]

    def slot(k):
        pick = jnp.where(lane == ids[:, k:k + 1], 1.0, 0.0).astype(BF16)
        start = jnp.dot(pick, tiles, preferred_element_type=F32)
        return start.astype(I32) * tile_rows + ids[:, TOP_K + k:TOP_K + k + 1]

    out_ref[...] = jnp.where(lane == 0, slot(0), jnp.where(lane == 1, slot(1), 0))


def _row_slots(ids, offs, tm, tile_rows, n_tiles):
    T = ids.shape[0]
    E = offs.shape[0]
    ts = min(T, SLOT_ROW_TILE)
    assert n_tiles <= 256, "start tiles must stay exact in bf16 (8-bit significand)"
    start_tile = jnp.pad(offs // tile_rows, (0, LANES - E)).astype(BF16)
    tiles = jnp.broadcast_to(start_tile[:, None], (LANES, LANES))
    slots = pl.pallas_call(
        functools.partial(_slots_kernel, tile_rows=tile_rows),
        grid=(T // ts,),
        in_specs=[pl.BlockSpec((ts, LANES), lambda i: (i, 0)),
                  pl.BlockSpec((LANES, LANES), lambda i: (0, 0))],
        out_specs=pl.BlockSpec((ts, LANES), lambda i: (i, 0)),
        out_shape=jax.ShapeDtypeStruct((T, LANES), I32),
        compiler_params=_params("parallel"),
        name="moe_slots",
    )(ids, tiles)
    return slots[:, :TOP_K].reshape(T // tm, tm, TOP_K).transpose(0, 2, 1)


def _for_each_row(tm, fn):
    def group(t, c):
        base = pl.multiple_of(t * DMA_UNROLL, DMA_UNROLL)
        for u in range(DMA_UNROLL):
            for k in range(TOP_K):
                fn(base + u, k)
        return c

    lax.fori_loop(0, tm // DMA_UNROLL, group, 0)


def _scatter_kernel(slot_ref, x_ref, xg_in_ref, xg_ref, sem):
    del xg_in_ref
    tm = x_ref.shape[0]

    def copy(r, k):
        return pltpu.make_async_copy(x_ref.at[pl.ds(r, 1)],
                                     xg_ref.at[pl.ds(slot_ref[0, k, r], 1)], sem)

    _for_each_row(tm, lambda r, k: copy(r, k).start())
    _for_each_row(tm, lambda r, k: copy(r, k).wait())


def _scatter_rows(x2, slots, xg0):
    T, D = x2.shape
    nt, _, tm = slots.shape
    n_rows = xg0.shape[0]
    return pl.pallas_call(
        _scatter_kernel,
        grid=(nt,),
        in_specs=[pl.BlockSpec((1, TOP_K, tm), lambda i: (i, 0, 0), memory_space=pltpu.SMEM),
                  pl.BlockSpec((tm, D), lambda i: (i, 0)),
                  pl.BlockSpec(memory_space=pl.ANY)],
        out_specs=pl.BlockSpec(memory_space=pl.ANY),
        out_shape=jax.ShapeDtypeStruct((n_rows, D), F32),
        scratch_shapes=[pltpu.SemaphoreType.DMA(())],
        input_output_aliases={2: 0},
        compiler_params=_params("arbitrary"),
        name="moe_scatter",
    )(slots, x2, xg0)


def _expert_kernel(te_ref, na_ref, x_ref, wg_ref, wu_ref, wd_ref, o_ref, xb_ref, acc_ref):
    g = pl.program_id(0)
    f = pl.program_id(1)
    active = g < na_ref[0]

    @pl.when(active & (f == 0))
    def _start():
        xb_ref[...] = x_ref[...].astype(BF16)
        acc_ref[...] = jnp.zeros_like(acc_ref)

    @pl.when(active)
    def _accumulate():
        acc_ref[...] += _swiglu_chunk(xb_ref[...], wg_ref[...], wu_ref[...], wd_ref[...])

    last = f == pl.num_programs(1) - 1

    @pl.when(active & last)
    def _finish():
        o_ref[...] = acc_ref[...]

    @pl.when(jnp.logical_not(active) & last)
    def _unused_tile():
        o_ref[...] = jnp.zeros_like(o_ref)


def _expert_ffn(xg, wg, wu, wd, tile_expert, n_active, tm):
    P, D = xg.shape
    E, _, F = wg.shape
    tf = _hidden_chunk(F, EXPERT_HIDDEN_CHUNK)
    nf = F // tf

    def tile(g, te, na):
        return jnp.maximum(jnp.minimum(g, na[0] - 1), 0)

    def chunk(g, f, na):
        return jnp.where(g < na[0], f, nf - 1)

    return pl.pallas_call(
        _expert_kernel,
        grid_spec=pltpu.PrefetchScalarGridSpec(
            num_scalar_prefetch=2,
            grid=(P // tm, nf),
            in_specs=[pl.BlockSpec((tm, D), lambda g, f, te, na: (tile(g, te, na), 0)),
                      pl.BlockSpec((None, D, tf),
                                   lambda g, f, te, na: (te[tile(g, te, na)], 0, chunk(g, f, na))),
                      pl.BlockSpec((None, D, tf),
                                   lambda g, f, te, na: (te[tile(g, te, na)], 0, chunk(g, f, na))),
                      pl.BlockSpec((None, tf, D),
                                   lambda g, f, te, na: (te[tile(g, te, na)], chunk(g, f, na), 0))],
            out_specs=pl.BlockSpec((tm, D), lambda g, f, te, na: (g, 0)),
            scratch_shapes=[pltpu.VMEM((tm, D), BF16), pltpu.VMEM((tm, D), F32)]),
        out_shape=jax.ShapeDtypeStruct((P, D), F32),
        compiler_params=_params("arbitrary", "arbitrary"),
        name="moe_experts",
    )(tile_expert, n_active, xg, wg, wu, wd)


def _combine_kernel(slot_ref, next_ref, yg_ref, wts_ref, x_ref, g_ref, b_ref, o_ref,
                    buf_ref, sem, *, alpha):
    tm = x_ref.shape[0]
    i = pl.program_id(0)
    cur = lax.rem(i, 2)

    def copy(table, buf, r, k):
        return pltpu.make_async_copy(yg_ref.at[pl.ds(table[0, k, r], 1)],
                                     buf_ref.at[buf, k, pl.ds(r, 1)], sem.at[buf])

    @pl.when(i == 0)
    def _first_tile():
        _for_each_row(tm, lambda r, k: copy(slot_ref, 0, r, k).start())

    @pl.when(i + 1 < pl.num_programs(0))
    def _next_tile():
        _for_each_row(tm, lambda r, k: copy(next_ref, 1 - cur, r, k).start())

    _for_each_row(tm, lambda r, k: copy(slot_ref, cur, r, k).wait())
    w = wts_ref[...]
    ffn = w[:, 0:1] * buf_ref[cur, 0] + w[:, 1:2] * buf_ref[cur, 1]
    o_ref[...] = _layer_norm(alpha * x_ref[...] + ffn, g_ref[...], b_ref[...])


def _combine(yg, slots, wts, x2, ln_g, ln_b, alpha):
    T, D = x2.shape
    nt, _, tm = slots.shape
    row = lambda i: (i, 0)
    const = lambda i: (0, 0)
    smem = dict(memory_space=pltpu.SMEM)
    return pl.pallas_call(
        functools.partial(_combine_kernel, alpha=alpha),
        grid=(nt,),
        in_specs=[pl.BlockSpec((1, TOP_K, tm), lambda i: (i, 0, 0), **smem),
                  pl.BlockSpec((1, TOP_K, tm), lambda i: (jnp.minimum(i + 1, nt - 1), 0, 0), **smem),
                  pl.BlockSpec(memory_space=pl.ANY),
                  pl.BlockSpec((tm, LANES), row),
                  pl.BlockSpec((tm, D), row),
                  pl.BlockSpec((1, D), const),
                  pl.BlockSpec((1, D), const)],
        out_specs=pl.BlockSpec((tm, D), row),
        out_shape=jax.ShapeDtypeStruct((T, D), F32),
        scratch_shapes=[pltpu.VMEM((2, TOP_K, tm, D), F32), pltpu.SemaphoreType.DMA((2,))],
        compiler_params=_params("arbitrary"),
        name="moe_combine_ln",
    )(slots, slots, yg, wts, x2, ln_g, ln_b)


def _moe_tile(T):
    return min(T, EXPERT_ROW_TILE)


def _moe_rows(T, E):
    return TOP_K * T + E * _moe_tile(T)


def _moe_ffn(x2, router_w, router_b, wg, wu, wd, ln_g, ln_b, alpha, xg0):
    T, D = x2.shape
    E = router_w.shape[1]
    tm_e = _moe_tile(T)
    tm_r = min(T, DMA_ROW_TILE)
    ids, wts, counts = _router(x2, router_w, router_b)

    cnt = counts[0, :E]
    padded = ((cnt + tm_e - 1) // tm_e) * tm_e
    ends = jnp.cumsum(padded)
    offs = (ends - padded).astype(I32)
    n_rows = xg0.shape[0]
    n_tiles = n_rows // tm_e
    tile_start = jnp.arange(n_tiles, dtype=I32) * tm_e
    tile_expert = jnp.minimum(jnp.sum(tile_start[:, None] >= ends[None, :], axis=1), E - 1).astype(I32)
    n_active = (ends[-1:] // tm_e).astype(I32)

    slots = _row_slots(ids, offs, tm_r, tm_e, n_tiles)
    xg = _scatter_rows(x2, slots, xg0)
    yg = _expert_ffn(xg, wg, wu, wd, tile_expert, n_active, tm_e)
    return _combine(yg, slots, wts, x2, ln_g, ln_b, alpha)


def kernel(x, positions, w_in, gate_bias, conv_w, conv_b, norm_attn, norm_mlstm, w_out,
           ln1_g, ln1_b, dense_w_gate, dense_w_up, dense_w_down, router_w, router_b,
           moe_w_gate, moe_w_up, moe_w_down, ln2_g, ln2_b):
    B, S, D = x.shape
    depth = w_in.shape[0]
    T = B * S
    alpha = (2.0 * depth) ** 0.25
    n_main = 3 * ATT_WIDTH + 2 * ML_QK_WIDTH + 2 * ML_WIDTH
    nblk = S // MOBA_BLOCK

    tabs = _rope_tables(positions)
    x2 = x.reshape(T, D)
    moe_bf16 = None
    for l in range(depth):
        w_main = w_in[l, :, :n_main].astype(BF16)
        w_gate = jnp.pad(w_in[l, :, n_main:], ((0, 0), (0, LANES - 2 * ML_HEADS))).astype(BF16)
        gbias = jnp.pad(gate_bias[l], (0, LANES - 2 * ML_HEADS))[None, :]
        q, k, v, mqk, mv, mo, gates, kmean = _inproj(x2, w_main, w_gate, gbias, tabs)
        kmean = jnp.pad(kmean.reshape(B, nblk, ATT_WIDTH), ((0, 0), (0, LANES - nblk), (0, 0)))
        ride = []
        if l % 2 == 0 and l + 1 < depth:
            ride = [w[l // 2].reshape(-1, w.shape[-1]) for w in (moe_w_gate, moe_w_up, moe_w_down)]
            if not all(_can_ride_attention(a, B, S) for a in ride):
                ride = []
        clear = []
        if l % 2 == 1:
            clear = [jax.ShapeDtypeStruct((_moe_rows(T, router_w.shape[-1]), D), F32)]
            if not _can_ride_attention(clear[0], B, S):
                clear = []
        att, rode, cleared = _moba_attention(q, k, v, kmean, norm_attn[l][None, :], B, S,
                                             ride, clear)
        if ride:
            moe_bf16 = [a.reshape(w.shape[1:]) for a, w in zip(rode, (moe_w_gate, moe_w_up, moe_w_down))]
        ml = _mlstm(mqk, mv, mo, gates, conv_w[l], conv_b[l][None, :],
                    norm_mlstm[l][None, :], B, S)
        j = l // 2
        if l % 2 == 0:
            x2 = _mixer_dense_ffn(att, ml, w_out[l].astype(BF16), x2, ln1_g[l][None, :],
                                  ln1_b[l][None, :], dense_w_gate[j].astype(BF16),
                                  dense_w_up[j].astype(BF16), dense_w_down[j].astype(BF16),
                                  ln2_g[l][None, :], ln2_b[l][None, :], alpha)
        else:
            x2 = _outproj(att, ml, w_out[l].astype(BF16), x2, ln1_g[l][None, :],
                          ln1_b[l][None, :], alpha)
            if moe_bf16 is None:
                moe_bf16 = [w[j].astype(BF16) for w in (moe_w_gate, moe_w_up, moe_w_down)]
            xg0 = cleared[0] if cleared else jnp.zeros((_moe_rows(T, router_w.shape[-1]), D), F32)
            x2 = _moe_ffn(x2, router_w[j], router_b[j], *moe_bf16,
                          ln2_g[l][None, :], ln2_b[l][None, :], alpha, xg0)
            moe_bf16 = None
    return x2.reshape(B, S, D)
```
